```python
import jax, jax.numpy as jnp
from jax import lax
import numpy as np

D_MODEL = 1024
BATCH = 8
SEQ = 4096
DEPTH = 1

N_HEADS = 16
N_KV_HEADS = 4
HEAD_DIM = 64
Q_PER_KV = N_HEADS // N_KV_HEADS
WINDOW = 128
BLOCK = 128
ATTN_WIDTH = N_HEADS * HEAD_DIM
KV_WIDTH = N_KV_HEADS * HEAD_DIM
GMLP_WIDTH = 1024
GMLP_GROUPS = 8
GMLP_GROUP_DIM = GMLP_WIDTH // GMLP_GROUPS
GMLP_CHUNK = 128
N_EXPERT_GROUPS = 4
EXPERTS_PER_GROUP = 4
N_EXPERTS = N_EXPERT_GROUPS * EXPERTS_PER_GROUP
TOP_K = 2
D_EXPERT = 512

N_BRANCHES = 2
IN_WIDTH = ATTN_WIDTH + 2 * KV_WIDTH + 2 * GMLP_WIDTH + N_BRANCHES * D_MODEL
SPLITS = (ATTN_WIDTH,
          ATTN_WIDTH + KV_WIDTH,
          ATTN_WIDTH + 2 * KV_WIDTH,
          ATTN_WIDTH + 2 * KV_WIDTH + GMLP_WIDTH,
          ATTN_WIDTH + 2 * KV_WIDTH + 2 * GMLP_WIDTH)
EPS = 1e-6
NEG_INF = -1e30

kernel_name = "hybrid_swa_gmlp_hmoe_adaln"


def alibi_slopes(n_heads):
    return np.array([2.0 ** (-8.0 * (i + 1) / n_heads) for i in range(n_heads)], dtype=np.float32)


def rmsnorm(x, gain):
    xf = x.astype(jnp.float32)
    y = xf * lax.rsqrt(jnp.mean(xf * xf, axis=-1, keepdims=True) + EPS)
    return y.astype(x.dtype) * gain


def layernorm(x, gain, bias):
    xf = x.astype(jnp.float32)
    mu = jnp.mean(xf, axis=-1, keepdims=True)
    var = jnp.mean(jnp.square(xf - mu), axis=-1, keepdims=True)
    y = (xf - mu) * lax.rsqrt(var + EPS)
    return y.astype(x.dtype) * gain + bias


def modulate(h, shift, scale):
    return h * (1.0 + scale[:, None, :]) + shift[:, None, :]


def sliding_window_attention(q, k, v, q_gain, k_gain, sinks):
    B, S = q.shape[0], q.shape[1]
    nb = S // BLOCK
    q = rmsnorm(q, q_gain)
    k = rmsnorm(k, k_gain)
    qb = q.reshape(B, nb, BLOCK, N_KV_HEADS, Q_PER_KV, HEAD_DIM)
    kb = k.reshape(B, nb, BLOCK, N_KV_HEADS, HEAD_DIM)
    vb = v.reshape(B, nb, BLOCK, N_KV_HEADS, HEAD_DIM)
    pad = ((0, 0), (1, 0), (0, 0), (0, 0), (0, 0))
    k_cat = jnp.concatenate([jnp.pad(kb, pad)[:, :-1], kb], axis=2)
    v_cat = jnp.concatenate([jnp.pad(vb, pad)[:, :-1], vb], axis=2)
    scores = jnp.einsum('bnqkgd,bnskd->bnkgqs', qb, k_cat).astype(jnp.float32) * (HEAD_DIM ** -0.5)
    q_idx = jnp.arange(BLOCK)[:, None] + BLOCK
    k_idx = jnp.arange(2 * BLOCK)[None, :]
    dist = q_idx - k_idx
    in_window = (dist >= 0) & (dist < WINDOW)
    blk = jnp.arange(nb)[:, None, None]
    valid = in_window[None] & ((blk > 0) | (k_idx >= BLOCK)[None])
    slopes = jnp.asarray(alibi_slopes(N_HEADS)).reshape(N_KV_HEADS, Q_PER_KV)
    scores = scores - slopes[None, None, :, :, None, None] * dist.astype(jnp.float32)[None, None, None, None]
    scores = jnp.where(valid[None, :, None, None], scores, NEG_INF)
    sink = sinks.astype(jnp.float32).reshape(N_KV_HEADS, Q_PER_KV)[None, None, :, :, None, None]
    sink = jnp.broadcast_to(sink, scores.shape[:-1] + (1,))
    probs = jax.nn.softmax(jnp.concatenate([scores, sink], axis=-1), axis=-1)[..., :-1]
    out = jnp.einsum('bnkgqs,bnskd->bnqkgd', probs.astype(v_cat.dtype), v_cat)
    return out.reshape(B, S, ATTN_WIDTH)


def chunked_spatial_gating(u, vg, norm_gain, norm_bias, w_spatial, b_spatial):
    B, S = u.shape[0], u.shape[1]
    nc = S // GMLP_CHUNK
    vg = layernorm(vg, norm_gain, norm_bias)
    vc = vg.reshape(B, nc, GMLP_CHUNK, GMLP_GROUPS, GMLP_GROUP_DIM)
    tril = jnp.tril(jnp.ones((GMLP_CHUNK, GMLP_CHUNK), dtype=bool))
    w = jnp.where(tril[None], w_spatial, 0.0)
    mixed = jnp.einsum('gts,bnsgc->bntgc', w, vc) + b_spatial.T[None, None, :, :, None]
    return u * mixed.reshape(B, S, GMLP_WIDTH)


def hierarchical_moe(h, w_group_router, b_group_router, w_expert_router, b_expert_router, w_gate, w_up, w_down):
    B, S, D = h.shape
    hf = h.reshape(B * S, D)
    g_logits = (hf @ w_group_router).astype(jnp.float32) + b_group_router
    g_prob = jax.nn.softmax(g_logits, axis=-1)
    g_idx = jnp.argmax(g_logits, axis=-1)
    g_weight = jnp.take_along_axis(g_prob, g_idx[:, None], axis=-1)
    e_logits = ((hf @ w_expert_router).astype(jnp.float32) + b_expert_router).reshape(-1, N_EXPERT_GROUPS, EXPERTS_PER_GROUP)
    e_in_group = jnp.take_along_axis(e_logits, g_idx[:, None, None], axis=1)[:, 0]
    top_vals, top_idx = lax.top_k(e_in_group, TOP_K)
    top_w = jax.nn.softmax(top_vals, axis=-1) * g_weight
    w_in_group = jnp.sum(jax.nn.one_hot(top_idx, EXPERTS_PER_GROUP, dtype=jnp.float32) * top_w[..., None], axis=1)
    combine = jax.nn.one_hot(g_idx, N_EXPERT_GROUPS, dtype=jnp.float32)[:, :, None] * w_in_group[:, None, :]
    combine = combine.astype(hf.dtype)
    out = jnp.zeros_like(hf)
    for gi in range(N_EXPERT_GROUPS):
        a = jnp.einsum('td,edf->tef', hf, w_gate[gi])
        b = jnp.einsum('td,edf->tef', hf, w_up[gi])
        hid = jax.nn.silu(a) * b * combine[:, gi, :, None]
        out = out + jnp.einsum('tef,efd->td', hid, w_down[gi])
    return out.reshape(B, S, D)


def setup_inputs(seed: int = 0) -> dict:
    key = jax.random.key(seed)
    ks = jax.random.split(key, 25)
    L, D = DEPTH, D_MODEL

    def nrm(k, shape, s):
        return jax.random.normal(k, shape, jnp.float32) * s

    return {
        "x": nrm(ks[0], (BATCH, SEQ, D), 1.0),
        "c": nrm(ks[1], (BATCH, D), 1.0),
        "w_ada": nrm(ks[2], (L, D, 6 * D), 0.2 * D ** -0.5),
        "b_ada": nrm(ks[3], (L, 6 * D), 0.02),
        "norm1_gain": 1.0 + nrm(ks[4], (L, D), 0.02),
        "w_in": nrm(ks[5], (L, D, IN_WIDTH), D ** -0.5),
        "b_branch_gate": nrm(ks[6], (L, N_BRANCHES * D), 0.02),
        "q_norm_gain": 1.0 + nrm(ks[7], (L, HEAD_DIM), 0.02),
        "k_norm_gain": 1.0 + nrm(ks[8], (L, HEAD_DIM), 0.02),
        "attn_sinks": nrm(ks[9], (L, N_HEADS), 0.5),
        "gmlp_norm_gain": 1.0 + nrm(ks[10], (L, GMLP_WIDTH), 0.02),
        "gmlp_norm_bias": nrm(ks[11], (L, GMLP_WIDTH), 0.02),
        "gmlp_w_spatial": nrm(ks[12], (L, GMLP_GROUPS, GMLP_CHUNK, GMLP_CHUNK), 0.5 * GMLP_CHUNK ** -0.5),
        "gmlp_b_spatial": 1.0 + nrm(ks[13], (L, GMLP_GROUPS, GMLP_CHUNK), 0.02),
        "w_o_attn": nrm(ks[14], (L, ATTN_WIDTH, D), ATTN_WIDTH ** -0.5),
        "w_o_gmlp": nrm(ks[15], (L, GMLP_WIDTH, D), GMLP_WIDTH ** -0.5),
        "w_out": nrm(ks[16], (L, D, D), D ** -0.5),
        "norm2_gain": 1.0 + nrm(ks[17], (L, D), 0.02),
        "w_group_router": nrm(ks[18], (L, D, N_EXPERT_GROUPS), D ** -0.5),
        "b_group_router": nrm(ks[19], (L, N_EXPERT_GROUPS), 0.01),
        "w_expert_router": nrm(ks[20], (L, D, N_EXPERTS), D ** -0.5),
        "b_expert_router": nrm(ks[21], (L, N_EXPERTS), 0.01),
        "w_expert_gate": nrm(ks[22], (L, N_EXPERT_GROUPS, EXPERTS_PER_GROUP, D, D_EXPERT), D ** -0.5),
        "w_expert_up": nrm(ks[23], (L, N_EXPERT_GROUPS, EXPERTS_PER_GROUP, D, D_EXPERT), D ** -0.5),
        "w_expert_down": nrm(ks[24], (L, N_EXPERT_GROUPS, EXPERTS_PER_GROUP, D_EXPERT, D), D_EXPERT ** -0.5),
    }


def reference(x, c, w_ada, b_ada, norm1_gain, w_in, b_branch_gate, q_norm_gain, k_norm_gain, attn_sinks,
              gmlp_norm_gain, gmlp_norm_bias, gmlp_w_spatial, gmlp_b_spatial, w_o_attn, w_o_gmlp, w_out,
              norm2_gain, w_group_router, b_group_router, w_expert_router, b_expert_router,
              w_expert_gate, w_expert_up, w_expert_down):
    B, S, _ = x.shape
    c_act = jax.nn.silu(c)
    for l in range(DEPTH):
        mod = c_act @ w_ada[l] + b_ada[l]
        shift1, scale1, gate1, shift2, scale2, gate2 = jnp.split(mod, 6, axis=-1)

        h = modulate(rmsnorm(x, norm1_gain[l]), shift1, scale1)
        proj = h @ w_in[l]
        q, k, v, u, vg, gate_logits = jnp.split(proj, SPLITS, axis=-1)
        q = q.reshape(B, S, N_HEADS, HEAD_DIM)
        k = k.reshape(B, S, N_KV_HEADS, HEAD_DIM)
        v = v.reshape(B, S, N_KV_HEADS, HEAD_DIM)
        y_attn = sliding_window_attention(q, k, v, q_norm_gain[l], k_norm_gain[l], attn_sinks[l])
        u = jax.nn.gelu(u, approximate=False)
        vg = jax.nn.gelu(vg, approximate=False)
        y_gmlp = chunked_spatial_gating(u, vg, gmlp_norm_gain[l], gmlp_norm_bias[l], gmlp_w_spatial[l], gmlp_b_spatial[l])
        gates = jax.nn.sigmoid(gate_logits + b_branch_gate[l])
        g_attn, g_gmlp = jnp.split(gates, N_BRANCHES, axis=-1)
        merged = g_attn * (y_attn @ w_o_attn[l]) + g_gmlp * (y_gmlp @ w_o_gmlp[l])
        x = x + gate1[:, None, :] * (merged @ w_out[l])

        h2 = modulate(rmsnorm(x, norm2_gain[l]), shift2, scale2)
        y_moe = hierarchical_moe(h2, w_group_router[l], b_group_router[l], w_expert_router[l], b_expert_router[l],
                                 w_expert_gate[l], w_expert_up[l], w_expert_down[l])
        x = x + gate2[:, None, :] * y_moe
    return x
```

```python
import functools

import jax
import jax.numpy as jnp
import numpy as np
from jax import lax
from jax.experimental import pallas as pl
from jax.experimental.pallas import tpu as pltpu

D_MODEL = 1024
BATCH = 8
SEQ = 4096
TOKENS = BATCH * SEQ
N_HEADS = 16
N_KV_HEADS = 4
HEAD_DIM = 64
Q_PER_KV = N_HEADS // N_KV_HEADS
BLOCK = 128
ATTN_WIDTH = N_HEADS * HEAD_DIM
KV_WIDTH = N_KV_HEADS * HEAD_DIM
GMLP_WIDTH = 1024
GMLP_GROUPS = 8
GMLP_CHUNK = 128
N_EXPERT_GROUPS = 4
EXPERTS_PER_GROUP = 4
N_EXPERTS = N_EXPERT_GROUPS * EXPERTS_PER_GROUP
D_EXPERT = 512
IN_WIDTH = ATTN_WIDTH + 2 * KV_WIDTH + 2 * GMLP_WIDTH + 2 * D_MODEL
EPS = 1e-6
NEG_INF = -1e30

Q0, Q1 = 0, ATTN_WIDTH
K0, K1 = Q1, Q1 + KV_WIDTH
V0, V1 = K1, K1 + KV_WIDTH
U0, U1 = V1, V1 + GMLP_WIDTH
G0, G1 = U1, U1 + GMLP_WIDTH
B0, B1 = G1, G1 + 2 * D_MODEL

PAIRS = ((0, 1), (0, 2), (0, 3), (1, 2), (1, 3), (2, 3))
N_BUCKETS = N_EXPERT_GROUPS * len(PAIRS)
BUCKET_ROWS = 32

TM_PROJ = 512
TM_ATTN = 512
TM_MIX = 512
TM_MOE = 256
TM_ROW = 512
NT_MOE = TOKENS // TM_MOE + N_BUCKETS
SORTED_ROWS = NT_MOE * TM_MOE
META_LANES = 256
assert NT_MOE <= META_LANES

VMEM_LIMIT = 56 * 1024 * 1024

F32 = jnp.float32
BF16 = jnp.bfloat16
U32 = jnp.uint32
I32 = jnp.int32


def _alibi_slopes():
    return [float(np.float32(2.0 ** (-8.0 * (i + 1) / N_HEADS))) for i in range(N_HEADS)]


def _params(n_axes):
    return pltpu.CompilerParams(dimension_semantics=("arbitrary",) * n_axes, vmem_limit_bytes=VMEM_LIMIT)


def _sigmoid(x):
    return 1.0 / (1.0 + jnp.exp(-x))


def _gelu(x):
    return 0.5 * x * (1.0 + lax.erf(x * np.float32(1.0 / np.sqrt(2.0))))


def _dot(a, b):
    return jnp.dot(a, b, preferred_element_type=F32)


def _dot_nt(a, b):
    return lax.dot_general(a, b, (((1,), (1,)), ((), ())), preferred_element_type=F32)


def _mod_kernel(c_ref, w_ref, b_ref, o_ref):
    c = c_ref[...]
    ca = c * _sigmoid(c)
    o_ref[...] = _dot(ca.astype(BF16), w_ref[...].astype(BF16)) + b_ref[...]


def _mod_call(c, w_ada, b_ada):
    n = w_ada.shape[1]
    bn = 1536
    return pl.pallas_call(
        _mod_kernel,
        grid=(n // bn,),
        in_specs=[
            pl.BlockSpec((BATCH, D_MODEL), lambda j: (0, 0)),
            pl.BlockSpec((D_MODEL, bn), lambda j: (0, j)),
            pl.BlockSpec((1, bn), lambda j: (0, j)),
        ],
        out_specs=pl.BlockSpec((BATCH, bn), lambda j: (0, j)),
        out_shape=jax.ShapeDtypeStruct((BATCH, n), F32),
        compiler_params=_params(1),
        name="mod",
    )(c, w_ada, b_ada.reshape(1, n))


def _inproj_kernel(x_ref, sc_ref, sh_ref, ng_ref, w_ref, qg_ref, kg_ref, lng_ref, lnb_ref, bg_ref, hp_ref, he_ref,
                   q_o, k_o, v_o, u_o, vg_o, gt_o):
    x = x_ref[...]
    ms = jnp.mean(x * x, axis=-1, keepdims=True)
    h = (x * lax.rsqrt(ms + EPS)) * ng_ref[...]
    h = h * (1.0 + sc_ref[0]) + sh_ref[0]
    hb = h.astype(BF16)

    def proj(c0, c1):
        return _dot(hb, w_ref[:, c0:c1])

    def head_rmsnorm(raw, pool, expand, gain):
        ssq = _dot((raw * raw).astype(BF16), pool)
        r = lax.rsqrt(ssq * (1.0 / HEAD_DIM) + EPS)
        r_hi = r.astype(BF16)
        r_lo = (r - r_hi.astype(F32)).astype(BF16)
        return raw * (_dot(r_hi, expand) + _dot(r_lo, expand)) * gain

    q_o[...] = head_rmsnorm(proj(Q0, Q1), hp_ref[...], he_ref[...], qg_ref[...]).astype(BF16)
    k_o[...] = head_rmsnorm(proj(K0, K1), hp_ref[0:KV_WIDTH, :], he_ref[:, 0:KV_WIDTH], kg_ref[...]).astype(BF16)
    v_o[...] = proj(V0, V1).astype(BF16)
    u_o[...] = _gelu(proj(U0, U1)).astype(BF16)
    vg = _gelu(proj(G0, G1))
    mu = jnp.mean(vg, axis=-1, keepdims=True)
    vc = vg - mu
    var = jnp.mean(vc * vc, axis=-1, keepdims=True)
    vg_o[...] = (vc * lax.rsqrt(var + EPS) * lng_ref[...] + lnb_ref[...]).astype(BF16)
    gt_o[...] = _sigmoid(proj(B0, B1) + bg_ref[...]).astype(BF16)


def _inproj_call(x2, scale1, shift1, norm_gain, w_in, q_gain_row, k_gain_row, ln_gain, ln_bias, b_gate, pool, expand):
    tm = TM_PROJ
    per_batch = SEQ // tm
    row = lambda n: pl.BlockSpec((1, n), lambda i: (0, 0))
    modrow = pl.BlockSpec((1, 1, D_MODEL), lambda i: (i // per_batch, 0, 0))
    tile = lambda n: pl.BlockSpec((tm, n), lambda i: (i, 0))
    out = lambda n: jax.ShapeDtypeStruct((TOKENS, n), BF16)
    return pl.pallas_call(
        _inproj_kernel,
        grid=(TOKENS // tm,),
        in_specs=[
            tile(D_MODEL), modrow, modrow, row(D_MODEL),
            pl.BlockSpec((D_MODEL, IN_WIDTH), lambda i: (0, 0)),
            row(ATTN_WIDTH), row(KV_WIDTH), row(GMLP_WIDTH), row(GMLP_WIDTH), row(2 * D_MODEL),
            pl.BlockSpec((ATTN_WIDTH, 128), lambda i: (0, 0)),
            pl.BlockSpec((128, ATTN_WIDTH), lambda i: (0, 0)),
        ],
        out_specs=[tile(ATTN_WIDTH), tile(KV_WIDTH), tile(KV_WIDTH), tile(GMLP_WIDTH), tile(GMLP_WIDTH),
                   tile(2 * D_MODEL)],
        out_shape=[out(ATTN_WIDTH), out(KV_WIDTH), out(KV_WIDTH), out(GMLP_WIDTH), out(GMLP_WIDTH),
                   out(2 * D_MODEL)],
        compiler_params=_params(1),
        name="inproj",
    )(x2, scale1, shift1, norm_gain, w_in, q_gain_row, k_gain_row, ln_gain, ln_bias, b_gate, pool, expand)


def _attn_kernel(sink_ref, q_ref, kc_ref, kp_ref, vc_ref, vp_ref, o_ref):
    i = pl.program_id(1)
    slopes = _alibi_slopes()
    qi = lax.broadcasted_iota(I32, (BLOCK, 2 * BLOCK), 0) + BLOCK
    kj = lax.broadcasted_iota(I32, (BLOCK, 2 * BLOCK), 1)
    dist = qi - kj
    in_window = (dist >= 0) & (dist < BLOCK)
    distf = dist.astype(F32)
    first_key = jnp.where(i > 0, 0, BLOCK)
    for sb in range(TM_ATTN // BLOCK):
        r0, r1 = sb * BLOCK, (sb + 1) * BLOCK
        if sb == 0:
            k_prev, v_prev = kp_ref[...], vp_ref[...]
            valid = in_window & (kj >= first_key)
        else:
            k_prev, v_prev = kc_ref[r0 - BLOCK:r0, :], vc_ref[r0 - BLOCK:r0, :]
            valid = in_window
        k_cat = jnp.concatenate([k_prev, kc_ref[r0:r1, :]], axis=0)
        v_cat = jnp.concatenate([v_prev, vc_ref[r0:r1, :]], axis=0)
        q_blk = q_ref[r0:r1, :]
        outs = []
        for kv in range(N_KV_HEADS):
            kh = k_cat[:, kv * HEAD_DIM:(kv + 1) * HEAD_DIM]
            vh = v_cat[:, kv * HEAD_DIM:(kv + 1) * HEAD_DIM]
            for g in range(Q_PER_KV):
                hd = kv * Q_PER_KV + g
                qh = q_blk[:, hd * HEAD_DIM:(hd + 1) * HEAD_DIM]
                s = _dot_nt(qh, kh) - slopes[hd] * distf
                s = jnp.where(valid, s, NEG_INF)
                sink = sink_ref[hd]
                m = jnp.maximum(jnp.max(s, axis=-1, keepdims=True), sink)
                p = jnp.exp(s - m)
                den = jnp.sum(p, axis=-1, keepdims=True) + jnp.exp(sink - m)
                outs.append(_dot(p.astype(BF16), vh) / den)
        o_ref[r0:r1, :] = jnp.concatenate(outs, axis=1).astype(BF16)


def _attn_call(sinks, q, k, v):
    tq = TM_ATTN
    per_batch = SEQ // tq
    blocks_per_tile = tq // BLOCK
    cur = lambda n: pl.BlockSpec((tq, n), lambda b, i: (b * per_batch + i, 0))
    prev = pl.BlockSpec(
        (BLOCK, KV_WIDTH),
        lambda b, i: (b * (SEQ // BLOCK) + jnp.maximum(i * blocks_per_tile - 1, 0), 0))
    return pl.pallas_call(
        _attn_kernel,
        grid=(BATCH, per_batch),
        in_specs=[
            pl.BlockSpec(memory_space=pltpu.SMEM),
            cur(ATTN_WIDTH), cur(KV_WIDTH), prev, cur(KV_WIDTH), prev,
        ],
        out_specs=cur(ATTN_WIDTH),
        out_shape=jax.ShapeDtypeStruct((TOKENS, ATTN_WIDTH), BF16),
        compiler_params=_params(2),
        name="attn",
    )(sinks, q, k, k, v, v)


def _first_argmax(vals):
    m = vals[0]
    for v in vals[1:]:
        m = jnp.maximum(m, v)
    idx = jnp.full(m.shape, len(vals) - 1, I32)
    for k in range(len(vals) - 2, -1, -1):
        idx = jnp.where(vals[k] == m, k, idx)
    return m, idx


def _mixer_kernel(x_ref, ya_ref, u_ref, vg_ref, gt_ref, ws_ref, bmap_ref, woa_ref, wog_ref, wout_ref, gate1_ref,
                  ng_ref, sc_ref, sh_ref, wrt_ref, brt_ref, x1_o, h2_o, bk_o):
    tm = TM_MIX
    t_idx = lax.broadcasted_iota(I32, (GMLP_CHUNK, GMLP_CHUNK), 0)
    s_idx = lax.broadcasted_iota(I32, (GMLP_CHUNK, GMLP_CHUNK), 1)
    ws = [jnp.where(t_idx >= s_idx, ws_ref[g], 0.0).astype(BF16) for g in range(GMLP_GROUPS)]
    gc = GMLP_WIDTH // GMLP_GROUPS
    chunks = []
    for c in range(tm // GMLP_CHUNK):
        r0, r1 = c * GMLP_CHUNK, (c + 1) * GMLP_CHUNK
        mixed = jnp.concatenate(
            [_dot(ws[g], vg_ref[r0:r1, g * gc:(g + 1) * gc]) for g in range(GMLP_GROUPS)], axis=1)
        mixed = mixed + bmap_ref[...]
        chunks.append((u_ref[r0:r1, :].astype(F32) * mixed).astype(BF16))
    y_gmlp = jnp.concatenate(chunks, axis=0)
    pa = _dot(ya_ref[...], woa_ref[...])
    pg = _dot(y_gmlp, wog_ref[...])
    merged = gt_ref[:, 0:D_MODEL].astype(F32) * pa + gt_ref[:, D_MODEL:2 * D_MODEL].astype(F32) * pg
    x1 = x_ref[...] + gate1_ref[0] * _dot(merged.astype(BF16), wout_ref[...])
    x1_o[...] = x1

    ms = jnp.mean(x1 * x1, axis=-1, keepdims=True)
    h2 = (x1 * lax.rsqrt(ms + EPS)) * ng_ref[...]
    h2 = h2 * (1.0 + sc_ref[0]) + sh_ref[0]
    h2_o[...] = h2

    lg = _dot_nt(wrt_ref[...], h2.astype(BF16)) + brt_ref[...]
    _, gi = _first_argmax([lg[r:r + 1, :] for r in range(N_EXPERT_GROUPS)])
    el = [lg[8 + r:9 + r, :] for r in range(N_EXPERTS)]
    eg = []
    for k in range(EXPERTS_PER_GROUP):
        v = el[(N_EXPERT_GROUPS - 1) * EXPERTS_PER_GROUP + k]
        for g in range(N_EXPERT_GROUPS - 2, -1, -1):
            v = jnp.where(gi == g, el[g * EXPERTS_PER_GROUP + k], v)
        eg.append(v)
    _, i1 = _first_argmax(eg)
    _, i2 = _first_argmax([jnp.where(i1 == k, -3e38, eg[k]) for k in range(EXPERTS_PER_GROUP)])
    lo = jnp.minimum(i1, i2)
    hi = jnp.maximum(i1, i2)
    pair = jnp.where(lo == 0, hi - 1, jnp.where(lo == 1, hi + 1, 5))
    bk_o[0] = gi * len(PAIRS) + pair


def _mixer_call(x2, y_attn, u, vg, gates, w_spatial, bias_map, wo_attn, wo_gmlp, w_out, gate1, norm_gain, scale2, shift2,
                wr_t, br_t):
    tm = TM_MIX
    per_batch = SEQ // tm
    nt = TOKENS // tm
    tile = lambda n: pl.BlockSpec((tm, n), lambda i: (i, 0))
    full = lambda shape: pl.BlockSpec(shape, lambda i: (0,) * len(shape))
    modrow = pl.BlockSpec((1, 1, D_MODEL), lambda i: (i // per_batch, 0, 0))
    return pl.pallas_call(
        _mixer_kernel,
        grid=(nt,),
        in_specs=[
            tile(D_MODEL), tile(ATTN_WIDTH), tile(GMLP_WIDTH), tile(GMLP_WIDTH), tile(2 * D_MODEL),
            full((GMLP_GROUPS, GMLP_CHUNK, GMLP_CHUNK)), full((GMLP_CHUNK, GMLP_WIDTH)),
            full((ATTN_WIDTH, D_MODEL)), full((GMLP_WIDTH, D_MODEL)), full((D_MODEL, D_MODEL)),
            modrow, full((1, D_MODEL)), modrow, modrow,
            full((BUCKET_ROWS, D_MODEL)), full((BUCKET_ROWS, 1)),
        ],
        out_specs=[tile(D_MODEL), tile(D_MODEL), pl.BlockSpec((1, 1, tm), lambda i: (i, 0, 0))],
        out_shape=[
            jax.ShapeDtypeStruct((TOKENS, D_MODEL), F32),
            jax.ShapeDtypeStruct((TOKENS, D_MODEL), F32),
            jax.ShapeDtypeStruct((nt, 1, tm), I32),
        ],
        compiler_params=_params(1),
        name="mixer",
    )(x2, y_attn, u, vg, gates, w_spatial, bias_map, wo_attn, wo_gmlp, w_out, gate1, norm_gain, scale2, shift2,
      wr_t, br_t)


def _rank_kernel(bk_ref, pos_o, meta_o, cnt, off):
    tm = TM_MIX
    ph = pl.program_id(0)
    i = pl.program_id(1)
    bk = bk_ref[0]
    sub = lax.broadcasted_iota(I32, (BUCKET_ROWS, tm), 0)
    onehot = sub == bk

    @pl.when((ph == 0) & (i == 0))
    def _():
        cnt[...] = jnp.zeros_like(cnt)

    @pl.when(ph == 0)
    def _():
        cnt[...] += jnp.sum(jnp.where(onehot, 1.0, 0.0), axis=1, keepdims=True)

    @pl.when((ph == 1) & (i == 0))
    def _():
        tiles = jnp.floor((cnt[...] + (TM_MOE - 1)) * (1.0 / TM_MOE))
        r = lax.broadcasted_iota(I32, (BUCKET_ROWS, BUCKET_ROWS), 0)
        c = lax.broadcasted_iota(I32, (BUCKET_ROWS, BUCKET_ROWS), 1)
        before = jnp.where(c < r, 1.0, 0.0).astype(BF16)
        first_tile = _dot(before, tiles.astype(BF16))
        off[...] = first_tile * TM_MOE
        cnt[...] = jnp.zeros_like(cnt)
        end_tile = (first_tile + tiles)[:, 0:1]
        tj = lax.broadcasted_iota(I32, (BUCKET_ROWS, META_LANES), 1).astype(F32)
        bsub = lax.broadcasted_iota(I32, (BUCKET_ROWS, META_LANES), 0)
        passed = jnp.where((tj >= end_tile) & (bsub < N_BUCKETS), 1.0, 0.0)
        tile_bucket = jnp.sum(passed, axis=0, keepdims=True)
        n_tiles = jnp.sum(jnp.where(bsub == N_BUCKETS - 1, jnp.broadcast_to(end_tile, passed.shape), 0.0),
                          axis=0, keepdims=True)
        row = lax.broadcasted_iota(I32, (8, META_LANES), 0)
        meta = jnp.where(row == 0, jnp.broadcast_to(tile_bucket, (8, META_LANES)),
                         jnp.where(row == 1, jnp.broadcast_to(n_tiles, (8, META_LANES)), 0.0))
        meta_o[...] = meta.astype(I32)

    @pl.when(ph == 1)
    def _():
        s_idx = lax.broadcasted_iota(I32, (tm, tm), 0)
        t_idx = lax.broadcasted_iota(I32, (tm, tm), 1)
        upto = jnp.where(s_idx <= t_idx, 1.0, 0.0).astype(BF16)
        incl = _dot(jnp.where(onehot, 1.0, 0.0).astype(BF16), upto)
        base = off[:, 0:1] + cnt[:, 0:1]
        posf = jnp.sum(jnp.where(onehot, base + incl - 1.0, 0.0), axis=0, keepdims=True)
        pos_o[0] = posf.astype(I32)
        cnt[...] += incl[:, tm - 1:tm]


def _rank_call(buckets):
    nt, _, tm = buckets.shape
    return pl.pallas_call(
        _rank_kernel,
        grid=(2, nt),
        in_specs=[pl.BlockSpec((1, 1, tm), lambda ph, i: (i, 0, 0))],
        out_specs=[
            pl.BlockSpec((1, 1, tm), lambda ph, i: (ph * i, 0, 0)),
            pl.BlockSpec((8, META_LANES), lambda ph, i: (0, 0)),
        ],
        out_shape=[
            jax.ShapeDtypeStruct((nt, 1, tm), I32),
            jax.ShapeDtypeStruct((8, META_LANES), I32),
        ],
        scratch_shapes=[pltpu.VMEM((BUCKET_ROWS, 128), F32), pltpu.VMEM((BUCKET_ROWS, 128), F32)],
        compiler_params=_params(2),
        name="rank",
    )(buckets)


def _row_copy(src, src_row, dst, dst_row, sem):
    return pltpu.make_async_copy(src.at[pl.ds(src_row, 1)], dst.at[pl.ds(dst_row, 1)], sem)


def _scatter_kernel(pos_ref, h_ref, init_ref, o_ref, sem):
    del init_ref

    def start(i, carry):
        _row_copy(h_ref, i, o_ref, pos_ref[0, 0, i], sem).start()
        return carry

    def wait(i, carry):
        _row_copy(h_ref, 0, o_ref, 0, sem).wait()
        return carry

    lax.fori_loop(0, TM_ROW, start, 0)
    lax.fori_loop(0, TM_ROW, wait, 0)


def _scatter_call(pos, h2):
    tm = TM_ROW
    width = h2.shape[1]
    init = jnp.zeros((SORTED_ROWS, width), F32)
    return pl.pallas_call(
        _scatter_kernel,
        grid=(TOKENS // tm,),
        in_specs=[
            pl.BlockSpec((1, 1, tm), lambda i: (i, 0, 0), memory_space=pltpu.SMEM),
            pl.BlockSpec((tm, width), lambda i: (i, 0)),
            pl.BlockSpec(memory_space=pl.ANY),
        ],
        out_specs=pl.BlockSpec(memory_space=pl.ANY),
        out_shape=jax.ShapeDtypeStruct((SORTED_ROWS, width), F32),
        scratch_shapes=[pltpu.SemaphoreType.DMA(())],
        input_output_aliases={2: 0},
        compiler_params=_params(1),
        name="scatter",
    )(pos, h2, init)


def _gather_kernel(pos_ref, ys_ref, x1_ref, gate2_ref, o_ref, buf, sem):
    def start(i, carry):
        _row_copy(ys_ref, pos_ref[0, 0, i], buf, i, sem).start()
        return carry

    def wait(i, carry):
        _row_copy(ys_ref, 0, buf, 0, sem).wait()
        return carry

    lax.fori_loop(0, TM_ROW, start, 0)
    lax.fori_loop(0, TM_ROW, wait, 0)
    o_ref[...] = x1_ref[...] + gate2_ref[0] * buf[...]


def _gather_call(pos, y_sorted, x1, gate2):
    tm = TM_ROW
    per_batch = SEQ // tm
    width = y_sorted.shape[1]
    return pl.pallas_call(
        _gather_kernel,
        grid=(TOKENS // tm,),
        in_specs=[
            pl.BlockSpec((1, 1, tm), lambda i: (i, 0, 0), memory_space=pltpu.SMEM),
            pl.BlockSpec(memory_space=pl.ANY),
            pl.BlockSpec((tm, D_MODEL), lambda i: (i, 0)),
            pl.BlockSpec((1, 1, D_MODEL), lambda i: (i // per_batch, 0, 0)),
        ],
        out_specs=pl.BlockSpec((tm, D_MODEL), lambda i: (i, 0)),
        out_shape=jax.ShapeDtypeStruct((TOKENS, D_MODEL), F32),
        scratch_shapes=[pltpu.VMEM((tm, width), F32), pltpu.SemaphoreType.DMA(())],
        compiler_params=_params(1),
        name="gather",
    )(pos, y_sorted, x1, gate2)


def _moe_kernel(tb_ref, nt_ref, tab_ref, h_ref, wr_ref, br_ref, wg_lo, wu_lo, wd_lo, wg_hi, wu_hi, wd_hi, o_ref):
    j = pl.program_id(0)

    @pl.when(j < nt_ref[0])
    def _():
        b = tb_ref[j]
        e_lo = tab_ref[b]
        e_hi = tab_ref[N_BUCKETS + b]
        g = tab_ref[2 * N_BUCKETS + b]
        h = h_ref[...].astype(BF16)
        lg = _dot(h, wr_ref[...]) + br_ref[...]
        lane = lax.broadcasted_iota(I32, lg.shape, 1)
        is_group = lane < N_EXPERT_GROUPS
        gmax = jnp.max(jnp.where(is_group, lg, NEG_INF), axis=1, keepdims=True)
        ge = jnp.exp(lg - gmax)
        g_w = (jnp.sum(jnp.where(lane == g, ge, 0.0), axis=1, keepdims=True)
               / jnp.sum(jnp.where(is_group, ge, 0.0), axis=1, keepdims=True))
        v_lo = jnp.sum(jnp.where(lane == N_EXPERT_GROUPS + e_lo, lg, 0.0), axis=1, keepdims=True)
        v_hi = jnp.sum(jnp.where(lane == N_EXPERT_GROUPS + e_hi, lg, 0.0), axis=1, keepdims=True)
        m = jnp.maximum(v_lo, v_hi)
        x_lo = jnp.exp(v_lo - m)
        x_hi = jnp.exp(v_hi - m)
        w_lo = x_lo / (x_lo + x_hi) * g_w
        w_hi = x_hi / (x_lo + x_hi) * g_w

        def hidden(wg, wu, w):
            a = _dot(h, wg[0])
            return (a * _sigmoid(a) * _dot(h, wu[0]) * w).astype(BF16)

        y = _dot(hidden(wg_lo, wu_lo, w_lo), wd_lo[0]) + _dot(hidden(wg_hi, wu_hi, w_hi), wd_hi[0])
        o_ref[...] = y

    @pl.when(j >= nt_ref[0])
    def _():
        o_ref[...] = jnp.zeros_like(o_ref)


def _moe_call(tile_bucket, n_tiles, tables, h_sorted, wr, br, w_gate, w_up, w_down):
    tm = TM_MOE
    width = h_sorted.shape[1]

    def last(j, nt):
        return jnp.minimum(j, nt[0] - 1)

    def w_spec(shape, which):
        return pl.BlockSpec(
            (1,) + shape, lambda j, tb, nt, tab: (tab[which * N_BUCKETS + tb[last(j, nt)]], 0, 0))

    up = (D_MODEL, D_EXPERT)
    down = (D_EXPERT, D_MODEL)
    grid_spec = pltpu.PrefetchScalarGridSpec(
        num_scalar_prefetch=3,
        grid=(NT_MOE,),
        in_specs=[
            pl.BlockSpec((tm, width), lambda j, tb, nt, tab: (last(j, nt), 0)),
            pl.BlockSpec((D_MODEL, 128), lambda j, tb, nt, tab: (0, 0)),
            pl.BlockSpec((1, 128), lambda j, tb, nt, tab: (0, 0)),
            w_spec(up, 0), w_spec(up, 0), w_spec(down, 0),
            w_spec(up, 1), w_spec(up, 1), w_spec(down, 1),
        ],
        out_specs=pl.BlockSpec((tm, width), lambda j, tb, nt, tab: (j, 0)),
    )
    return pl.pallas_call(
        _moe_kernel,
        grid_spec=grid_spec,
        out_shape=jax.ShapeDtypeStruct((SORTED_ROWS, width), F32),
        compiler_params=_params(1),
        name="moe",
    )(tile_bucket, n_tiles, tables, h_sorted, wr, br, w_gate, w_up, w_down, w_gate, w_up, w_down)


def _bucket_tables():
    e_lo, e_hi, grp = [], [], []
    for g in range(N_EXPERT_GROUPS):
        for lo, hi in PAIRS:
            e_lo.append(g * EXPERTS_PER_GROUP + lo)
            e_hi.append(g * EXPERTS_PER_GROUP + hi)
            grp.append(g)
    return jnp.asarray(e_lo + e_hi + grp, I32)


def kernel(x, c, w_ada, b_ada, norm1_gain, w_in, b_branch_gate, q_norm_gain, k_norm_gain, attn_sinks, gmlp_norm_gain, gmlp_norm_bias, gmlp_w_spatial, gmlp_b_spatial, w_o_attn, w_o_gmlp, w_out, norm2_gain, w_group_router, b_group_router, w_expert_router, b_expert_router, w_expert_gate, w_expert_up, w_expert_down):
    depth = w_ada.shape[0]
    x2 = x.reshape(TOKENS, D_MODEL)
    head_of_col = np.arange(ATTN_WIDTH) // HEAD_DIM
    pool = jnp.asarray(head_of_col[:, None] == np.arange(128)[None, :], BF16)
    expand = jnp.asarray(np.arange(128)[:, None] == head_of_col[None, :], BF16)
    tables = _bucket_tables()
    row = lambda v: v.reshape(1, -1)
    for l in range(depth):
        mod = _mod_call(c, w_ada[l], b_ada[l])
        shift1, scale1, gate1, shift2, scale2, gate2 = [
            m.reshape(BATCH, 1, D_MODEL) for m in jnp.split(mod, 6, axis=-1)]

        q_gain_row = row(jnp.tile(q_norm_gain[l], N_HEADS) * (HEAD_DIM ** -0.5))
        k_gain_row = row(jnp.tile(k_norm_gain[l], N_KV_HEADS))
        q, k, v, u, vg, gates = _inproj_call(
            x2, scale1, shift1, row(norm1_gain[l]), w_in[l].astype(BF16), q_gain_row, k_gain_row,
            row(gmlp_norm_gain[l]), row(gmlp_norm_bias[l]), row(b_branch_gate[l]), pool, expand)

        y_attn = _attn_call(attn_sinks[l], q, k, v)

        bias_map = jnp.repeat(gmlp_b_spatial[l].T, GMLP_WIDTH // GMLP_GROUPS, axis=1)
        wr_t = jnp.zeros((BUCKET_ROWS, D_MODEL), F32)
        wr_t = wr_t.at[0:N_EXPERT_GROUPS].set(w_group_router[l].T).at[8:8 + N_EXPERTS].set(w_expert_router[l].T)
        br_t = jnp.zeros((BUCKET_ROWS, 1), F32)
        br_t = br_t.at[0:N_EXPERT_GROUPS, 0].set(b_group_router[l]).at[8:8 + N_EXPERTS, 0].set(b_expert_router[l])
        x1, h2p, buckets = _mixer_call(
            x2, y_attn, u, vg, gates, gmlp_w_spatial[l], bias_map, w_o_attn[l].astype(BF16),
            w_o_gmlp[l].astype(BF16), w_out[l].astype(BF16), gate1, row(norm2_gain[l]), scale2, shift2,
            wr_t.astype(BF16), br_t)

        pos, meta = _rank_call(buckets)
        h_sorted = _scatter_call(pos, h2p)

        wr = jnp.zeros((D_MODEL, 128), F32)
        wr = wr.at[:, 0:N_EXPERT_GROUPS].set(w_group_router[l])
        wr = wr.at[:, N_EXPERT_GROUPS:N_EXPERT_GROUPS + N_EXPERTS].set(w_expert_router[l])
        br = jnp.zeros((1, 128), F32)
        br = br.at[0, 0:N_EXPERT_GROUPS].set(b_group_router[l])
        br = br.at[0, N_EXPERT_GROUPS:N_EXPERT_GROUPS + N_EXPERTS].set(b_expert_router[l])
        flat = lambda w: w.reshape((N_EXPERTS,) + w.shape[2:]).astype(BF16)
        y_sorted = _moe_call(
            meta[0], meta[1, 0:1], tables, h_sorted, wr.astype(BF16), br,
            flat(w_expert_gate[l]), flat(w_expert_up[l]), flat(w_expert_down[l]))

        x2 = _gather_call(pos, y_sorted, x1, gate2)
    return x2.reshape(x.shape)
```

```python
import functools

import jax
import jax.numpy as jnp
import numpy as np
from jax import lax
from jax.experimental import pallas as pl
from jax.experimental.pallas import tpu as pltpu

D_MODEL = 1024
BATCH = 8
SEQ = 4096
TOKENS = BATCH * SEQ
N_HEADS = 16
N_KV_HEADS = 4
HEAD_DIM = 64
Q_PER_KV = N_HEADS // N_KV_HEADS
BLOCK = 128
ATTN_WIDTH = N_HEADS * HEAD_DIM
KV_WIDTH = N_KV_HEADS * HEAD_DIM
GMLP_WIDTH = 1024
GMLP_GROUPS = 8
GMLP_CHUNK = 128
N_EXPERT_GROUPS = 4
EXPERTS_PER_GROUP = 4
N_EXPERTS = N_EXPERT_GROUPS * EXPERTS_PER_GROUP
D_EXPERT = 512
IN_WIDTH = ATTN_WIDTH + 2 * KV_WIDTH + 2 * GMLP_WIDTH + 2 * D_MODEL
EPS = 1e-6
NEG_INF = -1e30

Q0, Q1 = 0, ATTN_WIDTH
K0, K1 = Q1, Q1 + KV_WIDTH
V0, V1 = K1, K1 + KV_WIDTH
U0, U1 = V1, V1 + GMLP_WIDTH
G0, G1 = U1, U1 + GMLP_WIDTH
B0, B1 = G1, G1 + 2 * D_MODEL

PAIRS = ((0, 1), (0, 2), (0, 3), (1, 2), (1, 3), (2, 3))
N_BUCKETS = N_EXPERT_GROUPS * len(PAIRS)
BUCKET_ROWS = 32

TM_PROJ = 512
TM_ATTN = 512
TM_MIX = 512
TM_MOE = 256
TM_ROW = 512
ROW_UNROLL = 16
NT_MOE = TOKENS // TM_MOE + N_BUCKETS
SORTED_ROWS = NT_MOE * TM_MOE
META_LANES = 256
assert NT_MOE <= META_LANES

VMEM_LIMIT = 56 * 1024 * 1024

F32 = jnp.float32
BF16 = jnp.bfloat16
U32 = jnp.uint32
I32 = jnp.int32


def _alibi_slopes():
    return [float(np.float32(2.0 ** (-8.0 * (i + 1) / N_HEADS))) for i in range(N_HEADS)]


def _params(n_axes):
    return pltpu.CompilerParams(dimension_semantics=("arbitrary",) * n_axes, vmem_limit_bytes=VMEM_LIMIT)


def _sigmoid(x):
    return 1.0 / (1.0 + jnp.exp(-x))


def _gelu(x):
    return 0.5 * x * (1.0 + lax.erf(x * np.float32(1.0 / np.sqrt(2.0))))


def _dot(a, b):
    return jnp.dot(a, b, preferred_element_type=F32)


def _dot_nt(a, b):
    return lax.dot_general(a, b, (((1,), (1,)), ((), ())), preferred_element_type=F32)


def _mod_kernel(c_ref, w_ref, b_ref, o_ref):
    c = c_ref[...]
    ca = c * _sigmoid(c)
    o_ref[...] = _dot(ca.astype(BF16), w_ref[...].astype(BF16)) + b_ref[...]


def _mod_call(c, w_ada, b_ada):
    n = w_ada.shape[1]
    bn = 1536
    return pl.pallas_call(
        _mod_kernel,
        grid=(n // bn,),
        in_specs=[
            pl.BlockSpec((BATCH, D_MODEL), lambda j: (0, 0)),
            pl.BlockSpec((D_MODEL, bn), lambda j: (0, j)),
            pl.BlockSpec((1, bn), lambda j: (0, j)),
        ],
        out_specs=pl.BlockSpec((BATCH, bn), lambda j: (0, j)),
        out_shape=jax.ShapeDtypeStruct((BATCH, n), F32),
        compiler_params=_params(1),
        name="mod",
    )(c, w_ada, b_ada.reshape(1, n))


def _inproj_kernel(x_ref, sc_ref, sh_ref, ng_ref, w_ref, qg_ref, kg_ref, lng_ref, lnb_ref, bg_ref, hp_ref, he_ref,
                   q_o, k_o, v_o, u_o, vg_o, gt_o):
    x = x_ref[...]
    ms = jnp.mean(x * x, axis=-1, keepdims=True)
    h = (x * lax.rsqrt(ms + EPS)) * ng_ref[...]
    h = h * (1.0 + sc_ref[0]) + sh_ref[0]
    hb = h.astype(BF16)

    def proj(c0, c1):
        return _dot(hb, w_ref[:, c0:c1])

    def head_rmsnorm(raw, pool, expand, gain):
        ssq = _dot((raw * raw).astype(BF16), pool)
        r = lax.rsqrt(ssq * (1.0 / HEAD_DIM) + EPS)
        r_hi = r.astype(BF16)
        r_lo = (r - r_hi.astype(F32)).astype(BF16)
        return raw * (_dot(r_hi, expand) + _dot(r_lo, expand)) * gain

    q_o[...] = head_rmsnorm(proj(Q0, Q1), hp_ref[...], he_ref[...], qg_ref[...]).astype(BF16)
    k_o[...] = head_rmsnorm(proj(K0, K1), hp_ref[0:KV_WIDTH, :], he_ref[:, 0:KV_WIDTH], kg_ref[...]).astype(BF16)
    v_o[...] = proj(V0, V1).astype(BF16)
    u_o[...] = _gelu(proj(U0, U1)).astype(BF16)
    vg = _gelu(proj(G0, G1))
    mu = jnp.mean(vg, axis=-1, keepdims=True)
    vc = vg - mu
    var = jnp.mean(vc * vc, axis=-1, keepdims=True)
    vg_o[...] = (vc * lax.rsqrt(var + EPS) * lng_ref[...] + lnb_ref[...]).astype(BF16)
    gt_o[...] = _sigmoid(proj(B0, B1) + bg_ref[...]).astype(BF16)


def _inproj_call(x2, scale1, shift1, norm_gain, w_in, q_gain_row, k_gain_row, ln_gain, ln_bias, b_gate, pool, expand):
    tm = TM_PROJ
    per_batch = SEQ // tm
    row = lambda n: pl.BlockSpec((1, n), lambda i: (0, 0))
    modrow = pl.BlockSpec((1, 1, D_MODEL), lambda i: (i // per_batch, 0, 0))
    tile = lambda n: pl.BlockSpec((tm, n), lambda i: (i, 0))
    out = lambda n: jax.ShapeDtypeStruct((TOKENS, n), BF16)
    return pl.pallas_call(
        _inproj_kernel,
        grid=(TOKENS // tm,),
        in_specs=[
            tile(D_MODEL), modrow, modrow, row(D_MODEL),
            pl.BlockSpec((D_MODEL, IN_WIDTH), lambda i: (0, 0)),
            row(ATTN_WIDTH), row(KV_WIDTH), row(GMLP_WIDTH), row(GMLP_WIDTH), row(2 * D_MODEL),
            pl.BlockSpec((ATTN_WIDTH, 128), lambda i: (0, 0)),
            pl.BlockSpec((128, ATTN_WIDTH), lambda i: (0, 0)),
        ],
        out_specs=[tile(ATTN_WIDTH), tile(KV_WIDTH), tile(KV_WIDTH), tile(GMLP_WIDTH), tile(GMLP_WIDTH),
                   tile(2 * D_MODEL)],
        out_shape=[out(ATTN_WIDTH), out(KV_WIDTH), out(KV_WIDTH), out(GMLP_WIDTH), out(GMLP_WIDTH),
                   out(2 * D_MODEL)],
        compiler_params=_params(1),
        name="inproj",
    )(x2, scale1, shift1, norm_gain, w_in, q_gain_row, k_gain_row, ln_gain, ln_bias, b_gate, pool, expand)


def _attn_kernel(sink_ref, q_ref, kc_ref, kp_ref, vc_ref, vp_ref, o_ref):
    i = pl.program_id(1)
    slopes = _alibi_slopes()
    qi = lax.broadcasted_iota(I32, (BLOCK, 2 * BLOCK), 0) + BLOCK
    kj = lax.broadcasted_iota(I32, (BLOCK, 2 * BLOCK), 1)
    dist = qi - kj
    in_window = (dist >= 0) & (dist < BLOCK)
    distf = dist.astype(F32)
    first_key = jnp.where(i > 0, 0, BLOCK)
    for sb in range(TM_ATTN // BLOCK):
        r0, r1 = sb * BLOCK, (sb + 1) * BLOCK
        if sb == 0:
            k_prev, v_prev = kp_ref[...], vp_ref[...]
            valid = in_window & (kj >= first_key)
        else:
            k_prev, v_prev = kc_ref[r0 - BLOCK:r0, :], vc_ref[r0 - BLOCK:r0, :]
            valid = in_window
        k_cat = jnp.concatenate([k_prev, kc_ref[r0:r1, :]], axis=0)
        v_cat = jnp.concatenate([v_prev, vc_ref[r0:r1, :]], axis=0)
        q_blk = q_ref[r0:r1, :]
        outs = []
        for kv in range(N_KV_HEADS):
            kh = k_cat[:, kv * HEAD_DIM:(kv + 1) * HEAD_DIM]
            vh = v_cat[:, kv * HEAD_DIM:(kv + 1) * HEAD_DIM]
            for g in range(Q_PER_KV):
                hd = kv * Q_PER_KV + g
                qh = q_blk[:, hd * HEAD_DIM:(hd + 1) * HEAD_DIM]
                s = _dot_nt(qh, kh) - slopes[hd] * distf
                s = jnp.where(valid, s, NEG_INF)
                sink = sink_ref[hd]
                m = jnp.maximum(jnp.max(s, axis=-1, keepdims=True), sink)
                p = jnp.exp(s - m)
                den = jnp.sum(p, axis=-1, keepdims=True) + jnp.exp(sink - m)
                outs.append(_dot(p.astype(BF16), vh) / den)
        o_ref[r0:r1, :] = jnp.concatenate(outs, axis=1).astype(BF16)


def _attn_call(sinks, q, k, v):
    tq = TM_ATTN
    per_batch = SEQ // tq
    blocks_per_tile = tq // BLOCK
    cur = lambda n: pl.BlockSpec((tq, n), lambda b, i: (b * per_batch + i, 0))
    prev = pl.BlockSpec(
        (BLOCK, KV_WIDTH),
        lambda b, i: (b * (SEQ // BLOCK) + jnp.maximum(i * blocks_per_tile - 1, 0), 0))
    return pl.pallas_call(
        _attn_kernel,
        grid=(BATCH, per_batch),
        in_specs=[
            pl.BlockSpec(memory_space=pltpu.SMEM),
            cur(ATTN_WIDTH), cur(KV_WIDTH), prev, cur(KV_WIDTH), prev,
        ],
        out_specs=cur(ATTN_WIDTH),
        out_shape=jax.ShapeDtypeStruct((TOKENS, ATTN_WIDTH), BF16),
        compiler_params=_params(2),
        name="attn",
    )(sinks, q, k, k, v, v)


def _first_argmax(vals):
    m = vals[0]
    for v in vals[1:]:
        m = jnp.maximum(m, v)
    idx = jnp.full(m.shape, len(vals) - 1, I32)
    for k in range(len(vals) - 2, -1, -1):
        idx = jnp.where(vals[k] == m, k, idx)
    return m, idx


def _mixer_kernel(x_ref, ya_ref, u_ref, vg_ref, gt_ref, ws_ref, bmap_ref, woa_ref, wog_ref, wout_ref, gate1_ref,
                  ng_ref, sc_ref, sh_ref, wrt_ref, brt_ref, x1_o, h2_o, bk_o):
    tm = TM_MIX
    t_idx = lax.broadcasted_iota(I32, (GMLP_CHUNK, GMLP_CHUNK), 0)
    s_idx = lax.broadcasted_iota(I32, (GMLP_CHUNK, GMLP_CHUNK), 1)
    ws = [jnp.where(t_idx >= s_idx, ws_ref[g], 0.0).astype(BF16) for g in range(GMLP_GROUPS)]
    gc = GMLP_WIDTH // GMLP_GROUPS
    chunks = []
    for c in range(tm // GMLP_CHUNK):
        r0, r1 = c * GMLP_CHUNK, (c + 1) * GMLP_CHUNK
        mixed = jnp.concatenate(
            [_dot(ws[g], vg_ref[r0:r1, g * gc:(g + 1) * gc]) for g in range(GMLP_GROUPS)], axis=1)
        mixed = mixed + bmap_ref[...]
        chunks.append((u_ref[r0:r1, :].astype(F32) * mixed).astype(BF16))
    y_gmlp = jnp.concatenate(chunks, axis=0)
    pa = _dot(ya_ref[...], woa_ref[...])
    pg = _dot(y_gmlp, wog_ref[...])
    merged = gt_ref[:, 0:D_MODEL].astype(F32) * pa + gt_ref[:, D_MODEL:2 * D_MODEL].astype(F32) * pg
    x1 = x_ref[...] + gate1_ref[0] * _dot(merged.astype(BF16), wout_ref[...])
    x1_o[...] = x1

    ms = jnp.mean(x1 * x1, axis=-1, keepdims=True)
    h2 = (x1 * lax.rsqrt(ms + EPS)) * ng_ref[...]
    h2 = h2 * (1.0 + sc_ref[0]) + sh_ref[0]
    h2_o[...] = h2

    lg = _dot_nt(wrt_ref[...], h2.astype(BF16)) + brt_ref[...]
    _, gi = _first_argmax([lg[r:r + 1, :] for r in range(N_EXPERT_GROUPS)])
    el = [lg[8 + r:9 + r, :] for r in range(N_EXPERTS)]
    eg = []
    for k in range(EXPERTS_PER_GROUP):
        v = el[(N_EXPERT_GROUPS - 1) * EXPERTS_PER_GROUP + k]
        for g in range(N_EXPERT_GROUPS - 2, -1, -1):
            v = jnp.where(gi == g, el[g * EXPERTS_PER_GROUP + k], v)
        eg.append(v)
    _, i1 = _first_argmax(eg)
    _, i2 = _first_argmax([jnp.where(i1 == k, -3e38, eg[k]) for k in range(EXPERTS_PER_GROUP)])
    lo = jnp.minimum(i1, i2)
    hi = jnp.maximum(i1, i2)
    pair = jnp.where(lo == 0, hi - 1, jnp.where(lo == 1, hi + 1, 5))
    bk_o[0] = gi * len(PAIRS) + pair


def _mixer_call(x2, y_attn, u, vg, gates, w_spatial, bias_map, wo_attn, wo_gmlp, w_out, gate1, norm_gain, scale2, shift2,
                wr_t, br_t):
    tm = TM_MIX
    per_batch = SEQ // tm
    nt = TOKENS // tm
    tile = lambda n: pl.BlockSpec((tm, n), lambda i: (i, 0))
    full = lambda shape: pl.BlockSpec(shape, lambda i: (0,) * len(shape))
    modrow = pl.BlockSpec((1, 1, D_MODEL), lambda i: (i // per_batch, 0, 0))
    return pl.pallas_call(
        _mixer_kernel,
        grid=(nt,),
        in_specs=[
            tile(D_MODEL), tile(ATTN_WIDTH), tile(GMLP_WIDTH), tile(GMLP_WIDTH), tile(2 * D_MODEL),
            full((GMLP_GROUPS, GMLP_CHUNK, GMLP_CHUNK)), full((GMLP_CHUNK, GMLP_WIDTH)),
            full((ATTN_WIDTH, D_MODEL)), full((GMLP_WIDTH, D_MODEL)), full((D_MODEL, D_MODEL)),
            modrow, full((1, D_MODEL)), modrow, modrow,
            full((BUCKET_ROWS, D_MODEL)), full((BUCKET_ROWS, 1)),
        ],
        out_specs=[tile(D_MODEL), tile(D_MODEL), pl.BlockSpec((1, 1, tm), lambda i: (i, 0, 0))],
        out_shape=[
            jax.ShapeDtypeStruct((TOKENS, D_MODEL), F32),
            jax.ShapeDtypeStruct((TOKENS, D_MODEL), F32),
            jax.ShapeDtypeStruct((nt, 1, tm), I32),
        ],
        compiler_params=_params(1),
        name="mixer",
    )(x2, y_attn, u, vg, gates, w_spatial, bias_map, wo_attn, wo_gmlp, w_out, gate1, norm_gain, scale2, shift2,
      wr_t, br_t)


def _rank_kernel(bk_ref, pos_o, meta_o, cnt, off):
    tm = TM_MIX
    ph = pl.program_id(0)
    i = pl.program_id(1)
    bk = bk_ref[0]
    sub = lax.broadcasted_iota(I32, (BUCKET_ROWS, tm), 0)
    onehot = sub == bk

    @pl.when((ph == 0) & (i == 0))
    def _():
        cnt[...] = jnp.zeros_like(cnt)

    @pl.when(ph == 0)
    def _():
        cnt[...] += jnp.sum(jnp.where(onehot, 1.0, 0.0), axis=1, keepdims=True)

    @pl.when((ph == 1) & (i == 0))
    def _():
        tiles = jnp.floor((cnt[...] + (TM_MOE - 1)) * (1.0 / TM_MOE))
        r = lax.broadcasted_iota(I32, (BUCKET_ROWS, BUCKET_ROWS), 0)
        c = lax.broadcasted_iota(I32, (BUCKET_ROWS, BUCKET_ROWS), 1)
        before = jnp.where(c < r, 1.0, 0.0).astype(BF16)
        first_tile = _dot(before, tiles.astype(BF16))
        off[...] = first_tile * TM_MOE
        cnt[...] = jnp.zeros_like(cnt)
        end_tile = (first_tile + tiles)[:, 0:1]
        tj = lax.broadcasted_iota(I32, (BUCKET_ROWS, META_LANES), 1).astype(F32)
        bsub = lax.broadcasted_iota(I32, (BUCKET_ROWS, META_LANES), 0)
        passed = jnp.where((tj >= end_tile) & (bsub < N_BUCKETS), 1.0, 0.0)
        tile_bucket = jnp.sum(passed, axis=0, keepdims=True)
        n_tiles = jnp.sum(jnp.where(bsub == N_BUCKETS - 1, jnp.broadcast_to(end_tile, passed.shape), 0.0),
                          axis=0, keepdims=True)
        row = lax.broadcasted_iota(I32, (8, META_LANES), 0)
        meta = jnp.where(row == 0, jnp.broadcast_to(tile_bucket, (8, META_LANES)),
                         jnp.where(row == 1, jnp.broadcast_to(n_tiles, (8, META_LANES)), 0.0))
        meta_o[...] = meta.astype(I32)

    @pl.when(ph == 1)
    def _():
        s_idx = lax.broadcasted_iota(I32, (tm, tm), 0)
        t_idx = lax.broadcasted_iota(I32, (tm, tm), 1)
        upto = jnp.where(s_idx <= t_idx, 1.0, 0.0).astype(BF16)
        incl = _dot(jnp.where(onehot, 1.0, 0.0).astype(BF16), upto)
        base = off[:, 0:1] + cnt[:, 0:1]
        posf = jnp.sum(jnp.where(onehot, base + incl - 1.0, 0.0), axis=0, keepdims=True)
        pos_o[0] = posf.astype(I32)
        cnt[...] += incl[:, tm - 1:tm]


def _rank_call(buckets):
    nt, _, tm = buckets.shape
    return pl.pallas_call(
        _rank_kernel,
        grid=(2, nt),
        in_specs=[pl.BlockSpec((1, 1, tm), lambda ph, i: (i, 0, 0))],
        out_specs=[
            pl.BlockSpec((1, 1, tm), lambda ph, i: (ph * i, 0, 0)),
            pl.BlockSpec((8, META_LANES), lambda ph, i: (0, 0)),
        ],
        out_shape=[
            jax.ShapeDtypeStruct((nt, 1, tm), I32),
            jax.ShapeDtypeStruct((8, META_LANES), I32),
        ],
        scratch_shapes=[pltpu.VMEM((BUCKET_ROWS, 128), F32), pltpu.VMEM((BUCKET_ROWS, 128), F32)],
        compiler_params=_params(2),
        name="rank",
    )(buckets)


def _row_copy(src, src_row, dst, dst_row, sem):
    return pltpu.make_async_copy(src.at[pl.ds(src_row, 1)], dst.at[pl.ds(dst_row, 1)], sem)


def _tile_wait(src, dst, sem):
    pltpu.make_async_copy(src.at[pl.ds(0, TM_ROW)], dst, sem).wait()


def _scatter_kernel(pos_ref, h_ref, init_ref, o_ref, sem):
    del init_ref
    i = pl.program_id(0)
    base = i * TM_ROW

    def start(r, carry):
        _row_copy(h_ref, base + r, o_ref, pos_ref[0, 0, r], sem).start()
        return carry

    lax.fori_loop(0, TM_ROW, start, 0, unroll=ROW_UNROLL)

    @pl.when(i > 0)
    def _():
        _tile_wait(h_ref, o_ref.at[pl.ds(0, TM_ROW)], sem)

    @pl.when(i == pl.num_programs(0) - 1)
    def _():
        _tile_wait(h_ref, o_ref.at[pl.ds(0, TM_ROW)], sem)


def _scatter_call(pos, h2):
    tm = TM_ROW
    width = h2.shape[1]
    init = jnp.zeros((SORTED_ROWS, width), F32)
    return pl.pallas_call(
        _scatter_kernel,
        grid=(TOKENS // tm,),
        in_specs=[
            pl.BlockSpec((1, 1, tm), lambda i: (i, 0, 0), memory_space=pltpu.SMEM),
            pl.BlockSpec(memory_space=pl.ANY),
            pl.BlockSpec(memory_space=pl.ANY),
        ],
        out_specs=pl.BlockSpec(memory_space=pl.ANY),
        out_shape=jax.ShapeDtypeStruct((SORTED_ROWS, width), F32),
        scratch_shapes=[pltpu.SemaphoreType.DMA(())],
        input_output_aliases={2: 0},
        compiler_params=_params(1),
        name="scatter",
    )(pos, h2, init)


def _gather_kernel(pos_ref, pos_next_ref, ys_ref, x1_ref, gate2_ref, o_ref, buf, sem):
    i = pl.program_id(0)
    n = pl.num_programs(0)
    slot = lax.rem(i, 2)

    def fetch(p_ref, s):
        def start(r, carry):
            _row_copy(ys_ref, p_ref[0, 0, r], buf.at[s], r, sem.at[s]).start()
            return carry
        lax.fori_loop(0, TM_ROW, start, 0, unroll=ROW_UNROLL)

    @pl.when(i == 0)
    def _():
        fetch(pos_ref, 0)

    @pl.when(i + 1 < n)
    def _():
        fetch(pos_next_ref, 1 - slot)

    _tile_wait(ys_ref, buf.at[slot], sem.at[slot])
    o_ref[...] = x1_ref[...] + gate2_ref[0] * buf[slot]


def _gather_call(pos, y_sorted, x1, gate2):
    tm = TM_ROW
    per_batch = SEQ // tm
    n = TOKENS // tm
    width = y_sorted.shape[1]
    return pl.pallas_call(
        _gather_kernel,
        grid=(n,),
        in_specs=[
            pl.BlockSpec((1, 1, tm), lambda i: (i, 0, 0), memory_space=pltpu.SMEM),
            pl.BlockSpec((1, 1, tm), lambda i: (jnp.minimum(i + 1, n - 1), 0, 0), memory_space=pltpu.SMEM),
            pl.BlockSpec(memory_space=pl.ANY),
            pl.BlockSpec((tm, D_MODEL), lambda i: (i, 0)),
            pl.BlockSpec((1, 1, D_MODEL), lambda i: (i // per_batch, 0, 0)),
        ],
        out_specs=pl.BlockSpec((tm, D_MODEL), lambda i: (i, 0)),
        out_shape=jax.ShapeDtypeStruct((TOKENS, D_MODEL), F32),
        scratch_shapes=[pltpu.VMEM((2, tm, width), F32), pltpu.SemaphoreType.DMA((2,))],
        compiler_params=_params(1),
        name="gather",
    )(pos, pos, y_sorted, x1, gate2)


def _moe_kernel(tb_ref, nt_ref, tab_ref, h_ref, wr_ref, br_ref, wg_lo, wu_lo, wd_lo, wg_hi, wu_hi, wd_hi, o_ref):
    j = pl.program_id(0)

    @pl.when(j < nt_ref[0])
    def _():
        b = tb_ref[j]
        e_lo = tab_ref[b]
        e_hi = tab_ref[N_BUCKETS + b]
        g = tab_ref[2 * N_BUCKETS + b]
        h = h_ref[...].astype(BF16)
        lg = _dot(h, wr_ref[...]) + br_ref[...]
        lane = lax.broadcasted_iota(I32, lg.shape, 1)
        is_group = lane < N_EXPERT_GROUPS
        gmax = jnp.max(jnp.where(is_group, lg, NEG_INF), axis=1, keepdims=True)
        ge = jnp.exp(lg - gmax)
        g_w = (jnp.sum(jnp.where(lane == g, ge, 0.0), axis=1, keepdims=True)
               / jnp.sum(jnp.where(is_group, ge, 0.0), axis=1, keepdims=True))
        v_lo = jnp.sum(jnp.where(lane == N_EXPERT_GROUPS + e_lo, lg, 0.0), axis=1, keepdims=True)
        v_hi = jnp.sum(jnp.where(lane == N_EXPERT_GROUPS + e_hi, lg, 0.0), axis=1, keepdims=True)
        m = jnp.maximum(v_lo, v_hi)
        x_lo = jnp.exp(v_lo - m)
        x_hi = jnp.exp(v_hi - m)
        w_lo = x_lo / (x_lo + x_hi) * g_w
        w_hi = x_hi / (x_lo + x_hi) * g_w

        def hidden(wg, wu, w):
            a = _dot(h, wg[0])
            return (a * _sigmoid(a) * _dot(h, wu[0]) * w).astype(BF16)

        y = _dot(hidden(wg_lo, wu_lo, w_lo), wd_lo[0]) + _dot(hidden(wg_hi, wu_hi, w_hi), wd_hi[0])
        o_ref[...] = y

    @pl.when(j >= nt_ref[0])
    def _():
        o_ref[...] = jnp.zeros_like(o_ref)


def _moe_call(tile_bucket, n_tiles, tables, h_sorted, wr, br, w_gate, w_up, w_down):
    tm = TM_MOE
    width = h_sorted.shape[1]

    def last(j, nt):
        return jnp.minimum(j, nt[0] - 1)

    def w_spec(shape, which):
        return pl.BlockSpec(
            (1,) + shape, lambda j, tb, nt, tab: (tab[which * N_BUCKETS + tb[last(j, nt)]], 0, 0))

    up = (D_MODEL, D_EXPERT)
    down = (D_EXPERT, D_MODEL)
    grid_spec = pltpu.PrefetchScalarGridSpec(
        num_scalar_prefetch=3,
        grid=(NT_MOE,),
        in_specs=[
            pl.BlockSpec((tm, width), lambda j, tb, nt, tab: (last(j, nt), 0)),
            pl.BlockSpec((D_MODEL, 128), lambda j, tb, nt, tab: (0, 0)),
            pl.BlockSpec((1, 128), lambda j, tb, nt, tab: (0, 0)),
            w_spec(up, 0), w_spec(up, 0), w_spec(down, 0),
            w_spec(up, 1), w_spec(up, 1), w_spec(down, 1),
        ],
        out_specs=pl.BlockSpec((tm, width), lambda j, tb, nt, tab: (j, 0)),
    )
    return pl.pallas_call(
        _moe_kernel,
        grid_spec=grid_spec,
        out_shape=jax.ShapeDtypeStruct((SORTED_ROWS, width), F32),
        compiler_params=_params(1),
        name="moe",
    )(tile_bucket, n_tiles, tables, h_sorted, wr, br, w_gate, w_up, w_down, w_gate, w_up, w_down)


def _bucket_tables():
    e_lo, e_hi, grp = [], [], []
    for g in range(N_EXPERT_GROUPS):
        for lo, hi in PAIRS:
            e_lo.append(g * EXPERTS_PER_GROUP + lo)
            e_hi.append(g * EXPERTS_PER_GROUP + hi)
            grp.append(g)
    return jnp.asarray(e_lo + e_hi + grp, I32)


def kernel(x, c, w_ada, b_ada, norm1_gain, w_in, b_branch_gate, q_norm_gain, k_norm_gain, attn_sinks, gmlp_norm_gain, gmlp_norm_bias, gmlp_w_spatial, gmlp_b_spatial, w_o_attn, w_o_gmlp, w_out, norm2_gain, w_group_router, b_group_router, w_expert_router, b_expert_router, w_expert_gate, w_expert_up, w_expert_down):
    depth = w_ada.shape[0]
    x2 = x.reshape(TOKENS, D_MODEL)
    head_of_col = np.arange(ATTN_WIDTH) // HEAD_DIM
    pool = jnp.asarray(head_of_col[:, None] == np.arange(128)[None, :], BF16)
    expand = jnp.asarray(np.arange(128)[:, None] == head_of_col[None, :], BF16)
    tables = _bucket_tables()
    row = lambda v: v.reshape(1, -1)
    for l in range(depth):
        mod = _mod_call(c, w_ada[l], b_ada[l])
        shift1, scale1, gate1, shift2, scale2, gate2 = [
            m.reshape(BATCH, 1, D_MODEL) for m in jnp.split(mod, 6, axis=-1)]

        q_gain_row = row(jnp.tile(q_norm_gain[l], N_HEADS) * (HEAD_DIM ** -0.5))
        k_gain_row = row(jnp.tile(k_norm_gain[l], N_KV_HEADS))
        q, k, v, u, vg, gates = _inproj_call(
            x2, scale1, shift1, row(norm1_gain[l]), w_in[l].astype(BF16), q_gain_row, k_gain_row,
            row(gmlp_norm_gain[l]), row(gmlp_norm_bias[l]), row(b_branch_gate[l]), pool, expand)

        y_attn = _attn_call(attn_sinks[l], q, k, v)

        bias_map = jnp.repeat(gmlp_b_spatial[l].T, GMLP_WIDTH // GMLP_GROUPS, axis=1)
        wr_t = jnp.zeros((BUCKET_ROWS, D_MODEL), F32)
        wr_t = wr_t.at[0:N_EXPERT_GROUPS].set(w_group_router[l].T).at[8:8 + N_EXPERTS].set(w_expert_router[l].T)
        br_t = jnp.zeros((BUCKET_ROWS, 1), F32)
        br_t = br_t.at[0:N_EXPERT_GROUPS, 0].set(b_group_router[l]).at[8:8 + N_EXPERTS, 0].set(b_expert_router[l])
        x1, h2p, buckets = _mixer_call(
            x2, y_attn, u, vg, gates, gmlp_w_spatial[l], bias_map, w_o_attn[l].astype(BF16),
            w_o_gmlp[l].astype(BF16), w_out[l].astype(BF16), gate1, row(norm2_gain[l]), scale2, shift2,
            wr_t.astype(BF16), br_t)

        pos, meta = _rank_call(buckets)
        h_sorted = _scatter_call(pos, h2p)

        wr = jnp.zeros((D_MODEL, 128), F32)
        wr = wr.at[:, 0:N_EXPERT_GROUPS].set(w_group_router[l])
        wr = wr.at[:, N_EXPERT_GROUPS:N_EXPERT_GROUPS + N_EXPERTS].set(w_expert_router[l])
        br = jnp.zeros((1, 128), F32)
        br = br.at[0, 0:N_EXPERT_GROUPS].set(b_group_router[l])
        br = br.at[0, N_EXPERT_GROUPS:N_EXPERT_GROUPS + N_EXPERTS].set(b_expert_router[l])
        flat = lambda w: w.reshape((N_EXPERTS,) + w.shape[2:]).astype(BF16)
        y_sorted = _moe_call(
            meta[0], meta[1, 0:1], tables, h_sorted, wr.astype(BF16), br,
            flat(w_expert_gate[l]), flat(w_expert_up[l]), flat(w_expert_down[l]))

        x2 = _gather_call(pos, y_sorted, x1, gate2)
    return x2.reshape(x.shape)
```

```python
import functools

import jax
import jax.numpy as jnp
import numpy as np
from jax import lax
from jax.experimental import pallas as pl
from jax.experimental.pallas import tpu as pltpu

D_MODEL = 1024
BATCH = 8
SEQ = 4096
TOKENS = BATCH * SEQ
N_HEADS = 16
N_KV_HEADS = 4
HEAD_DIM = 64
Q_PER_KV = N_HEADS // N_KV_HEADS
BLOCK = 128
HEAD_SLOT = 128
ALIBI_PARTS = 3
ATTN_WIDTH = N_HEADS * HEAD_DIM
KV_WIDTH = N_KV_HEADS * HEAD_DIM
GMLP_WIDTH = 1024
GMLP_GROUPS = 8
GMLP_CHUNK = 128
N_EXPERT_GROUPS = 4
EXPERTS_PER_GROUP = 4
N_EXPERTS = N_EXPERT_GROUPS * EXPERTS_PER_GROUP
D_EXPERT = 512
IN_WIDTH = ATTN_WIDTH + 2 * KV_WIDTH + 2 * GMLP_WIDTH + 2 * D_MODEL
EPS = 1e-6
NEG_INF = -1e30
LOG2E = float(np.float32(np.log2(np.e)))

Q0, Q1 = 0, ATTN_WIDTH
K0, K1 = Q1, Q1 + KV_WIDTH
V0, V1 = K1, K1 + KV_WIDTH
U0, U1 = V1, V1 + GMLP_WIDTH
G0, G1 = U1, U1 + GMLP_WIDTH
B0, B1 = G1, G1 + 2 * D_MODEL

PAIRS = ((0, 1), (0, 2), (0, 3), (1, 2), (1, 3), (2, 3))
N_BUCKETS = N_EXPERT_GROUPS * len(PAIRS)
BUCKET_ROWS = 32

TM_PROJ = 512
TM_ATTN = 512
TM_MIX = 512
TM_MOE = 256
TM_ROW = 512
SCORES_AHEAD = 2
ROW_UNROLL = 16
NT_MOE = TOKENS // TM_MOE + N_BUCKETS
SORTED_ROWS = NT_MOE * TM_MOE
META_LANES = 256
assert NT_MOE <= META_LANES

VMEM_LIMIT = 56 * 1024 * 1024

F32 = jnp.float32
BF16 = jnp.bfloat16
U32 = jnp.uint32
I32 = jnp.int32


def _alibi_slopes():
    return [float(np.float32(2.0 ** (-8.0 * (i + 1) / N_HEADS))) for i in range(N_HEADS)]


def _params(n_axes):
    return pltpu.CompilerParams(dimension_semantics=("arbitrary",) * n_axes, vmem_limit_bytes=VMEM_LIMIT)


def _sigmoid(x):
    return 1.0 / (1.0 + jnp.exp(-x))


def _gelu(x):
    return 0.5 * x * (1.0 + lax.erf(x * np.float32(1.0 / np.sqrt(2.0))))


def _dot(a, b):
    return jnp.dot(a, b, preferred_element_type=F32)


def _dot_nt(a, b):
    return lax.dot_general(a, b, (((1,), (1,)), ((), ())), preferred_element_type=F32)


def _mod_kernel(c_ref, w_ref, b_ref, o_ref):
    c = c_ref[...]
    ca = c * _sigmoid(c)
    o_ref[...] = _dot(ca.astype(BF16), w_ref[...].astype(BF16)) + b_ref[...]


def _mod_call(c, w_ada, b_ada):
    n = w_ada.shape[1]
    bn = 1536
    return pl.pallas_call(
        _mod_kernel,
        grid=(n // bn,),
        in_specs=[
            pl.BlockSpec((BATCH, D_MODEL), lambda j: (0, 0)),
            pl.BlockSpec((D_MODEL, bn), lambda j: (0, j)),
            pl.BlockSpec((1, bn), lambda j: (0, j)),
        ],
        out_specs=pl.BlockSpec((BATCH, bn), lambda j: (0, j)),
        out_shape=jax.ShapeDtypeStruct((BATCH, n), F32),
        compiler_params=_params(1),
        name="mod",
    )(c, w_ada, b_ada.reshape(1, n))


def _inproj_kernel(x_ref, sc_ref, sh_ref, ng_ref, w_ref, wvt_ref, qg_ref, kg_ref, lng_ref, lnb_ref, bg_ref, hp_ref,
                   he_ref, qaug_ref, q_o, k_o, vt_o, u_o, vg_o, gt_o):
    x = x_ref[...]
    ms = jnp.mean(x * x, axis=-1, keepdims=True)
    h = (x * lax.rsqrt(ms + EPS)) * ng_ref[...]
    h = h * (1.0 + sc_ref[0]) + sh_ref[0]
    hb = h.astype(BF16)

    def proj(c0, c1):
        return _dot(hb, w_ref[:, c0:c1])

    def head_rmsnorm(raw, pool, expand, gain):
        ssq = _dot((raw * raw).astype(BF16), pool)
        r = lax.rsqrt(ssq * (1.0 / HEAD_DIM) + EPS)
        r_hi = r.astype(BF16)
        r_lo = (r - r_hi.astype(F32)).astype(BF16)
        return raw * (_dot(r_hi, expand) + _dot(r_lo, expand)) * gain

    low_half = lax.broadcasted_iota(I32, (1, HEAD_SLOT), 1) < HEAD_DIM

    def store_head_slots(xn, o_ref, spare):
        for p in range(xn.shape[1] // HEAD_SLOT):
            pair = xn[:, p * HEAD_SLOT:(p + 1) * HEAD_SLOT]
            for hd, head in ((2 * p, pair), (2 * p + 1, pltpu.roll(pair, HEAD_DIM, 1))):
                cols = slice(hd * HEAD_SLOT, (hd + 1) * HEAD_SLOT)
                fill = 0.0 if spare is None else spare[:, cols]
                o_ref[:, cols] = jnp.where(low_half, head, fill).astype(BF16)

    store_head_slots(head_rmsnorm(proj(Q0, Q1), hp_ref[...], he_ref[...], qg_ref[...]), q_o, qaug_ref)
    store_head_slots(head_rmsnorm(proj(K0, K1), hp_ref[0:KV_WIDTH, :], he_ref[:, 0:KV_WIDTH], kg_ref[...]), k_o, None)
    vt_o[...] = _dot_nt(wvt_ref[...], hb).astype(BF16)
    u_o[...] = _gelu(proj(U0, U1)).astype(BF16)
    vg = _gelu(proj(G0, G1))
    mu = jnp.mean(vg, axis=-1, keepdims=True)
    vc = vg - mu
    var = jnp.mean(vc * vc, axis=-1, keepdims=True)
    vg_o[...] = (vc * lax.rsqrt(var + EPS) * lng_ref[...] + lnb_ref[...]).astype(BF16)
    gt_o[...] = _sigmoid(proj(B0, B1) + bg_ref[...]).astype(BF16)


def _inproj_call(x2, scale1, shift1, norm_gain, w_in, wv_t, q_gain_row, k_gain_row, ln_gain, ln_bias, b_gate, pool,
                 expand, q_aug):
    tm = TM_PROJ
    per_batch = SEQ // tm
    row = lambda n: pl.BlockSpec((1, n), lambda i: (0, 0))
    modrow = pl.BlockSpec((1, 1, D_MODEL), lambda i: (i // per_batch, 0, 0))
    tile = lambda n: pl.BlockSpec((tm, n), lambda i: (i, 0))
    out = lambda n: jax.ShapeDtypeStruct((TOKENS, n), BF16)
    return pl.pallas_call(
        _inproj_kernel,
        grid=(TOKENS // tm,),
        in_specs=[
            tile(D_MODEL), modrow, modrow, row(D_MODEL),
            pl.BlockSpec((D_MODEL, IN_WIDTH), lambda i: (0, 0)),
            pl.BlockSpec((KV_WIDTH, D_MODEL), lambda i: (0, 0)),
            row(ATTN_WIDTH), row(KV_WIDTH), row(GMLP_WIDTH), row(GMLP_WIDTH), row(2 * D_MODEL),
            pl.BlockSpec((ATTN_WIDTH, 128), lambda i: (0, 0)),
            pl.BlockSpec((128, ATTN_WIDTH), lambda i: (0, 0)),
            row(N_HEADS * HEAD_SLOT),
        ],
        out_specs=[tile(N_HEADS * HEAD_SLOT), tile(N_KV_HEADS * HEAD_SLOT),
                   pl.BlockSpec((KV_WIDTH, tm), lambda i: (0, i)),
                   tile(GMLP_WIDTH), tile(GMLP_WIDTH), tile(2 * D_MODEL)],
        out_shape=[out(N_HEADS * HEAD_SLOT), out(N_KV_HEADS * HEAD_SLOT),
                   jax.ShapeDtypeStruct((KV_WIDTH, TOKENS), BF16),
                   out(GMLP_WIDTH), out(GMLP_WIDTH), out(2 * D_MODEL)],
        compiler_params=_params(1),
        name="inproj",
    )(x2, scale1, shift1, norm_gain, w_in, wv_t, q_gain_row, k_gain_row, ln_gain, ln_bias, b_gate, pool, expand,
      q_aug)


def _attn_kernel(sink_ref, q_ref, kc_ref, kp_ref, vc_ref, vp_ref, o_ref):
    i = pl.program_id(1)
    slopes = _alibi_slopes()
    kj = lax.broadcasted_iota(I32, (BLOCK, 2 * BLOCK), 0)
    qi = lax.broadcasted_iota(I32, (BLOCK, 2 * BLOCK), 1) & (BLOCK - 1)
    own = kj <= qi
    from_own = jnp.where(own, 1.0, 0.0).astype(BF16)
    from_prev = jnp.where(own, 0.0, 1.0).astype(BF16)
    has_prev = i > 0
    low_half = lax.broadcasted_iota(I32, (1, HEAD_SLOT), 1) < HEAD_DIM
    k_lane = lax.broadcasted_iota(I32, (2 * BLOCK, HEAD_SLOT), 1)
    k_aug = jnp.where((k_lane >= HEAD_DIM) & (k_lane < HEAD_DIM + ALIBI_PARTS),
                      lax.broadcasted_iota(I32, (2 * BLOCK, HEAD_SLOT), 0), 0).astype(F32).astype(BF16)
    q_pos = (lax.broadcasted_iota(I32, (1, BLOCK), 1) + BLOCK).astype(F32)

    keys_of = {}

    def scores(item):
        sb, kv, pr = item
        r0, r1 = sb * BLOCK, (sb + 1) * BLOCK
        if (sb, kv) not in keys_of:
            cols = slice(kv * HEAD_SLOT, (kv + 1) * HEAD_SLOT)
            k_prev = kp_ref[:, cols] if sb == 0 else kc_ref[r0 - BLOCK:r0, cols]
            keys_of[sb, kv] = jnp.where(low_half, jnp.concatenate([k_prev, kc_ref[r0:r1, cols]], axis=0), k_aug)
        ha = kv * Q_PER_KV + 2 * pr
        queries = jnp.concatenate([q_ref[r0:r1, ha * HEAD_SLOT:(ha + 1) * HEAD_SLOT],
                                   q_ref[r0:r1, (ha + 1) * HEAD_SLOT:(ha + 2) * HEAD_SLOT]], axis=0)
        s = _dot_nt(keys_of[sb, kv], queries)
        s_prev = s[0:BLOCK, :]
        if sb == 0:
            s_prev = jnp.where(has_prev, s_prev, NEG_INF)
        return jnp.where(own, s[BLOCK:2 * BLOCK, :], s_prev)

    def attend(item, s):
        sb, kv, pr = item
        r0, r1 = sb * BLOCK, (sb + 1) * BLOCK
        rows = slice(kv * HEAD_DIM, (kv + 1) * HEAD_DIM)
        v_prev = vp_ref[rows, :] if sb == 0 else vc_ref[rows, r0 - BLOCK:r0]
        vt = jnp.concatenate([v_prev, vc_ref[rows, r0:r1]], axis=1)
        ha = kv * Q_PER_KV + 2 * pr
        hb = ha + 1
        sink = LOG2E * jnp.concatenate(
            [sink_ref[ha] + slopes[ha] * q_pos, sink_ref[hb] + slopes[hb] * q_pos], axis=1)
        m = jnp.maximum(jnp.max(s, axis=0, keepdims=True), sink)
        p = jnp.exp2(s - m)
        den = jnp.sum(p, axis=0, keepdims=True) + jnp.exp2(sink - m)
        pb = p.astype(BF16)
        p_keys = jnp.concatenate([pb * from_prev, pb * from_own], axis=0)
        o = _dot(vt, p_keys) / den
        pair = jnp.concatenate([o[:, 0:BLOCK], o[:, BLOCK:2 * BLOCK]], axis=0).T
        o_ref[r0:r1, ha * HEAD_DIM:(hb + 1) * HEAD_DIM] = pair.astype(BF16)

    items = [(sb, kv, pr) for sb in range(TM_ATTN // BLOCK) for kv in range(N_KV_HEADS)
             for pr in range(Q_PER_KV // 2)]
    pending = []
    for n in range(len(items) + SCORES_AHEAD):
        if n < len(items):
            pending.append(scores(items[n]))
        if n >= SCORES_AHEAD:
            attend(items[n - SCORES_AHEAD], pending.pop(0))


def _attn_call(sinks, q, k, vt):
    tq = TM_ATTN
    per_batch = SEQ // tq
    blocks_per_tile = tq // BLOCK
    q_width = N_HEADS * HEAD_SLOT
    k_width = N_KV_HEADS * HEAD_SLOT
    cur = lambda n: pl.BlockSpec((tq, n), lambda b, i: (b * per_batch + i, 0))
    prev_block = lambda b, i: b * (SEQ // BLOCK) + jnp.maximum(i * blocks_per_tile - 1, 0)
    return pl.pallas_call(
        _attn_kernel,
        grid=(BATCH, per_batch),
        in_specs=[
            pl.BlockSpec(memory_space=pltpu.SMEM),
            cur(q_width), cur(k_width),
            pl.BlockSpec((BLOCK, k_width), lambda b, i: (prev_block(b, i), 0)),
            pl.BlockSpec((KV_WIDTH, tq), lambda b, i: (0, b * per_batch + i)),
            pl.BlockSpec((KV_WIDTH, BLOCK), lambda b, i: (0, prev_block(b, i))),
        ],
        out_specs=cur(ATTN_WIDTH),
        out_shape=jax.ShapeDtypeStruct((TOKENS, ATTN_WIDTH), BF16),
        compiler_params=_params(2),
        name="attn",
    )(sinks, q, k, k, vt, vt)


def _first_argmax(vals):
    m = vals[0]
    for v in vals[1:]:
        m = jnp.maximum(m, v)
    idx = jnp.full(m.shape, len(vals) - 1, I32)
    for k in range(len(vals) - 2, -1, -1):
        idx = jnp.where(vals[k] == m, k, idx)
    return m, idx


def _mixer_kernel(x_ref, ya_ref, u_ref, vg_ref, gt_ref, ws_ref, bmap_ref, woa_ref, wog_ref, wout_ref, gate1_ref,
                  ng_ref, sc_ref, sh_ref, wrt_ref, brt_ref, x1_o, h2_o, bk_o):
    tm = TM_MIX
    t_idx = lax.broadcasted_iota(I32, (GMLP_CHUNK, GMLP_CHUNK), 0)
    s_idx = lax.broadcasted_iota(I32, (GMLP_CHUNK, GMLP_CHUNK), 1)
    ws = [jnp.where(t_idx >= s_idx, ws_ref[g], 0.0).astype(BF16) for g in range(GMLP_GROUPS)]
    gc = GMLP_WIDTH // GMLP_GROUPS
    chunks = []
    for c in range(tm // GMLP_CHUNK):
        r0, r1 = c * GMLP_CHUNK, (c + 1) * GMLP_CHUNK
        mixed = jnp.concatenate(
            [_dot(ws[g], vg_ref[r0:r1, g * gc:(g + 1) * gc]) for g in range(GMLP_GROUPS)], axis=1)
        mixed = mixed + bmap_ref[...]
        chunks.append((u_ref[r0:r1, :].astype(F32) * mixed).astype(BF16))
    y_gmlp = jnp.concatenate(chunks, axis=0)
    pa = _dot(ya_ref[...], woa_ref[...])
    pg = _dot(y_gmlp, wog_ref[...])
    merged = gt_ref[:, 0:D_MODEL].astype(F32) * pa + gt_ref[:, D_MODEL:2 * D_MODEL].astype(F32) * pg
    x1 = x_ref[...] + gate1_ref[0] * _dot(merged.astype(BF16), wout_ref[...])
    x1_o[...] = x1

    ms = jnp.mean(x1 * x1, axis=-1, keepdims=True)
    h2 = (x1 * lax.rsqrt(ms + EPS)) * ng_ref[...]
    h2 = h2 * (1.0 + sc_ref[0]) + sh_ref[0]
    h2_o[...] = h2

    lg = _dot_nt(wrt_ref[...], h2.astype(BF16)) + brt_ref[...]
    _, gi = _first_argmax([lg[r:r + 1, :] for r in range(N_EXPERT_GROUPS)])
    el = [lg[8 + r:9 + r, :] for r in range(N_EXPERTS)]
    eg = []
    for k in range(EXPERTS_PER_GROUP):
        v = el[(N_EXPERT_GROUPS - 1) * EXPERTS_PER_GROUP + k]
        for g in range(N_EXPERT_GROUPS - 2, -1, -1):
            v = jnp.where(gi == g, el[g * EXPERTS_PER_GROUP + k], v)
        eg.append(v)
    _, i1 = _first_argmax(eg)
    _, i2 = _first_argmax([jnp.where(i1 == k, -3e38, eg[k]) for k in range(EXPERTS_PER_GROUP)])
    lo = jnp.minimum(i1, i2)
    hi = jnp.maximum(i1, i2)
    pair = jnp.where(lo == 0, hi - 1, jnp.where(lo == 1, hi + 1, 5))
    bk_o[0] = gi * len(PAIRS) + pair


def _mixer_call(x2, y_attn, u, vg, gates, w_spatial, bias_map, wo_attn, wo_gmlp, w_out, gate1, norm_gain, scale2, shift2,
                wr_t, br_t):
    tm = TM_MIX
    per_batch = SEQ // tm
    nt = TOKENS // tm
    tile = lambda n: pl.BlockSpec((tm, n), lambda i: (i, 0))
    full = lambda shape: pl.BlockSpec(shape, lambda i: (0,) * len(shape))
    modrow = pl.BlockSpec((1, 1, D_MODEL), lambda i: (i // per_batch, 0, 0))
    return pl.pallas_call(
        _mixer_kernel,
        grid=(nt,),
        in_specs=[
            tile(D_MODEL), tile(ATTN_WIDTH), tile(GMLP_WIDTH), tile(GMLP_WIDTH), tile(2 * D_MODEL),
            full((GMLP_GROUPS, GMLP_CHUNK, GMLP_CHUNK)), full((GMLP_CHUNK, GMLP_WIDTH)),
            full((ATTN_WIDTH, D_MODEL)), full((GMLP_WIDTH, D_MODEL)), full((D_MODEL, D_MODEL)),
            modrow, full((1, D_MODEL)), modrow, modrow,
            full((BUCKET_ROWS, D_MODEL)), full((BUCKET_ROWS, 1)),
        ],
        out_specs=[tile(D_MODEL), tile(D_MODEL), pl.BlockSpec((1, 1, tm), lambda i: (i, 0, 0))],
        out_shape=[
            jax.ShapeDtypeStruct((TOKENS, D_MODEL), F32),
            jax.ShapeDtypeStruct((TOKENS, D_MODEL), F32),
            jax.ShapeDtypeStruct((nt, 1, tm), I32),
        ],
        compiler_params=_params(1),
        name="mixer",
    )(x2, y_attn, u, vg, gates, w_spatial, bias_map, wo_attn, wo_gmlp, w_out, gate1, norm_gain, scale2, shift2,
      wr_t, br_t)


def _rank_kernel(bk_ref, pos_o, meta_o, cnt, off):
    tm = TM_MIX
    ph = pl.program_id(0)
    i = pl.program_id(1)
    bk = bk_ref[0]
    sub = lax.broadcasted_iota(I32, (BUCKET_ROWS, tm), 0)
    onehot = sub == bk

    @pl.when((ph == 0) & (i == 0))
    def _():
        cnt[...] = jnp.zeros_like(cnt)

    @pl.when(ph == 0)
    def _():
        cnt[...] += jnp.sum(jnp.where(onehot, 1.0, 0.0), axis=1, keepdims=True)

    @pl.when((ph == 1) & (i == 0))
    def _():
        tiles = jnp.floor((cnt[...] + (TM_MOE - 1)) * (1.0 / TM_MOE))
        r = lax.broadcasted_iota(I32, (BUCKET_ROWS, BUCKET_ROWS), 0)
        c = lax.broadcasted_iota(I32, (BUCKET_ROWS, BUCKET_ROWS), 1)
        before = jnp.where(c < r, 1.0, 0.0).astype(BF16)
        first_tile = _dot(before, tiles.astype(BF16))
        off[...] = first_tile * TM_MOE
        cnt[...] = jnp.zeros_like(cnt)
        end_tile = (first_tile + tiles)[:, 0:1]
        tj = lax.broadcasted_iota(I32, (BUCKET_ROWS, META_LANES), 1).astype(F32)
        bsub = lax.broadcasted_iota(I32, (BUCKET_ROWS, META_LANES), 0)
        passed = jnp.where((tj >= end_tile) & (bsub < N_BUCKETS), 1.0, 0.0)
        tile_bucket = jnp.sum(passed, axis=0, keepdims=True)
        n_tiles = jnp.sum(jnp.where(bsub == N_BUCKETS - 1, jnp.broadcast_to(end_tile, passed.shape), 0.0),
                          axis=0, keepdims=True)
        row = lax.broadcasted_iota(I32, (8, META_LANES), 0)
        meta = jnp.where(row == 0, jnp.broadcast_to(tile_bucket, (8, META_LANES)),
                         jnp.where(row == 1, jnp.broadcast_to(n_tiles, (8, META_LANES)), 0.0))
        meta_o[...] = meta.astype(I32)

    @pl.when(ph == 1)
    def _():
        s_idx = lax.broadcasted_iota(I32, (tm, tm), 0)
        t_idx = lax.broadcasted_iota(I32, (tm, tm), 1)
        upto = jnp.where(s_idx <= t_idx, 1.0, 0.0).astype(BF16)
        incl = _dot(jnp.where(onehot, 1.0, 0.0).astype(BF16), upto)
        base = off[:, 0:1] + cnt[:, 0:1]
        posf = jnp.sum(jnp.where(onehot, base + incl - 1.0, 0.0), axis=0, keepdims=True)
        pos_o[0] = posf.astype(I32)
        cnt[...] += incl[:, tm - 1:tm]


def _rank_call(buckets):
    nt, _, tm = buckets.shape
    return pl.pallas_call(
        _rank_kernel,
        grid=(2, nt),
        in_specs=[pl.BlockSpec((1, 1, tm), lambda ph, i: (i, 0, 0))],
        out_specs=[
            pl.BlockSpec((1, 1, tm), lambda ph, i: (ph * i, 0, 0)),
            pl.BlockSpec((8, META_LANES), lambda ph, i: (0, 0)),
        ],
        out_shape=[
            jax.ShapeDtypeStruct((nt, 1, tm), I32),
            jax.ShapeDtypeStruct((8, META_LANES), I32),
        ],
        scratch_shapes=[pltpu.VMEM((BUCKET_ROWS, 128), F32), pltpu.VMEM((BUCKET_ROWS, 128), F32)],
        compiler_params=_params(2),
        name="rank",
    )(buckets)


def _row_copy(src, src_row, dst, dst_row, sem):
    return pltpu.make_async_copy(src.at[pl.ds(src_row, 1)], dst.at[pl.ds(dst_row, 1)], sem)


def _start_tile_rows(copy_of_row):
    def trip(c, carry):
        for k in range(ROW_UNROLL):
            copy_of_row(c * ROW_UNROLL + k).start(priority=k % 2)
        return carry
    lax.fori_loop(0, TM_ROW // ROW_UNROLL, trip, 0)


def _tile_wait(src, dst, sem):
    pltpu.make_async_copy(src.at[pl.ds(0, TM_ROW)], dst, sem).wait()


def _scatter_kernel(pos_ref, h_ref, init_ref, o_ref, sem):
    del init_ref
    _start_tile_rows(lambda r: _row_copy(h_ref, r, o_ref, pos_ref[0, 0, r], sem))
    _tile_wait(h_ref, o_ref.at[pl.ds(0, TM_ROW)], sem)


def _scatter_call(pos, h2):
    tm = TM_ROW
    width = h2.shape[1]
    init = jnp.zeros((SORTED_ROWS, width), F32)
    return pl.pallas_call(
        _scatter_kernel,
        grid=(TOKENS // tm,),
        in_specs=[
            pl.BlockSpec((1, 1, tm), lambda i: (i, 0, 0), memory_space=pltpu.SMEM),
            pl.BlockSpec((tm, width), lambda i: (i, 0)),
            pl.BlockSpec(memory_space=pl.ANY),
        ],
        out_specs=pl.BlockSpec(memory_space=pl.ANY),
        out_shape=jax.ShapeDtypeStruct((SORTED_ROWS, width), F32),
        scratch_shapes=[pltpu.SemaphoreType.DMA(())],
        input_output_aliases={2: 0},
        compiler_params=_params(1),
        name="scatter",
    )(pos, h2, init)


def _gather_kernel(pos_ref, pos_next_ref, ys_ref, x1_ref, gate2_ref, o_ref, buf, sem):
    i = pl.program_id(0)
    n = pl.num_programs(0)
    slot = lax.rem(i, 2)

    def fetch(p_ref, s):
        _start_tile_rows(lambda r: _row_copy(ys_ref, p_ref[0, 0, r], buf.at[s], r, sem.at[s]))

    @pl.when(i == 0)
    def _():
        fetch(pos_ref, 0)

    @pl.when(i + 1 < n)
    def _():
        fetch(pos_next_ref, 1 - slot)

    _tile_wait(ys_ref, buf.at[slot], sem.at[slot])
    o_ref[...] = x1_ref[...] + gate2_ref[0] * buf[slot]


def _gather_call(pos, y_sorted, x1, gate2):
    tm = TM_ROW
    per_batch = SEQ // tm
    n = TOKENS // tm
    width = y_sorted.shape[1]
    return pl.pallas_call(
        _gather_kernel,
        grid=(n,),
        in_specs=[
            pl.BlockSpec((1, 1, tm), lambda i: (i, 0, 0), memory_space=pltpu.SMEM),
            pl.BlockSpec((1, 1, tm), lambda i: (jnp.minimum(i + 1, n - 1), 0, 0), memory_space=pltpu.SMEM),
            pl.BlockSpec(memory_space=pl.ANY),
            pl.BlockSpec((tm, D_MODEL), lambda i: (i, 0)),
            pl.BlockSpec((1, 1, D_MODEL), lambda i: (i // per_batch, 0, 0)),
        ],
        out_specs=pl.BlockSpec((tm, D_MODEL), lambda i: (i, 0)),
        out_shape=jax.ShapeDtypeStruct((TOKENS, D_MODEL), F32),
        scratch_shapes=[pltpu.VMEM((2, tm, width), F32), pltpu.SemaphoreType.DMA((2,))],
        compiler_params=_params(1),
        name="gather",
    )(pos, pos, y_sorted, x1, gate2)


def _moe_kernel(tb_ref, nt_ref, tab_ref, h_ref, wr_ref, br_ref, wg_lo, wu_lo, wd_lo, wg_hi, wu_hi, wd_hi, o_ref):
    j = pl.program_id(0)

    @pl.when(j < nt_ref[0])
    def _():
        b = tb_ref[j]
        e_lo = tab_ref[b]
        e_hi = tab_ref[N_BUCKETS + b]
        g = tab_ref[2 * N_BUCKETS + b]
        h = h_ref[...].astype(BF16)
        lg = _dot(h, wr_ref[...]) + br_ref[...]
        lane = lax.broadcasted_iota(I32, lg.shape, 1)
        is_group = lane < N_EXPERT_GROUPS
        gmax = jnp.max(jnp.where(is_group, lg, NEG_INF), axis=1, keepdims=True)
        ge = jnp.exp(lg - gmax)
        g_w = (jnp.sum(jnp.where(lane == g, ge, 0.0), axis=1, keepdims=True)
               / jnp.sum(jnp.where(is_group, ge, 0.0), axis=1, keepdims=True))
        v_lo = jnp.sum(jnp.where(lane == N_EXPERT_GROUPS + e_lo, lg, 0.0), axis=1, keepdims=True)
        v_hi = jnp.sum(jnp.where(lane == N_EXPERT_GROUPS + e_hi, lg, 0.0), axis=1, keepdims=True)
        m = jnp.maximum(v_lo, v_hi)
        x_lo = jnp.exp(v_lo - m)
        x_hi = jnp.exp(v_hi - m)
        w_lo = x_lo / (x_lo + x_hi) * g_w
        w_hi = x_hi / (x_lo + x_hi) * g_w

        def hidden(wg, wu, w):
            a = _dot(h, wg[0])
            return (a * _sigmoid(a) * _dot(h, wu[0]) * w).astype(BF16)

        y = _dot(hidden(wg_lo, wu_lo, w_lo), wd_lo[0]) + _dot(hidden(wg_hi, wu_hi, w_hi), wd_hi[0])
        o_ref[...] = y

    @pl.when(j >= nt_ref[0])
    def _():
        o_ref[...] = jnp.zeros_like(o_ref)


def _moe_call(tile_bucket, n_tiles, tables, h_sorted, wr, br, w_gate, w_up, w_down):
    tm = TM_MOE
    width = h_sorted.shape[1]

    def last(j, nt):
        return jnp.minimum(j, nt[0] - 1)

    def w_spec(shape, which):
        return pl.BlockSpec(
            (1,) + shape, lambda j, tb, nt, tab: (tab[which * N_BUCKETS + tb[last(j, nt)]], 0, 0))

    up = (D_MODEL, D_EXPERT)
    down = (D_EXPERT, D_MODEL)
    grid_spec = pltpu.PrefetchScalarGridSpec(
        num_scalar_prefetch=3,
        grid=(NT_MOE,),
        in_specs=[
            pl.BlockSpec((tm, width), lambda j, tb, nt, tab: (last(j, nt), 0)),
            pl.BlockSpec((D_MODEL, 128), lambda j, tb, nt, tab: (0, 0)),
            pl.BlockSpec((1, 128), lambda j, tb, nt, tab: (0, 0)),
            w_spec(up, 0), w_spec(up, 0), w_spec(down, 0),
            w_spec(up, 1), w_spec(up, 1), w_spec(down, 1),
        ],
        out_specs=pl.BlockSpec((tm, width), lambda j, tb, nt, tab: (j, 0)),
    )
    return pl.pallas_call(
        _moe_kernel,
        grid_spec=grid_spec,
        out_shape=jax.ShapeDtypeStruct((SORTED_ROWS, width), F32),
        compiler_params=_params(1),
        name="moe",
    )(tile_bucket, n_tiles, tables, h_sorted, wr, br, w_gate, w_up, w_down, w_gate, w_up, w_down)


def _alibi_query_lanes():
    lanes = np.zeros((1, N_HEADS * HEAD_SLOT), np.float32)
    for hd, slope in enumerate(_alibi_slopes()):
        rest = np.float32(slope * LOG2E)
        for part in range(ALIBI_PARTS):
            piece = np.float32(np.asarray(rest).astype(jnp.bfloat16))
            lanes[0, hd * HEAD_SLOT + HEAD_DIM + part] = piece
            rest = np.float32(rest - piece)
    return jnp.asarray(lanes)


def _bucket_tables():
    e_lo, e_hi, grp = [], [], []
    for g in range(N_EXPERT_GROUPS):
        for lo, hi in PAIRS:
            e_lo.append(g * EXPERTS_PER_GROUP + lo)
            e_hi.append(g * EXPERTS_PER_GROUP + hi)
            grp.append(g)
    return jnp.asarray(e_lo + e_hi + grp, I32)


def kernel(x, c, w_ada, b_ada, norm1_gain, w_in, b_branch_gate, q_norm_gain, k_norm_gain, attn_sinks, gmlp_norm_gain, gmlp_norm_bias, gmlp_w_spatial, gmlp_b_spatial, w_o_attn, w_o_gmlp, w_out, norm2_gain, w_group_router, b_group_router, w_expert_router, b_expert_router, w_expert_gate, w_expert_up, w_expert_down):
    depth = w_ada.shape[0]
    x2 = x.reshape(TOKENS, D_MODEL)
    head_of_col = np.arange(ATTN_WIDTH) // HEAD_DIM
    pool = jnp.asarray(head_of_col[:, None] == np.arange(128)[None, :], BF16)
    expand = jnp.asarray(np.arange(128)[:, None] == head_of_col[None, :], BF16)
    tables = _bucket_tables()
    q_aug = _alibi_query_lanes()
    row = lambda v: v.reshape(1, -1)
    for l in range(depth):
        mod = _mod_call(c, w_ada[l], b_ada[l])
        shift1, scale1, gate1, shift2, scale2, gate2 = [
            m.reshape(BATCH, 1, D_MODEL) for m in jnp.split(mod, 6, axis=-1)]

        q_gain_row = row(jnp.tile(q_norm_gain[l], N_HEADS) * (HEAD_DIM ** -0.5 * LOG2E))
        k_gain_row = row(jnp.tile(k_norm_gain[l], N_KV_HEADS))
        q, k, vt, u, vg, gates = _inproj_call(
            x2, scale1, shift1, row(norm1_gain[l]), w_in[l].astype(BF16), w_in[l][:, V0:V1].T.astype(BF16),
            q_gain_row, k_gain_row, row(gmlp_norm_gain[l]), row(gmlp_norm_bias[l]), row(b_branch_gate[l]),
            pool, expand, q_aug)

        y_attn = _attn_call(attn_sinks[l], q, k, vt)

        bias_map = jnp.repeat(gmlp_b_spatial[l].T, GMLP_WIDTH // GMLP_GROUPS, axis=1)
        wr_t = jnp.zeros((BUCKET_ROWS, D_MODEL), F32)
        wr_t = wr_t.at[0:N_EXPERT_GROUPS].set(w_group_router[l].T).at[8:8 + N_EXPERTS].set(w_expert_router[l].T)
        br_t = jnp.zeros((BUCKET_ROWS, 1), F32)
        br_t = br_t.at[0:N_EXPERT_GROUPS, 0].set(b_group_router[l]).at[8:8 + N_EXPERTS, 0].set(b_expert_router[l])
        x1, h2p, buckets = _mixer_call(
            x2, y_attn, u, vg, gates, gmlp_w_spatial[l], bias_map, w_o_attn[l].astype(BF16),
            w_o_gmlp[l].astype(BF16), w_out[l].astype(BF16), gate1, row(norm2_gain[l]), scale2, shift2,
            wr_t.astype(BF16), br_t)

        pos, meta = _rank_call(buckets)
        h_sorted = _scatter_call(pos, h2p)

        wr = jnp.zeros((D_MODEL, 128), F32)
        wr = wr.at[:, 0:N_EXPERT_GROUPS].set(w_group_router[l])
        wr = wr.at[:, N_EXPERT_GROUPS:N_EXPERT_GROUPS + N_EXPERTS].set(w_expert_router[l])
        br = jnp.zeros((1, 128), F32)
        br = br.at[0, 0:N_EXPERT_GROUPS].set(b_group_router[l])
        br = br.at[0, N_EXPERT_GROUPS:N_EXPERT_GROUPS + N_EXPERTS].set(b_expert_router[l])
        flat = lambda w: w.reshape((N_EXPERTS,) + w.shape[2:]).astype(BF16)
        y_sorted = _moe_call(
            meta[0], meta[1, 0:1], tables, h_sorted, wr.astype(BF16), br,
            flat(w_expert_gate[l]), flat(w_expert_up[l]), flat(w_expert_down[l]))

        x2 = _gather_call(pos, y_sorted, x1, gate2)
    return x2.reshape(x.shape)
```

```python
import functools

import jax
import jax.numpy as jnp
import numpy as np
from jax import lax
from jax.experimental import pallas as pl
from jax.experimental.pallas import tpu as pltpu

D_MODEL = 1024
BATCH = 8
SEQ = 4096
TOKENS = BATCH * SEQ
N_HEADS = 16
N_KV_HEADS = 4
HEAD_DIM = 64
Q_PER_KV = N_HEADS // N_KV_HEADS
BLOCK = 128
HEAD_SLOT = 128
ALIBI_PARTS = 3
ATTN_WIDTH = N_HEADS * HEAD_DIM
KV_WIDTH = N_KV_HEADS * HEAD_DIM
GMLP_WIDTH = 1024
GMLP_GROUPS = 8
GMLP_CHUNK = 128
N_EXPERT_GROUPS = 4
EXPERTS_PER_GROUP = 4
N_EXPERTS = N_EXPERT_GROUPS * EXPERTS_PER_GROUP
D_EXPERT = 512
IN_WIDTH = ATTN_WIDTH + 2 * KV_WIDTH + 2 * GMLP_WIDTH + 2 * D_MODEL
EPS = 1e-6
NEG_INF = -1e30
LOG2E = float(np.float32(np.log2(np.e)))

Q0, Q1 = 0, ATTN_WIDTH
K0, K1 = Q1, Q1 + KV_WIDTH
V0, V1 = K1, K1 + KV_WIDTH
U0, U1 = V1, V1 + GMLP_WIDTH
G0, G1 = U1, U1 + GMLP_WIDTH
B0, B1 = G1, G1 + 2 * D_MODEL

PAIRS = ((0, 1), (0, 2), (0, 3), (1, 2), (1, 3), (2, 3))
N_BUCKETS = N_EXPERT_GROUPS * len(PAIRS)
BUCKET_ROWS = 32

TM_PROJ = 512
TM_ATTN = 512
TM_MIX = 512
TM_MOE = 256
TM_ROW = 512
SCORES_AHEAD = 2
ROW_UNROLL = 16
NT_MOE = TOKENS // TM_MOE + N_BUCKETS
SORTED_ROWS = NT_MOE * TM_MOE
SLAB = D_MODEL // 128
META_LANES = 256
assert NT_MOE <= META_LANES

VMEM_LIMIT = 56 * 1024 * 1024

F32 = jnp.float32
BF16 = jnp.bfloat16
U32 = jnp.uint32
I32 = jnp.int32


def _alibi_slopes():
    return [float(np.float32(2.0 ** (-8.0 * (i + 1) / N_HEADS))) for i in range(N_HEADS)]


def _params(n_axes):
    return pltpu.CompilerParams(dimension_semantics=("arbitrary",) * n_axes, vmem_limit_bytes=VMEM_LIMIT)


def _sigmoid(x):
    return 1.0 / (1.0 + jnp.exp(-x))


def _gelu(x):
    return 0.5 * x * (1.0 + lax.erf(x * np.float32(1.0 / np.sqrt(2.0))))


def _dot(a, b):
    return jnp.dot(a, b, preferred_element_type=F32)


def _dot_nt(a, b):
    return lax.dot_general(a, b, (((1,), (1,)), ((), ())), preferred_element_type=F32)


def _mod_kernel(c_ref, w_ref, b_ref, o_ref):
    c = c_ref[...]
    ca = c * _sigmoid(c)
    o_ref[...] = _dot(ca.astype(BF16), w_ref[...].astype(BF16)) + b_ref[...]


def _mod_call(c, w_ada, b_ada):
    n = w_ada.shape[1]
    bn = 1536
    return pl.pallas_call(
        _mod_kernel,
        grid=(n // bn,),
        in_specs=[
            pl.BlockSpec((BATCH, D_MODEL), lambda j: (0, 0)),
            pl.BlockSpec((D_MODEL, bn), lambda j: (0, j)),
            pl.BlockSpec((1, bn), lambda j: (0, j)),
        ],
        out_specs=pl.BlockSpec((BATCH, bn), lambda j: (0, j)),
        out_shape=jax.ShapeDtypeStruct((BATCH, n), F32),
        compiler_params=_params(1),
        name="mod",
    )(c, w_ada, b_ada.reshape(1, n))


def _inproj_kernel(x_ref, sc_ref, sh_ref, ng_ref, w_ref, wvt_ref, qg_ref, kg_ref, lng_ref, lnb_ref, bg_ref, hp_ref,
                   he_ref, qaug_ref, q_o, k_o, vt_o, u_o, vg_o, gt_o):
    x = x_ref[...]
    ms = jnp.mean(x * x, axis=-1, keepdims=True)
    h = (x * lax.rsqrt(ms + EPS)) * ng_ref[...]
    h = h * (1.0 + sc_ref[0]) + sh_ref[0]
    hb = h.astype(BF16)

    def proj(c0, c1):
        return _dot(hb, w_ref[:, c0:c1])

    def head_rmsnorm(raw, pool, expand, gain):
        ssq = _dot((raw * raw).astype(BF16), pool)
        r = lax.rsqrt(ssq * (1.0 / HEAD_DIM) + EPS)
        r_hi = r.astype(BF16)
        r_lo = (r - r_hi.astype(F32)).astype(BF16)
        return raw * (_dot(r_hi, expand) + _dot(r_lo, expand)) * gain

    low_half = lax.broadcasted_iota(I32, (1, HEAD_SLOT), 1) < HEAD_DIM

    def store_head_slots(xn, o_ref, spare):
        for p in range(xn.shape[1] // HEAD_SLOT):
            pair = xn[:, p * HEAD_SLOT:(p + 1) * HEAD_SLOT]
            for hd, head in ((2 * p, pair), (2 * p + 1, pltpu.roll(pair, HEAD_DIM, 1))):
                cols = slice(hd * HEAD_SLOT, (hd + 1) * HEAD_SLOT)
                fill = 0.0 if spare is None else spare[:, cols]
                o_ref[:, cols] = jnp.where(low_half, head, fill).astype(BF16)

    store_head_slots(head_rmsnorm(proj(Q0, Q1), hp_ref[...], he_ref[...], qg_ref[...]), q_o, qaug_ref)
    store_head_slots(head_rmsnorm(proj(K0, K1), hp_ref[0:KV_WIDTH, :], he_ref[:, 0:KV_WIDTH], kg_ref[...]), k_o, None)
    vt_o[...] = _dot_nt(wvt_ref[...], hb).astype(BF16)
    u_o[...] = _gelu(proj(U0, U1)).astype(BF16)
    vg = _gelu(proj(G0, G1))
    mu = jnp.mean(vg, axis=-1, keepdims=True)
    vc = vg - mu
    var = jnp.mean(vc * vc, axis=-1, keepdims=True)
    vg_o[...] = (vc * lax.rsqrt(var + EPS) * lng_ref[...] + lnb_ref[...]).astype(BF16)
    gt_o[...] = _sigmoid(proj(B0, B1) + bg_ref[...]).astype(BF16)


def _inproj_call(x2, scale1, shift1, norm_gain, w_in, wv_t, q_gain_row, k_gain_row, ln_gain, ln_bias, b_gate, pool,
                 expand, q_aug):
    tm = TM_PROJ
    per_batch = SEQ // tm
    row = lambda n: pl.BlockSpec((1, n), lambda i: (0, 0))
    modrow = pl.BlockSpec((1, 1, D_MODEL), lambda i: (i // per_batch, 0, 0))
    tile = lambda n: pl.BlockSpec((tm, n), lambda i: (i, 0))
    out = lambda n: jax.ShapeDtypeStruct((TOKENS, n), BF16)
    return pl.pallas_call(
        _inproj_kernel,
        grid=(TOKENS // tm,),
        in_specs=[
            tile(D_MODEL), modrow, modrow, row(D_MODEL),
            pl.BlockSpec((D_MODEL, IN_WIDTH), lambda i: (0, 0)),
            pl.BlockSpec((KV_WIDTH, D_MODEL), lambda i: (0, 0)),
            row(ATTN_WIDTH), row(KV_WIDTH), row(GMLP_WIDTH), row(GMLP_WIDTH), row(2 * D_MODEL),
            pl.BlockSpec((ATTN_WIDTH, 128), lambda i: (0, 0)),
            pl.BlockSpec((128, ATTN_WIDTH), lambda i: (0, 0)),
            row(N_HEADS * HEAD_SLOT),
        ],
        out_specs=[tile(N_HEADS * HEAD_SLOT), tile(N_KV_HEADS * HEAD_SLOT),
                   pl.BlockSpec((KV_WIDTH, tm), lambda i: (0, i)),
                   tile(GMLP_WIDTH), tile(GMLP_WIDTH), tile(2 * D_MODEL)],
        out_shape=[out(N_HEADS * HEAD_SLOT), out(N_KV_HEADS * HEAD_SLOT),
                   jax.ShapeDtypeStruct((KV_WIDTH, TOKENS), BF16),
                   out(GMLP_WIDTH), out(GMLP_WIDTH), out(2 * D_MODEL)],
        compiler_params=_params(1),
        name="inproj",
    )(x2, scale1, shift1, norm_gain, w_in, wv_t, q_gain_row, k_gain_row, ln_gain, ln_bias, b_gate, pool, expand,
      q_aug)


def _attn_kernel(sink_ref, q_ref, kc_ref, kp_ref, vc_ref, vp_ref, o_ref):
    i = pl.program_id(1)
    slopes = _alibi_slopes()
    kj = lax.broadcasted_iota(I32, (BLOCK, 2 * BLOCK), 0)
    qi = lax.broadcasted_iota(I32, (BLOCK, 2 * BLOCK), 1) & (BLOCK - 1)
    own = kj <= qi
    from_own = jnp.where(own, 1.0, 0.0).astype(BF16)
    from_prev = jnp.where(own, 0.0, 1.0).astype(BF16)
    has_prev = i > 0
    low_half = lax.broadcasted_iota(I32, (1, HEAD_SLOT), 1) < HEAD_DIM
    k_lane = lax.broadcasted_iota(I32, (2 * BLOCK, HEAD_SLOT), 1)
    k_aug = jnp.where((k_lane >= HEAD_DIM) & (k_lane < HEAD_DIM + ALIBI_PARTS),
                      lax.broadcasted_iota(I32, (2 * BLOCK, HEAD_SLOT), 0), 0).astype(F32).astype(BF16)
    q_pos = (lax.broadcasted_iota(I32, (1, BLOCK), 1) + BLOCK).astype(F32)

    keys_of = {}

    def scores(item):
        sb, kv, pr = item
        r0, r1 = sb * BLOCK, (sb + 1) * BLOCK
        if (sb, kv) not in keys_of:
            cols = slice(kv * HEAD_SLOT, (kv + 1) * HEAD_SLOT)
            k_prev = kp_ref[:, cols] if sb == 0 else kc_ref[r0 - BLOCK:r0, cols]
            keys_of[sb, kv] = jnp.where(low_half, jnp.concatenate([k_prev, kc_ref[r0:r1, cols]], axis=0), k_aug)
        ha = kv * Q_PER_KV + 2 * pr
        queries = jnp.concatenate([q_ref[r0:r1, ha * HEAD_SLOT:(ha + 1) * HEAD_SLOT],
                                   q_ref[r0:r1, (ha + 1) * HEAD_SLOT:(ha + 2) * HEAD_SLOT]], axis=0)
        s = _dot_nt(keys_of[sb, kv], queries)
        s_prev = s[0:BLOCK, :]
        if sb == 0:
            s_prev = jnp.where(has_prev, s_prev, NEG_INF)
        return jnp.where(own, s[BLOCK:2 * BLOCK, :], s_prev)

    def attend(item, s):
        sb, kv, pr = item
        r0, r1 = sb * BLOCK, (sb + 1) * BLOCK
        rows = slice(kv * HEAD_DIM, (kv + 1) * HEAD_DIM)
        v_prev = vp_ref[rows, :] if sb == 0 else vc_ref[rows, r0 - BLOCK:r0]
        vt = jnp.concatenate([v_prev, vc_ref[rows, r0:r1]], axis=1)
        ha = kv * Q_PER_KV + 2 * pr
        hb = ha + 1
        sink = LOG2E * jnp.concatenate(
            [sink_ref[ha] + slopes[ha] * q_pos, sink_ref[hb] + slopes[hb] * q_pos], axis=1)
        m = jnp.maximum(jnp.max(s, axis=0, keepdims=True), sink)
        p = jnp.exp2(s - m)
        den = jnp.sum(p, axis=0, keepdims=True) + jnp.exp2(sink - m)
        pb = p.astype(BF16)
        p_keys = jnp.concatenate([pb * from_prev, pb * from_own], axis=0)
        o = _dot(vt, p_keys) / den
        pair = jnp.concatenate([o[:, 0:BLOCK], o[:, BLOCK:2 * BLOCK]], axis=0).T
        o_ref[r0:r1, ha * HEAD_DIM:(hb + 1) * HEAD_DIM] = pair.astype(BF16)

    items = [(sb, kv, pr) for sb in range(TM_ATTN // BLOCK) for kv in range(N_KV_HEADS)
             for pr in range(Q_PER_KV // 2)]
    pending = []
    for n in range(len(items) + SCORES_AHEAD):
        if n < len(items):
            pending.append(scores(items[n]))
        if n >= SCORES_AHEAD:
            attend(items[n - SCORES_AHEAD], pending.pop(0))


def _attn_call(sinks, q, k, vt):
    tq = TM_ATTN
    per_batch = SEQ // tq
    blocks_per_tile = tq // BLOCK
    q_width = N_HEADS * HEAD_SLOT
    k_width = N_KV_HEADS * HEAD_SLOT
    cur = lambda n: pl.BlockSpec((tq, n), lambda b, i: (b * per_batch + i, 0))
    prev_block = lambda b, i: b * (SEQ // BLOCK) + jnp.maximum(i * blocks_per_tile - 1, 0)
    return pl.pallas_call(
        _attn_kernel,
        grid=(BATCH, per_batch),
        in_specs=[
            pl.BlockSpec(memory_space=pltpu.SMEM),
            cur(q_width), cur(k_width),
            pl.BlockSpec((BLOCK, k_width), lambda b, i: (prev_block(b, i), 0)),
            pl.BlockSpec((KV_WIDTH, tq), lambda b, i: (0, b * per_batch + i)),
            pl.BlockSpec((KV_WIDTH, BLOCK), lambda b, i: (0, prev_block(b, i))),
        ],
        out_specs=cur(ATTN_WIDTH),
        out_shape=jax.ShapeDtypeStruct((TOKENS, ATTN_WIDTH), BF16),
        compiler_params=_params(2),
        name="attn",
    )(sinks, q, k, k, vt, vt)


def _first_argmax(vals):
    m = vals[0]
    for v in vals[1:]:
        m = jnp.maximum(m, v)
    idx = jnp.full(m.shape, len(vals) - 1, I32)
    for k in range(len(vals) - 2, -1, -1):
        idx = jnp.where(vals[k] == m, k, idx)
    return m, idx


def _mixer_kernel(x_ref, ya_ref, u_ref, vg_ref, gt_ref, ws_ref, bmap_ref, woa_ref, wog_ref, wout_ref, gate1_ref,
                  ng_ref, sc_ref, sh_ref, wrt_ref, brt_ref, x1_o, h2_o, bk_o):
    tm = TM_MIX
    t_idx = lax.broadcasted_iota(I32, (GMLP_CHUNK, GMLP_CHUNK), 0)
    s_idx = lax.broadcasted_iota(I32, (GMLP_CHUNK, GMLP_CHUNK), 1)
    ws = [jnp.where(t_idx >= s_idx, ws_ref[g], 0.0).astype(BF16) for g in range(GMLP_GROUPS)]
    gc = GMLP_WIDTH // GMLP_GROUPS
    chunks = []
    for c in range(tm // GMLP_CHUNK):
        r0, r1 = c * GMLP_CHUNK, (c + 1) * GMLP_CHUNK
        mixed = jnp.concatenate(
            [_dot(ws[g], vg_ref[r0:r1, g * gc:(g + 1) * gc]) for g in range(GMLP_GROUPS)], axis=1)
        mixed = mixed + bmap_ref[...]
        chunks.append((u_ref[r0:r1, :].astype(F32) * mixed).astype(BF16))
    y_gmlp = jnp.concatenate(chunks, axis=0)
    pa = _dot(ya_ref[...], woa_ref[...])
    pg = _dot(y_gmlp, wog_ref[...])
    merged = gt_ref[:, 0:D_MODEL].astype(F32) * pa + gt_ref[:, D_MODEL:2 * D_MODEL].astype(F32) * pg
    x1 = x_ref[...] + gate1_ref[0] * _dot(merged.astype(BF16), wout_ref[...])
    x1_o[...] = x1

    ms = jnp.mean(x1 * x1, axis=-1, keepdims=True)
    h2 = (x1 * lax.rsqrt(ms + EPS)) * ng_ref[...]
    h2 = h2 * (1.0 + sc_ref[0]) + sh_ref[0]
    _store_slabs(h2_o, h2)

    lg = _dot_nt(wrt_ref[...], h2.astype(BF16)) + brt_ref[...]
    _, gi = _first_argmax([lg[r:r + 1, :] for r in range(N_EXPERT_GROUPS)])
    el = [lg[8 + r:9 + r, :] for r in range(N_EXPERTS)]
    eg = []
    for k in range(EXPERTS_PER_GROUP):
        v = el[(N_EXPERT_GROUPS - 1) * EXPERTS_PER_GROUP + k]
        for g in range(N_EXPERT_GROUPS - 2, -1, -1):
            v = jnp.where(gi == g, el[g * EXPERTS_PER_GROUP + k], v)
        eg.append(v)
    _, i1 = _first_argmax(eg)
    _, i2 = _first_argmax([jnp.where(i1 == k, -3e38, eg[k]) for k in range(EXPERTS_PER_GROUP)])
    lo = jnp.minimum(i1, i2)
    hi = jnp.maximum(i1, i2)
    pair = jnp.where(lo == 0, hi - 1, jnp.where(lo == 1, hi + 1, 5))
    bk_o[0] = gi * len(PAIRS) + pair


def _mixer_call(x2, y_attn, u, vg, gates, w_spatial, bias_map, wo_attn, wo_gmlp, w_out, gate1, norm_gain, scale2, shift2,
                wr_t, br_t):
    tm = TM_MIX
    per_batch = SEQ // tm
    nt = TOKENS // tm
    tile = lambda n: pl.BlockSpec((tm, n), lambda i: (i, 0))
    full = lambda shape: pl.BlockSpec(shape, lambda i: (0,) * len(shape))
    modrow = pl.BlockSpec((1, 1, D_MODEL), lambda i: (i // per_batch, 0, 0))
    return pl.pallas_call(
        _mixer_kernel,
        grid=(nt,),
        in_specs=[
            tile(D_MODEL), tile(ATTN_WIDTH), tile(GMLP_WIDTH), tile(GMLP_WIDTH), tile(2 * D_MODEL),
            full((GMLP_GROUPS, GMLP_CHUNK, GMLP_CHUNK)), full((GMLP_CHUNK, GMLP_WIDTH)),
            full((ATTN_WIDTH, D_MODEL)), full((GMLP_WIDTH, D_MODEL)), full((D_MODEL, D_MODEL)),
            modrow, full((1, D_MODEL)), modrow, modrow,
            full((BUCKET_ROWS, D_MODEL)), full((BUCKET_ROWS, 1)),
        ],
        out_specs=[tile(D_MODEL), pl.BlockSpec((tm * SLAB, 128), lambda i: (i, 0)),
                   pl.BlockSpec((1, 1, tm), lambda i: (i, 0, 0))],
        out_shape=[
            jax.ShapeDtypeStruct((TOKENS, D_MODEL), F32),
            jax.ShapeDtypeStruct((TOKENS * SLAB, 128), F32),
            jax.ShapeDtypeStruct((nt, 1, tm), I32),
        ],
        compiler_params=_params(1),
        name="mixer",
    )(x2, y_attn, u, vg, gates, w_spatial, bias_map, wo_attn, wo_gmlp, w_out, gate1, norm_gain, scale2, shift2,
      wr_t, br_t)


def _rank_kernel(bk_ref, pos_o, meta_o, cnt, off):
    tm = TM_MIX
    ph = pl.program_id(0)
    i = pl.program_id(1)
    bk = bk_ref[0]
    sub = lax.broadcasted_iota(I32, (BUCKET_ROWS, tm), 0)
    onehot = sub == bk

    @pl.when((ph == 0) & (i == 0))
    def _():
        cnt[...] = jnp.zeros_like(cnt)

    @pl.when(ph == 0)
    def _():
        cnt[...] += jnp.sum(jnp.where(onehot, 1.0, 0.0), axis=1, keepdims=True)

    @pl.when((ph == 1) & (i == 0))
    def _():
        tiles = jnp.floor((cnt[...] + (TM_MOE - 1)) * (1.0 / TM_MOE))
        r = lax.broadcasted_iota(I32, (BUCKET_ROWS, BUCKET_ROWS), 0)
        c = lax.broadcasted_iota(I32, (BUCKET_ROWS, BUCKET_ROWS), 1)
        before = jnp.where(c < r, 1.0, 0.0).astype(BF16)
        first_tile = _dot(before, tiles.astype(BF16))
        off[...] = first_tile * TM_MOE
        cnt[...] = jnp.zeros_like(cnt)
        end_tile = (first_tile + tiles)[:, 0:1]
        tj = lax.broadcasted_iota(I32, (BUCKET_ROWS, META_LANES), 1).astype(F32)
        bsub = lax.broadcasted_iota(I32, (BUCKET_ROWS, META_LANES), 0)
        passed = jnp.where((tj >= end_tile) & (bsub < N_BUCKETS), 1.0, 0.0)
        tile_bucket = jnp.sum(passed, axis=0, keepdims=True)
        n_tiles = jnp.sum(jnp.where(bsub == N_BUCKETS - 1, jnp.broadcast_to(end_tile, passed.shape), 0.0),
                          axis=0, keepdims=True)
        row = lax.broadcasted_iota(I32, (8, META_LANES), 0)
        meta = jnp.where(row == 0, jnp.broadcast_to(tile_bucket, (8, META_LANES)),
                         jnp.where(row == 1, jnp.broadcast_to(n_tiles, (8, META_LANES)), 0.0))
        meta_o[...] = meta.astype(I32)

    @pl.when(ph == 1)
    def _():
        s_idx = lax.broadcasted_iota(I32, (tm, tm), 0)
        t_idx = lax.broadcasted_iota(I32, (tm, tm), 1)
        upto = jnp.where(s_idx <= t_idx, 1.0, 0.0).astype(BF16)
        incl = _dot(jnp.where(onehot, 1.0, 0.0).astype(BF16), upto)
        base = off[:, 0:1] + cnt[:, 0:1]
        posf = jnp.sum(jnp.where(onehot, base + incl - 1.0, 0.0), axis=0, keepdims=True)
        pos_o[0] = posf.astype(I32)
        cnt[...] += incl[:, tm - 1:tm]


def _rank_call(buckets):
    nt, _, tm = buckets.shape
    return pl.pallas_call(
        _rank_kernel,
        grid=(2, nt),
        in_specs=[pl.BlockSpec((1, 1, tm), lambda ph, i: (i, 0, 0))],
        out_specs=[
            pl.BlockSpec((1, 1, tm), lambda ph, i: (ph * i, 0, 0)),
            pl.BlockSpec((8, META_LANES), lambda ph, i: (0, 0)),
        ],
        out_shape=[
            jax.ShapeDtypeStruct((nt, 1, tm), I32),
            jax.ShapeDtypeStruct((8, META_LANES), I32),
        ],
        scratch_shapes=[pltpu.VMEM((BUCKET_ROWS, 128), F32), pltpu.VMEM((BUCKET_ROWS, 128), F32)],
        compiler_params=_params(2),
        name="rank",
    )(buckets)


def _store_slabs(ref, x):
    n = x.shape[0]
    for s in range(SLAB):
        ref[pl.ds(s, n, stride=SLAB), :] = x[:, s * 128:(s + 1) * 128]


def _load_slabs(ref, n):
    return jnp.concatenate([ref[pl.ds(s, n, stride=SLAB), :] for s in range(SLAB)], axis=1)


def _row_copy(src, src_row, dst, dst_row, sem):
    return pltpu.make_async_copy(src.at[pl.ds(pl.multiple_of(src_row * SLAB, SLAB), SLAB)],
                                 dst.at[pl.ds(pl.multiple_of(dst_row * SLAB, SLAB), SLAB)], sem)


def _start_tile_rows(copy_of_row):
    def trip(c, carry):
        for k in range(ROW_UNROLL):
            copy_of_row(c * ROW_UNROLL + k).start(priority=k % 2)
        return carry
    lax.fori_loop(0, TM_ROW // ROW_UNROLL, trip, 0)


def _tile_wait(src, dst, sem):
    pltpu.make_async_copy(src.at[pl.ds(0, TM_ROW * SLAB)], dst, sem).wait()


def _scatter_kernel(pos_ref, h_ref, init_ref, o_ref, sem):
    del init_ref
    _start_tile_rows(lambda r: _row_copy(h_ref, r, o_ref, pos_ref[0, 0, r], sem))
    _tile_wait(h_ref, o_ref.at[pl.ds(0, TM_ROW * SLAB)], sem)


def _scatter_call(pos, h2):
    tm = TM_ROW
    init = jnp.zeros((SORTED_ROWS * SLAB, 128), F32)
    return pl.pallas_call(
        _scatter_kernel,
        grid=(TOKENS // tm,),
        in_specs=[
            pl.BlockSpec((1, 1, tm), lambda i: (i, 0, 0), memory_space=pltpu.SMEM),
            pl.BlockSpec((tm * SLAB, 128), lambda i: (i, 0)),
            pl.BlockSpec(memory_space=pl.ANY),
        ],
        out_specs=pl.BlockSpec(memory_space=pl.ANY),
        out_shape=jax.ShapeDtypeStruct((SORTED_ROWS * SLAB, 128), F32),
        scratch_shapes=[pltpu.SemaphoreType.DMA(())],
        input_output_aliases={2: 0},
        compiler_params=_params(1),
        name="scatter",
    )(pos, h2, init)


def _gather_kernel(pos_ref, pos_next_ref, ys_ref, x1_ref, gate2_ref, o_ref, buf, sem):
    i = pl.program_id(0)
    n = pl.num_programs(0)
    slot = lax.rem(i, 2)

    def fetch(p_ref, s):
        _start_tile_rows(lambda r: _row_copy(ys_ref, p_ref[0, 0, r], buf.at[s], r, sem.at[s]))

    @pl.when(i == 0)
    def _():
        fetch(pos_ref, 0)

    @pl.when(i + 1 < n)
    def _():
        fetch(pos_next_ref, 1 - slot)

    _tile_wait(ys_ref, buf.at[slot], sem.at[slot])
    o_ref[...] = x1_ref[...] + gate2_ref[0] * _load_slabs(buf.at[slot], TM_ROW)


def _gather_call(pos, y_sorted, x1, gate2):
    tm = TM_ROW
    per_batch = SEQ // tm
    n = TOKENS // tm
    return pl.pallas_call(
        _gather_kernel,
        grid=(n,),
        in_specs=[
            pl.BlockSpec((1, 1, tm), lambda i: (i, 0, 0), memory_space=pltpu.SMEM),
            pl.BlockSpec((1, 1, tm), lambda i: (jnp.minimum(i + 1, n - 1), 0, 0), memory_space=pltpu.SMEM),
            pl.BlockSpec(memory_space=pl.ANY),
            pl.BlockSpec((tm, D_MODEL), lambda i: (i, 0)),
            pl.BlockSpec((1, 1, D_MODEL), lambda i: (i // per_batch, 0, 0)),
        ],
        out_specs=pl.BlockSpec((tm, D_MODEL), lambda i: (i, 0)),
        out_shape=jax.ShapeDtypeStruct((TOKENS, D_MODEL), F32),
        scratch_shapes=[pltpu.VMEM((2, tm * SLAB, 128), F32), pltpu.SemaphoreType.DMA((2,))],
        compiler_params=_params(1),
        name="gather",
    )(pos, pos, y_sorted, x1, gate2)


def _moe_kernel(tb_ref, nt_ref, tab_ref, h_ref, wr_ref, br_ref, wg_lo, wu_lo, wd_lo, wg_hi, wu_hi, wd_hi, o_ref):
    j = pl.program_id(0)

    @pl.when(j < nt_ref[0])
    def _():
        b = tb_ref[j]
        e_lo = tab_ref[b]
        e_hi = tab_ref[N_BUCKETS + b]
        g = tab_ref[2 * N_BUCKETS + b]
        h = _load_slabs(h_ref, TM_MOE).astype(BF16)
        lg = _dot(h, wr_ref[...]) + br_ref[...]
        lane = lax.broadcasted_iota(I32, lg.shape, 1)
        is_group = lane < N_EXPERT_GROUPS
        gmax = jnp.max(jnp.where(is_group, lg, NEG_INF), axis=1, keepdims=True)
        ge = jnp.exp(lg - gmax)
        g_w = (jnp.sum(jnp.where(lane == g, ge, 0.0), axis=1, keepdims=True)
               / jnp.sum(jnp.where(is_group, ge, 0.0), axis=1, keepdims=True))
        v_lo = jnp.sum(jnp.where(lane == N_EXPERT_GROUPS + e_lo, lg, 0.0), axis=1, keepdims=True)
        v_hi = jnp.sum(jnp.where(lane == N_EXPERT_GROUPS + e_hi, lg, 0.0), axis=1, keepdims=True)
        m = jnp.maximum(v_lo, v_hi)
        x_lo = jnp.exp(v_lo - m)
        x_hi = jnp.exp(v_hi - m)
        w_lo = x_lo / (x_lo + x_hi) * g_w
        w_hi = x_hi / (x_lo + x_hi) * g_w

        def hidden(wg, wu, w):
            a = _dot(h, wg[0])
            return (a * _sigmoid(a) * _dot(h, wu[0]) * w).astype(BF16)

        y = _dot(hidden(wg_lo, wu_lo, w_lo), wd_lo[0]) + _dot(hidden(wg_hi, wu_hi, w_hi), wd_hi[0])
        _store_slabs(o_ref, y)

    @pl.when(j >= nt_ref[0])
    def _():
        o_ref[...] = jnp.zeros_like(o_ref)


def _moe_call(tile_bucket, n_tiles, tables, h_sorted, wr, br, w_gate, w_up, w_down):
    tm = TM_MOE

    def last(j, nt):
        return jnp.minimum(j, nt[0] - 1)

    def w_spec(shape, which):
        return pl.BlockSpec(
            (1,) + shape, lambda j, tb, nt, tab: (tab[which * N_BUCKETS + tb[last(j, nt)]], 0, 0))

    up = (D_MODEL, D_EXPERT)
    down = (D_EXPERT, D_MODEL)
    grid_spec = pltpu.PrefetchScalarGridSpec(
        num_scalar_prefetch=3,
        grid=(NT_MOE,),
        in_specs=[
            pl.BlockSpec((tm * SLAB, 128), lambda j, tb, nt, tab: (last(j, nt), 0)),
            pl.BlockSpec((D_MODEL, 128), lambda j, tb, nt, tab: (0, 0)),
            pl.BlockSpec((1, 128), lambda j, tb, nt, tab: (0, 0)),
            w_spec(up, 0), w_spec(up, 0), w_spec(down, 0),
            w_spec(up, 1), w_spec(up, 1), w_spec(down, 1),
        ],
        out_specs=pl.BlockSpec((tm * SLAB, 128), lambda j, tb, nt, tab: (j, 0)),
    )
    return pl.pallas_call(
        _moe_kernel,
        grid_spec=grid_spec,
        out_shape=jax.ShapeDtypeStruct((SORTED_ROWS * SLAB, 128), F32),
        compiler_params=_params(1),
        name="moe",
    )(tile_bucket, n_tiles, tables, h_sorted, wr, br, w_gate, w_up, w_down, w_gate, w_up, w_down)


def _alibi_query_lanes():
    lanes = np.zeros((1, N_HEADS * HEAD_SLOT), np.float32)
    for hd, slope in enumerate(_alibi_slopes()):
        rest = np.float32(slope * LOG2E)
        for part in range(ALIBI_PARTS):
            piece = np.float32(np.asarray(rest).astype(jnp.bfloat16))
            lanes[0, hd * HEAD_SLOT + HEAD_DIM + part] = piece
            rest = np.float32(rest - piece)
    return jnp.asarray(lanes)


def _bucket_tables():
    e_lo, e_hi, grp = [], [], []
    for g in range(N_EXPERT_GROUPS):
        for lo, hi in PAIRS:
            e_lo.append(g * EXPERTS_PER_GROUP + lo)
            e_hi.append(g * EXPERTS_PER_GROUP + hi)
            grp.append(g)
    return jnp.asarray(e_lo + e_hi + grp, I32)


def kernel(x, c, w_ada, b_ada, norm1_gain, w_in, b_branch_gate, q_norm_gain, k_norm_gain, attn_sinks, gmlp_norm_gain, gmlp_norm_bias, gmlp_w_spatial, gmlp_b_spatial, w_o_attn, w_o_gmlp, w_out, norm2_gain, w_group_router, b_group_router, w_expert_router, b_expert_router, w_expert_gate, w_expert_up, w_expert_down):
    depth = w_ada.shape[0]
    x2 = x.reshape(TOKENS, D_MODEL)
    head_of_col = np.arange(ATTN_WIDTH) // HEAD_DIM
    pool = jnp.asarray(head_of_col[:, None] == np.arange(128)[None, :], BF16)
    expand = jnp.asarray(np.arange(128)[:, None] == head_of_col[None, :], BF16)
    tables = _bucket_tables()
    q_aug = _alibi_query_lanes()
    row = lambda v: v.reshape(1, -1)
    for l in range(depth):
        mod = _mod_call(c, w_ada[l], b_ada[l])
        shift1, scale1, gate1, shift2, scale2, gate2 = [
            m.reshape(BATCH, 1, D_MODEL) for m in jnp.split(mod, 6, axis=-1)]

        q_gain_row = row(jnp.tile(q_norm_gain[l], N_HEADS) * (HEAD_DIM ** -0.5 * LOG2E))
        k_gain_row = row(jnp.tile(k_norm_gain[l], N_KV_HEADS))
        q, k, vt, u, vg, gates = _inproj_call(
            x2, scale1, shift1, row(norm1_gain[l]), w_in[l].astype(BF16), w_in[l][:, V0:V1].T.astype(BF16),
            q_gain_row, k_gain_row, row(gmlp_norm_gain[l]), row(gmlp_norm_bias[l]), row(b_branch_gate[l]),
            pool, expand, q_aug)

        y_attn = _attn_call(attn_sinks[l], q, k, vt)

        bias_map = jnp.repeat(gmlp_b_spatial[l].T, GMLP_WIDTH // GMLP_GROUPS, axis=1)
        wr_t = jnp.zeros((BUCKET_ROWS, D_MODEL), F32)
        wr_t = wr_t.at[0:N_EXPERT_GROUPS].set(w_group_router[l].T).at[8:8 + N_EXPERTS].set(w_expert_router[l].T)
        br_t = jnp.zeros((BUCKET_ROWS, 1), F32)
        br_t = br_t.at[0:N_EXPERT_GROUPS, 0].set(b_group_router[l]).at[8:8 + N_EXPERTS, 0].set(b_expert_router[l])
        x1, h2p, buckets = _mixer_call(
            x2, y_attn, u, vg, gates, gmlp_w_spatial[l], bias_map, w_o_attn[l].astype(BF16),
            w_o_gmlp[l].astype(BF16), w_out[l].astype(BF16), gate1, row(norm2_gain[l]), scale2, shift2,
            wr_t.astype(BF16), br_t)

        pos, meta = _rank_call(buckets)
        h_sorted = _scatter_call(pos, h2p)

        wr = jnp.zeros((D_MODEL, 128), F32)
        wr = wr.at[:, 0:N_EXPERT_GROUPS].set(w_group_router[l])
        wr = wr.at[:, N_EXPERT_GROUPS:N_EXPERT_GROUPS + N_EXPERTS].set(w_expert_router[l])
        br = jnp.zeros((1, 128), F32)
        br = br.at[0, 0:N_EXPERT_GROUPS].set(b_group_router[l])
        br = br.at[0, N_EXPERT_GROUPS:N_EXPERT_GROUPS + N_EXPERTS].set(b_expert_router[l])
        flat = lambda w: w.reshape((N_EXPERTS,) + w.shape[2:]).astype(BF16)
        y_sorted = _moe_call(
            meta[0], meta[1, 0:1], tables, h_sorted, wr.astype(BF16), br,
            flat(w_expert_gate[l]), flat(w_expert_up[l]), flat(w_expert_down[l]))

        x2 = _gather_call(pos, y_sorted, x1, gate2)
    return x2.reshape(x.shape)
```

```python
import functools

import jax
import jax.numpy as jnp
import numpy as np
from jax import lax
from jax.experimental import pallas as pl
from jax.experimental.pallas import tpu as pltpu

D_MODEL = 1024
BATCH = 8
SEQ = 4096
TOKENS = BATCH * SEQ
N_HEADS = 16
N_KV_HEADS = 4
HEAD_DIM = 64
Q_PER_KV = N_HEADS // N_KV_HEADS
BLOCK = 128
HEAD_SLOT = 128
ALIBI_PARTS = 3
ATTN_WIDTH = N_HEADS * HEAD_DIM
KV_WIDTH = N_KV_HEADS * HEAD_DIM
GMLP_WIDTH = 1024
GMLP_GROUPS = 8
GMLP_CHUNK = 128
N_EXPERT_GROUPS = 4
EXPERTS_PER_GROUP = 4
N_EXPERTS = N_EXPERT_GROUPS * EXPERTS_PER_GROUP
D_EXPERT = 512
IN_WIDTH = ATTN_WIDTH + 2 * KV_WIDTH + 2 * GMLP_WIDTH + 2 * D_MODEL
EPS = 1e-6
NEG_INF = -1e30
LOG2E = float(np.float32(np.log2(np.e)))

Q0, Q1 = 0, ATTN_WIDTH
K0, K1 = Q1, Q1 + KV_WIDTH
V0, V1 = K1, K1 + KV_WIDTH
U0, U1 = V1, V1 + GMLP_WIDTH
G0, G1 = U1, U1 + GMLP_WIDTH
B0, B1 = G1, G1 + 2 * D_MODEL

PAIRS = ((0, 1), (0, 2), (0, 3), (1, 2), (1, 3), (2, 3))
N_BUCKETS = N_EXPERT_GROUPS * len(PAIRS)
BUCKET_ROWS = 32
EXPERT_ROW0 = 8

TM_PROJ = 512
TM_ATTN = 512
TM_MIX = 512
TM_MOE = 256
TM_ROW = 512
SCORES_AHEAD = 6
ROW_UNROLL = 16
NT_MOE = TOKENS // TM_MOE + N_BUCKETS
SORTED_ROWS = NT_MOE * TM_MOE
SLAB = D_MODEL // 128
META_LANES = 256
assert NT_MOE <= META_LANES

VMEM_LIMIT = 56 * 1024 * 1024

F32 = jnp.float32
BF16 = jnp.bfloat16
U32 = jnp.uint32
I32 = jnp.int32


def _alibi_slopes():
    return [float(np.float32(2.0 ** (-8.0 * (i + 1) / N_HEADS))) for i in range(N_HEADS)]


def _params(n_axes):
    return pltpu.CompilerParams(dimension_semantics=("arbitrary",) * n_axes, vmem_limit_bytes=VMEM_LIMIT)


def _sigmoid(x):
    return 1.0 / (1.0 + jnp.exp(-x))


def _gelu(x):
    return 0.5 * x * (1.0 + lax.erf(x * np.float32(1.0 / np.sqrt(2.0))))


def _dot(a, b):
    return jnp.dot(a, b, preferred_element_type=F32)


def _dot_nt(a, b):
    return lax.dot_general(a, b, (((1,), (1,)), ((), ())), preferred_element_type=F32)


def _mod_kernel(c_ref, w_ref, b_ref, o_ref):
    c = c_ref[...]
    ca = c * _sigmoid(c)
    o_ref[...] = _dot(ca.astype(BF16), w_ref[...].astype(BF16)) + b_ref[...]


def _mod_call(c, w_ada, b_ada):
    n = w_ada.shape[1]
    bn = 1536
    return pl.pallas_call(
        _mod_kernel,
        grid=(n // bn,),
        in_specs=[
            pl.BlockSpec((BATCH, D_MODEL), lambda j: (0, 0)),
            pl.BlockSpec((D_MODEL, bn), lambda j: (0, j)),
            pl.BlockSpec((1, bn), lambda j: (0, j)),
        ],
        out_specs=pl.BlockSpec((BATCH, bn), lambda j: (0, j)),
        out_shape=jax.ShapeDtypeStruct((BATCH, n), F32),
        compiler_params=_params(1),
        name="mod",
    )(c, w_ada, b_ada.reshape(1, n))


def _inproj_kernel(x_ref, sc_ref, sh_ref, ng_ref, w_ref, wvt_ref, qg_ref, kg_ref, lng_ref, lnb_ref, bg_ref, qaug_ref,
                   q_o, k_o, vt_o, u_o, vg_o, gt_o):
    x = x_ref[...]
    ms = jnp.mean(x * x, axis=-1, keepdims=True)
    h = (x * lax.rsqrt(ms + EPS)) * ng_ref[...]
    h = h * (1.0 + sc_ref[0]) + sh_ref[0]
    hb = h.astype(BF16)

    def proj(c0, c1):
        return _dot(hb, w_ref[:, c0:c1])

    low_half = lax.broadcasted_iota(I32, (1, HEAD_SLOT), 1) < HEAD_DIM

    def store_normed_heads(raw, gain_ref, spare_ref, o_ref):
        for p in range(raw.shape[1] // HEAD_SLOT):
            pair = raw[:, p * HEAD_SLOT:(p + 1) * HEAD_SLOT]
            for hd, head in ((2 * p, pair), (2 * p + 1, pltpu.roll(pair, HEAD_DIM, 1))):
                cols = slice(hd * HEAD_SLOT, (hd + 1) * HEAD_SLOT)
                xh = jnp.where(low_half, head, 0.0)
                r = lax.rsqrt(jnp.sum(xh * xh, axis=-1, keepdims=True) * (1.0 / HEAD_DIM) + EPS)
                y = xh * r * gain_ref[:, cols]
                if spare_ref is not None:
                    y = y + spare_ref[:, cols]
                o_ref[:, cols] = y.astype(BF16)

    store_normed_heads(proj(Q0, Q1), qg_ref, qaug_ref, q_o)
    store_normed_heads(proj(K0, K1), kg_ref, None, k_o)
    vt_o[...] = _dot_nt(wvt_ref[...], hb).astype(BF16)
    u_o[...] = _gelu(proj(U0, U1)).astype(BF16)
    vg = _gelu(proj(G0, G1))
    mu = jnp.mean(vg, axis=-1, keepdims=True)
    vc = vg - mu
    var = jnp.mean(vc * vc, axis=-1, keepdims=True)
    vg_o[...] = (vc * lax.rsqrt(var + EPS) * lng_ref[...] + lnb_ref[...]).astype(BF16)
    gt_o[...] = _sigmoid(proj(B0, B1) + bg_ref[...]).astype(BF16)


def _inproj_call(x2, scale1, shift1, norm_gain, w_in, wv_t, q_gain_row, k_gain_row, ln_gain, ln_bias, b_gate, q_aug):
    tm = TM_PROJ
    per_batch = SEQ // tm
    row = lambda n: pl.BlockSpec((1, n), lambda i: (0, 0))
    modrow = pl.BlockSpec((1, 1, D_MODEL), lambda i: (i // per_batch, 0, 0))
    tile = lambda n: pl.BlockSpec((tm, n), lambda i: (i, 0))
    out = lambda n: jax.ShapeDtypeStruct((TOKENS, n), BF16)
    return pl.pallas_call(
        _inproj_kernel,
        grid=(TOKENS // tm,),
        in_specs=[
            tile(D_MODEL), modrow, modrow, row(D_MODEL),
            pl.BlockSpec((D_MODEL, IN_WIDTH), lambda i: (0, 0)),
            pl.BlockSpec((KV_WIDTH, D_MODEL), lambda i: (0, 0)),
            row(N_HEADS * HEAD_SLOT), row(N_KV_HEADS * HEAD_SLOT), row(GMLP_WIDTH), row(GMLP_WIDTH),
            row(2 * D_MODEL), row(N_HEADS * HEAD_SLOT),
        ],
        out_specs=[tile(N_HEADS * HEAD_SLOT), tile(N_KV_HEADS * HEAD_SLOT),
                   pl.BlockSpec((KV_WIDTH, tm), lambda i: (0, i)),
                   tile(GMLP_WIDTH), tile(GMLP_WIDTH), tile(2 * D_MODEL)],
        out_shape=[out(N_HEADS * HEAD_SLOT), out(N_KV_HEADS * HEAD_SLOT),
                   jax.ShapeDtypeStruct((KV_WIDTH, TOKENS), BF16),
                   out(GMLP_WIDTH), out(GMLP_WIDTH), out(2 * D_MODEL)],
        compiler_params=_params(1),
        name="inproj",
    )(x2, scale1, shift1, norm_gain, w_in, wv_t, q_gain_row, k_gain_row, ln_gain, ln_bias, b_gate, q_aug)


def _attn_kernel(sink_ref, q_ref, kc_ref, kp_ref, vc_ref, vp_ref, o_ref):
    i = pl.program_id(1)
    slopes = _alibi_slopes()
    kj = lax.broadcasted_iota(I32, (BLOCK, 2 * BLOCK), 0)
    qi = lax.broadcasted_iota(I32, (BLOCK, 2 * BLOCK), 1) & (BLOCK - 1)
    own = kj <= qi
    from_own = jnp.where(own, 1.0, 0.0).astype(BF16)
    from_prev = jnp.where(own, 0.0, 1.0).astype(BF16)
    has_prev = i > 0
    low_half = lax.broadcasted_iota(I32, (1, HEAD_SLOT), 1) < HEAD_DIM
    k_lane = lax.broadcasted_iota(I32, (2 * BLOCK, HEAD_SLOT), 1)
    k_aug = jnp.where((k_lane >= HEAD_DIM) & (k_lane < HEAD_DIM + ALIBI_PARTS),
                      lax.broadcasted_iota(I32, (2 * BLOCK, HEAD_SLOT), 0), 0).astype(F32).astype(BF16)
    q_pos = (lax.broadcasted_iota(I32, (1, BLOCK), 1) + BLOCK).astype(F32)

    keys_of = {}

    def scores(item):
        sb, kv, pr = item
        r0, r1 = sb * BLOCK, (sb + 1) * BLOCK
        if (sb, kv) not in keys_of:
            cols = slice(kv * HEAD_SLOT, (kv + 1) * HEAD_SLOT)
            k_prev = kp_ref[:, cols] if sb == 0 else kc_ref[r0 - BLOCK:r0, cols]
            keys_of[sb, kv] = jnp.where(low_half, jnp.concatenate([k_prev, kc_ref[r0:r1, cols]], axis=0), k_aug)
        ha = kv * Q_PER_KV + 2 * pr
        queries = jnp.concatenate([q_ref[r0:r1, ha * HEAD_SLOT:(ha + 1) * HEAD_SLOT],
                                   q_ref[r0:r1, (ha + 1) * HEAD_SLOT:(ha + 2) * HEAD_SLOT]], axis=0)
        s = _dot_nt(keys_of[sb, kv], queries)
        s_prev = s[0:BLOCK, :]
        if sb == 0:
            s_prev = jnp.where(has_prev, s_prev, NEG_INF)
        return jnp.where(own, s[BLOCK:2 * BLOCK, :], s_prev)

    def attend(item, s):
        sb, kv, pr = item
        r0, r1 = sb * BLOCK, (sb + 1) * BLOCK
        rows = slice(kv * HEAD_DIM, (kv + 1) * HEAD_DIM)
        v_prev = vp_ref[rows, :] if sb == 0 else vc_ref[rows, r0 - BLOCK:r0]
        vt = jnp.concatenate([v_prev, vc_ref[rows, r0:r1]], axis=1)
        ha = kv * Q_PER_KV + 2 * pr
        hb = ha + 1
        sink = LOG2E * jnp.concatenate(
            [sink_ref[ha] + slopes[ha] * q_pos, sink_ref[hb] + slopes[hb] * q_pos], axis=1)
        m = jnp.maximum(jnp.max(s, axis=0, keepdims=True), sink)
        p = jnp.exp2(s - m)
        den = jnp.sum(p, axis=0, keepdims=True) + jnp.exp2(sink - m)
        pb = p.astype(BF16)
        p_keys = jnp.concatenate([pb * from_prev, pb * from_own], axis=0)
        o = _dot(vt, p_keys) / den
        pair = jnp.concatenate([o[:, 0:BLOCK], o[:, BLOCK:2 * BLOCK]], axis=0).T
        o_ref[r0:r1, ha * HEAD_DIM:(hb + 1) * HEAD_DIM] = pair.astype(BF16)

    items = [(sb, kv, pr) for sb in range(TM_ATTN // BLOCK) for kv in range(N_KV_HEADS)
             for pr in range(Q_PER_KV // 2)]
    pending = []
    for n in range(len(items) + SCORES_AHEAD):
        if n < len(items):
            pending.append(scores(items[n]))
        if n >= SCORES_AHEAD:
            attend(items[n - SCORES_AHEAD], pending.pop(0))


def _attn_call(sinks, q, k, vt):
    tq = TM_ATTN
    per_batch = SEQ // tq
    blocks_per_tile = tq // BLOCK
    q_width = N_HEADS * HEAD_SLOT
    k_width = N_KV_HEADS * HEAD_SLOT
    cur = lambda n: pl.BlockSpec((tq, n), lambda b, i: (b * per_batch + i, 0))
    prev_block = lambda b, i: b * (SEQ // BLOCK) + jnp.maximum(i * blocks_per_tile - 1, 0)
    return pl.pallas_call(
        _attn_kernel,
        grid=(BATCH, per_batch),
        in_specs=[
            pl.BlockSpec(memory_space=pltpu.SMEM),
            cur(q_width), cur(k_width),
            pl.BlockSpec((BLOCK, k_width), lambda b, i: (prev_block(b, i), 0)),
            pl.BlockSpec((KV_WIDTH, tq), lambda b, i: (0, b * per_batch + i)),
            pl.BlockSpec((KV_WIDTH, BLOCK), lambda b, i: (0, prev_block(b, i))),
        ],
        out_specs=cur(ATTN_WIDTH),
        out_shape=jax.ShapeDtypeStruct((TOKENS, ATTN_WIDTH), BF16),
        compiler_params=_params(2),
        name="attn",
    )(sinks, q, k, k, vt, vt)


def _first_argmax(vals):
    m = vals[0]
    for v in vals[1:]:
        m = jnp.maximum(m, v)
    idx = jnp.full(m.shape, len(vals) - 1, I32)
    for k in range(len(vals) - 2, -1, -1):
        idx = jnp.where(vals[k] == m, k, idx)
    return m, idx


def _mixer_kernel(x_ref, ya_ref, u_ref, vg_ref, gt_ref, ws_ref, bmap_ref, woa_ref, wog_ref, wout_ref, gate1_ref,
                  ng_ref, sc_ref, sh_ref, wrt_ref, brt_ref, x1_o, h2_o, bk_o):
    tm = TM_MIX
    t_idx = lax.broadcasted_iota(I32, (GMLP_CHUNK, GMLP_CHUNK), 0)
    s_idx = lax.broadcasted_iota(I32, (GMLP_CHUNK, GMLP_CHUNK), 1)
    ws = [jnp.where(t_idx >= s_idx, ws_ref[g], 0.0).astype(BF16) for g in range(GMLP_GROUPS)]
    gc = GMLP_WIDTH // GMLP_GROUPS
    chunks = []
    for c in range(tm // GMLP_CHUNK):
        r0, r1 = c * GMLP_CHUNK, (c + 1) * GMLP_CHUNK
        mixed = jnp.concatenate(
            [_dot(ws[g], vg_ref[r0:r1, g * gc:(g + 1) * gc]) for g in range(GMLP_GROUPS)], axis=1)
        mixed = mixed + bmap_ref[...]
        chunks.append((u_ref[r0:r1, :].astype(F32) * mixed).astype(BF16))
    y_gmlp = jnp.concatenate(chunks, axis=0)
    pa = _dot(ya_ref[...], woa_ref[...])
    pg = _dot(y_gmlp, wog_ref[...])
    merged = gt_ref[:, 0:D_MODEL].astype(F32) * pa + gt_ref[:, D_MODEL:2 * D_MODEL].astype(F32) * pg
    x1 = x_ref[...] + gate1_ref[0] * _dot(merged.astype(BF16), wout_ref[...])
    x1_o[...] = x1

    ms = jnp.mean(x1 * x1, axis=-1, keepdims=True)
    h2 = (x1 * lax.rsqrt(ms + EPS)) * ng_ref[...]
    h2 = h2 * (1.0 + sc_ref[0]) + sh_ref[0]
    _store_slabs(h2_o, h2)

    lg = _dot_nt(wrt_ref[...], h2.astype(BF16)) + brt_ref[...]
    _, gi = _first_argmax([lg[r:r + 1, :] for r in range(N_EXPERT_GROUPS)])
    el = [lg[EXPERT_ROW0 + r:EXPERT_ROW0 + r + 1, :] for r in range(N_EXPERTS)]
    eg = []
    for k in range(EXPERTS_PER_GROUP):
        v = el[(N_EXPERT_GROUPS - 1) * EXPERTS_PER_GROUP + k]
        for g in range(N_EXPERT_GROUPS - 2, -1, -1):
            v = jnp.where(gi == g, el[g * EXPERTS_PER_GROUP + k], v)
        eg.append(v)
    _, i1 = _first_argmax(eg)
    _, i2 = _first_argmax([jnp.where(i1 == k, -3e38, eg[k]) for k in range(EXPERTS_PER_GROUP)])
    lo = jnp.minimum(i1, i2)
    hi = jnp.maximum(i1, i2)
    pair = jnp.where(lo == 0, hi - 1, jnp.where(lo == 1, hi + 1, 5))
    bk_o[0] = gi * len(PAIRS) + pair


def _mixer_call(x2, y_attn, u, vg, gates, w_spatial, bias_map, wo_attn, wo_gmlp, w_out, gate1, norm_gain, scale2, shift2,
                wr_t, br_t):
    tm = TM_MIX
    per_batch = SEQ // tm
    nt = TOKENS // tm
    tile = lambda n: pl.BlockSpec((tm, n), lambda i: (i, 0))
    full = lambda shape: pl.BlockSpec(shape, lambda i: (0,) * len(shape))
    modrow = pl.BlockSpec((1, 1, D_MODEL), lambda i: (i // per_batch, 0, 0))
    return pl.pallas_call(
        _mixer_kernel,
        grid=(nt,),
        in_specs=[
            tile(D_MODEL), tile(ATTN_WIDTH), tile(GMLP_WIDTH), tile(GMLP_WIDTH), tile(2 * D_MODEL),
            full((GMLP_GROUPS, GMLP_CHUNK, GMLP_CHUNK)), full((GMLP_CHUNK, GMLP_WIDTH)),
            full((ATTN_WIDTH, D_MODEL)), full((GMLP_WIDTH, D_MODEL)), full((D_MODEL, D_MODEL)),
            modrow, full((1, D_MODEL)), modrow, modrow,
            full((BUCKET_ROWS, D_MODEL)), full((BUCKET_ROWS, 1)),
        ],
        out_specs=[tile(D_MODEL), pl.BlockSpec((tm * SLAB, 128), lambda i: (i, 0)),
                   pl.BlockSpec((1, 1, tm), lambda i: (i, 0, 0))],
        out_shape=[
            jax.ShapeDtypeStruct((TOKENS, D_MODEL), F32),
            jax.ShapeDtypeStruct((TOKENS * SLAB, 128), F32),
            jax.ShapeDtypeStruct((nt, 1, tm), I32),
        ],
        compiler_params=_params(1),
        name="mixer",
    )(x2, y_attn, u, vg, gates, w_spatial, bias_map, wo_attn, wo_gmlp, w_out, gate1, norm_gain, scale2, shift2,
      wr_t, br_t)


def _rank_kernel(bk_ref, pos_o, meta_o, cnt, off):
    tm = TM_MIX
    ph = pl.program_id(0)
    i = pl.program_id(1)
    bk = bk_ref[0]
    sub = lax.broadcasted_iota(I32, (BUCKET_ROWS, tm), 0)
    onehot = sub == bk

    @pl.when((ph == 0) & (i == 0))
    def _():
        cnt[...] = jnp.zeros_like(cnt)

    @pl.when(ph == 0)
    def _():
        cnt[...] += jnp.sum(jnp.where(onehot, 1.0, 0.0), axis=1, keepdims=True)

    @pl.when((ph == 1) & (i == 0))
    def _():
        tiles = jnp.floor((cnt[...] + (TM_MOE - 1)) * (1.0 / TM_MOE))
        r = lax.broadcasted_iota(I32, (BUCKET_ROWS, BUCKET_ROWS), 0)
        c = lax.broadcasted_iota(I32, (BUCKET_ROWS, BUCKET_ROWS), 1)
        before = jnp.where(c < r, 1.0, 0.0).astype(BF16)
        first_tile = _dot(before, tiles.astype(BF16))
        off[...] = first_tile * TM_MOE
        cnt[...] = jnp.zeros_like(cnt)
        end_tile = (first_tile + tiles)[:, 0:1]
        tj = lax.broadcasted_iota(I32, (BUCKET_ROWS, META_LANES), 1).astype(F32)
        bsub = lax.broadcasted_iota(I32, (BUCKET_ROWS, META_LANES), 0)
        passed = jnp.where((tj >= end_tile) & (bsub < N_BUCKETS), 1.0, 0.0)
        tile_bucket = jnp.sum(passed, axis=0, keepdims=True)
        n_tiles = jnp.sum(jnp.where(bsub == N_BUCKETS - 1, jnp.broadcast_to(end_tile, passed.shape), 0.0),
                          axis=0, keepdims=True)
        row = lax.broadcasted_iota(I32, (8, META_LANES), 0)
        meta = jnp.where(row == 0, jnp.broadcast_to(tile_bucket, (8, META_LANES)),
                         jnp.where(row == 1, jnp.broadcast_to(n_tiles, (8, META_LANES)), 0.0))
        meta_o[...] = meta.astype(I32)

    @pl.when(ph == 1)
    def _():
        s_idx = lax.broadcasted_iota(I32, (tm, tm), 0)
        t_idx = lax.broadcasted_iota(I32, (tm, tm), 1)
        upto = jnp.where(s_idx <= t_idx, 1.0, 0.0).astype(BF16)
        incl = _dot(jnp.where(onehot, 1.0, 0.0).astype(BF16), upto)
        base = off[:, 0:1] + cnt[:, 0:1]
        posf = jnp.sum(jnp.where(onehot, base + incl - 1.0, 0.0), axis=0, keepdims=True)
        pos_o[0] = posf.astype(I32)
        cnt[...] += incl[:, tm - 1:tm]


def _rank_call(buckets):
    nt, _, tm = buckets.shape
    return pl.pallas_call(
        _rank_kernel,
        grid=(2, nt),
        in_specs=[pl.BlockSpec((1, 1, tm), lambda ph, i: (i, 0, 0))],
        out_specs=[
            pl.BlockSpec((1, 1, tm), lambda ph, i: (ph * i, 0, 0)),
            pl.BlockSpec((8, META_LANES), lambda ph, i: (0, 0)),
        ],
        out_shape=[
            jax.ShapeDtypeStruct((nt, 1, tm), I32),
            jax.ShapeDtypeStruct((8, META_LANES), I32),
        ],
        scratch_shapes=[pltpu.VMEM((BUCKET_ROWS, 128), F32), pltpu.VMEM((BUCKET_ROWS, 128), F32)],
        compiler_params=_params(2),
        name="rank",
    )(buckets)


def _store_slabs(ref, x):
    n = x.shape[0]
    for s in range(SLAB):
        ref[pl.ds(s, n, stride=SLAB), :] = x[:, s * 128:(s + 1) * 128]


def _load_slabs(ref, n):
    return jnp.concatenate([ref[pl.ds(s, n, stride=SLAB), :] for s in range(SLAB)], axis=1)


def _row_copy(src, src_row, dst, dst_row, sem):
    return pltpu.make_async_copy(src.at[pl.ds(pl.multiple_of(src_row * SLAB, SLAB), SLAB)],
                                 dst.at[pl.ds(pl.multiple_of(dst_row * SLAB, SLAB), SLAB)], sem)


def _start_tile_rows(copy_of_row):
    def trip(c, carry):
        for k in range(ROW_UNROLL):
            copy_of_row(c * ROW_UNROLL + k).start(priority=k % 2)
        return carry
    lax.fori_loop(0, TM_ROW // ROW_UNROLL, trip, 0)


def _tile_wait(src, dst, sem):
    pltpu.make_async_copy(src.at[pl.ds(0, TM_ROW * SLAB)], dst, sem).wait()


def _scatter_kernel(pos_ref, h_ref, init_ref, o_ref, sem):
    del init_ref
    _start_tile_rows(lambda r: _row_copy(h_ref, r, o_ref, pos_ref[0, 0, r], sem))
    _tile_wait(h_ref, o_ref.at[pl.ds(0, TM_ROW * SLAB)], sem)


def _scatter_call(pos, h2):
    tm = TM_ROW
    init = jnp.zeros((SORTED_ROWS * SLAB, 128), F32)
    return pl.pallas_call(
        _scatter_kernel,
        grid=(TOKENS // tm,),
        in_specs=[
            pl.BlockSpec((1, 1, tm), lambda i: (i, 0, 0), memory_space=pltpu.SMEM),
            pl.BlockSpec((tm * SLAB, 128), lambda i: (i, 0)),
            pl.BlockSpec(memory_space=pl.ANY),
        ],
        out_specs=pl.BlockSpec(memory_space=pl.ANY),
        out_shape=jax.ShapeDtypeStruct((SORTED_ROWS * SLAB, 128), F32),
        scratch_shapes=[pltpu.SemaphoreType.DMA(())],
        input_output_aliases={2: 0},
        compiler_params=_params(1),
        name="scatter",
    )(pos, h2, init)


def _gather_kernel(pos_ref, pos_next_ref, ys_ref, x1_ref, gate2_ref, o_ref, buf, sem):
    i = pl.program_id(0)
    n = pl.num_programs(0)
    slot = lax.rem(i, 2)

    def fetch(p_ref, s):
        _start_tile_rows(lambda r: _row_copy(ys_ref, p_ref[0, 0, r], buf.at[s], r, sem.at[s]))

    @pl.when(i == 0)
    def _():
        fetch(pos_ref, 0)

    @pl.when(i + 1 < n)
    def _():
        fetch(pos_next_ref, 1 - slot)

    _tile_wait(ys_ref, buf.at[slot], sem.at[slot])
    o_ref[...] = x1_ref[...] + gate2_ref[0] * _load_slabs(buf.at[slot], TM_ROW)


def _gather_call(pos, y_sorted, x1, gate2):
    tm = TM_ROW
    per_batch = SEQ // tm
    n = TOKENS // tm
    return pl.pallas_call(
        _gather_kernel,
        grid=(n,),
        in_specs=[
            pl.BlockSpec((1, 1, tm), lambda i: (i, 0, 0), memory_space=pltpu.SMEM),
            pl.BlockSpec((1, 1, tm), lambda i: (jnp.minimum(i + 1, n - 1), 0, 0), memory_space=pltpu.SMEM),
            pl.BlockSpec(memory_space=pl.ANY),
            pl.BlockSpec((tm, D_MODEL), lambda i: (i, 0)),
            pl.BlockSpec((1, 1, D_MODEL), lambda i: (i // per_batch, 0, 0)),
        ],
        out_specs=pl.BlockSpec((tm, D_MODEL), lambda i: (i, 0)),
        out_shape=jax.ShapeDtypeStruct((TOKENS, D_MODEL), F32),
        scratch_shapes=[pltpu.VMEM((2, tm * SLAB, 128), F32), pltpu.SemaphoreType.DMA((2,))],
        compiler_params=_params(1),
        name="gather",
    )(pos, pos, y_sorted, x1, gate2)


def _moe_kernel(tb_ref, nt_ref, tab_ref, h_ref, wr_ref, brow_ref, wg_lo, wu_lo, wd_lo, wg_hi, wu_hi, wd_hi, o_ref):
    j = pl.program_id(0)

    @pl.when(j < nt_ref[0])
    def _():
        b = tb_ref[j]
        e_lo = tab_ref[b]
        e_hi = tab_ref[N_BUCKETS + b]
        g = tab_ref[2 * N_BUCKETS + b]
        h = _load_slabs(h_ref, TM_MOE).astype(BF16)
        lg = _dot(h, wr_ref[...]) + brow_ref[...]
        lane = lax.broadcasted_iota(I32, lg.shape, 1)
        is_group = lane < N_EXPERT_GROUPS
        gmax = jnp.max(jnp.where(is_group, lg, NEG_INF), axis=1, keepdims=True)
        ge = jnp.exp(lg - gmax)
        g_w = (jnp.sum(jnp.where(lane == g, ge, 0.0), axis=1, keepdims=True)
               / jnp.sum(jnp.where(is_group, ge, 0.0), axis=1, keepdims=True))
        v_lo = jnp.sum(jnp.where(lane == EXPERT_ROW0 + e_lo, lg, 0.0), axis=1, keepdims=True)
        v_hi = jnp.sum(jnp.where(lane == EXPERT_ROW0 + e_hi, lg, 0.0), axis=1, keepdims=True)
        m = jnp.maximum(v_lo, v_hi)
        x_lo = jnp.exp(v_lo - m)
        x_hi = jnp.exp(v_hi - m)
        w_lo = x_lo / (x_lo + x_hi) * g_w
        w_hi = x_hi / (x_lo + x_hi) * g_w

        def hidden(wg, wu, w):
            a = _dot(h, wg[0])
            return (a * _sigmoid(a) * _dot(h, wu[0]) * w).astype(BF16)

        y = _dot(hidden(wg_lo, wu_lo, w_lo), wd_lo[0]) + _dot(hidden(wg_hi, wu_hi, w_hi), wd_hi[0])
        _store_slabs(o_ref, y)

    @pl.when(j >= nt_ref[0])
    def _():
        o_ref[...] = jnp.zeros_like(o_ref)


def _moe_call(tile_bucket, n_tiles, tables, h_sorted, wr, br_row, w_gate, w_up, w_down):
    tm = TM_MOE

    def last(j, nt):
        return jnp.minimum(j, nt[0] - 1)

    def w_spec(shape, which):
        return pl.BlockSpec(
            (1,) + shape, lambda j, tb, nt, tab: (tab[which * N_BUCKETS + tb[last(j, nt)]], 0, 0))

    up = (D_MODEL, D_EXPERT)
    down = (D_EXPERT, D_MODEL)
    grid_spec = pltpu.PrefetchScalarGridSpec(
        num_scalar_prefetch=3,
        grid=(NT_MOE,),
        in_specs=[
            pl.BlockSpec((tm * SLAB, 128), lambda j, tb, nt, tab: (last(j, nt), 0)),
            pl.BlockSpec((D_MODEL, 128), lambda j, tb, nt, tab: (0, 0)),
            pl.BlockSpec((1, 128), lambda j, tb, nt, tab: (0, 0)),
            w_spec(up, 0), w_spec(up, 0), w_spec(down, 0),
            w_spec(up, 1), w_spec(up, 1), w_spec(down, 1),
        ],
        out_specs=pl.BlockSpec((tm * SLAB, 128), lambda j, tb, nt, tab: (j, 0)),
    )
    return pl.pallas_call(
        _moe_kernel,
        grid_spec=grid_spec,
        out_shape=jax.ShapeDtypeStruct((SORTED_ROWS * SLAB, 128), F32),
        compiler_params=_params(1),
        name="moe",
    )(tile_bucket, n_tiles, tables, h_sorted, wr, br_row, w_gate, w_up, w_down, w_gate, w_up, w_down)


def _alibi_query_lanes():
    lanes = np.zeros((1, N_HEADS * HEAD_SLOT), np.float32)
    for hd, slope in enumerate(_alibi_slopes()):
        rest = np.float32(slope * LOG2E)
        for part in range(ALIBI_PARTS):
            piece = np.float32(np.asarray(rest).astype(jnp.bfloat16))
            lanes[0, hd * HEAD_SLOT + HEAD_DIM + part] = piece
            rest = np.float32(rest - piece)
    return jnp.asarray(lanes)


def _bucket_tables():
    e_lo, e_hi, grp = [], [], []
    for g in range(N_EXPERT_GROUPS):
        for lo, hi in PAIRS:
            e_lo.append(g * EXPERTS_PER_GROUP + lo)
            e_hi.append(g * EXPERTS_PER_GROUP + hi)
            grp.append(g)
    return jnp.asarray(e_lo + e_hi + grp, I32)


def kernel(x, c, w_ada, b_ada, norm1_gain, w_in, b_branch_gate, q_norm_gain, k_norm_gain, attn_sinks, gmlp_norm_gain, gmlp_norm_bias, gmlp_w_spatial, gmlp_b_spatial, w_o_attn, w_o_gmlp, w_out, norm2_gain, w_group_router, b_group_router, w_expert_router, b_expert_router, w_expert_gate, w_expert_up, w_expert_down):
    depth = w_ada.shape[0]
    x2 = x.reshape(TOKENS, D_MODEL)
    tables = _bucket_tables()
    q_aug = _alibi_query_lanes()
    row = lambda v: v.reshape(1, -1)
    for l in range(depth):
        mod = _mod_call(c, w_ada[l], b_ada[l])
        shift1, scale1, gate1, shift2, scale2, gate2 = [
            m.reshape(BATCH, 1, D_MODEL) for m in jnp.split(mod, 6, axis=-1)]

        slot_pad = jnp.zeros((HEAD_SLOT - HEAD_DIM,), F32)
        q_gain_row = row(jnp.tile(jnp.concatenate([q_norm_gain[l] * (HEAD_DIM ** -0.5 * LOG2E), slot_pad]), N_HEADS))
        k_gain_row = row(jnp.tile(jnp.concatenate([k_norm_gain[l], slot_pad]), N_KV_HEADS))
        q, k, vt, u, vg, gates = _inproj_call(
            x2, scale1, shift1, row(norm1_gain[l]), w_in[l].astype(BF16), w_in[l][:, V0:V1].T.astype(BF16),
            q_gain_row, k_gain_row, row(gmlp_norm_gain[l]), row(gmlp_norm_bias[l]), row(b_branch_gate[l]), q_aug)

        y_attn = _attn_call(attn_sinks[l], q, k, vt)

        bias_map = jnp.repeat(gmlp_b_spatial[l].T, GMLP_WIDTH // GMLP_GROUPS, axis=1)
        e0, e1 = EXPERT_ROW0, EXPERT_ROW0 + N_EXPERTS
        wr = jnp.zeros((D_MODEL, 128), F32)
        wr = wr.at[:, 0:N_EXPERT_GROUPS].set(w_group_router[l]).at[:, e0:e1].set(w_expert_router[l]).astype(BF16)
        br_row = jnp.zeros((1, 128), F32)
        br_row = br_row.at[0, 0:N_EXPERT_GROUPS].set(b_group_router[l]).at[0, e0:e1].set(b_expert_router[l])
        x1, h2p, buckets = _mixer_call(
            x2, y_attn, u, vg, gates, gmlp_w_spatial[l], bias_map, w_o_attn[l].astype(BF16),
            w_o_gmlp[l].astype(BF16), w_out[l].astype(BF16), gate1, row(norm2_gain[l]), scale2, shift2,
            wr[:, 0:BUCKET_ROWS].T, br_row[:, 0:BUCKET_ROWS].T)

        pos, meta = _rank_call(buckets)
        h_sorted = _scatter_call(pos, h2p)

        flat = lambda w: w.reshape((N_EXPERTS,) + w.shape[2:]).astype(BF16)
        y_sorted = _moe_call(
            meta[0], meta[1, 0:1], tables, h_sorted, wr, br_row,
            flat(w_expert_gate[l]), flat(w_expert_up[l]), flat(w_expert_down[l]))

        x2 = _gather_call(pos, y_sorted, x1, gate2)
    return x2.reshape(x.shape)
```

```python
import functools

import jax
import jax.numpy as jnp
import numpy as np
from jax import lax
from jax.experimental import pallas as pl
from jax.experimental.pallas import tpu as pltpu

D_MODEL = 1024
BATCH = 8
SEQ = 4096
TOKENS = BATCH * SEQ
N_HEADS = 16
N_KV_HEADS = 4
HEAD_DIM = 64
Q_PER_KV = N_HEADS // N_KV_HEADS
BLOCK = 128
HEAD_SLOT = 128
ALIBI_PARTS = 3
ATTN_WIDTH = N_HEADS * HEAD_DIM
KV_WIDTH = N_KV_HEADS * HEAD_DIM
GMLP_WIDTH = 1024
GMLP_GROUPS = 8
GMLP_CHUNK = 128
N_EXPERT_GROUPS = 4
EXPERTS_PER_GROUP = 4
N_EXPERTS = N_EXPERT_GROUPS * EXPERTS_PER_GROUP
D_EXPERT = 512
IN_WIDTH = ATTN_WIDTH + 2 * KV_WIDTH + 2 * GMLP_WIDTH + 2 * D_MODEL
EPS = 1e-6
NEG_INF = -1e30
LOG2E = float(np.float32(np.log2(np.e)))

Q0, Q1 = 0, ATTN_WIDTH
K0, K1 = Q1, Q1 + KV_WIDTH
V0, V1 = K1, K1 + KV_WIDTH
U0, U1 = V1, V1 + GMLP_WIDTH
G0, G1 = U1, U1 + GMLP_WIDTH
B0, B1 = G1, G1 + 2 * D_MODEL

PAIRS = ((0, 1), (0, 2), (0, 3), (1, 2), (1, 3), (2, 3))
N_BUCKETS = N_EXPERT_GROUPS * len(PAIRS)
BUCKET_ROWS = 32
EXPERT_ROW0 = 8

TM_PROJ = 512
TM_ATTN = 512
TM_MIX = 512
TM_MOE = 256
TM_ROW = 512
SCORES_AHEAD = 6
ROW_UNROLL = 16
NT_MOE = TOKENS // TM_MOE + N_BUCKETS
SORTED_ROWS = NT_MOE * TM_MOE
SLAB = D_MODEL // 128
META_LANES = 256
assert NT_MOE <= META_LANES

VMEM_LIMIT = 56 * 1024 * 1024

F32 = jnp.float32
BF16 = jnp.bfloat16
U32 = jnp.uint32
I32 = jnp.int32


def _alibi_slopes():
    return [float(np.float32(2.0 ** (-8.0 * (i + 1) / N_HEADS))) for i in range(N_HEADS)]


def _params(n_axes):
    return pltpu.CompilerParams(dimension_semantics=("arbitrary",) * n_axes, vmem_limit_bytes=VMEM_LIMIT)


def _sigmoid(x):
    return 1.0 / (1.0 + jnp.exp(-x))


def _gelu(x):
    return 0.5 * x * (1.0 + lax.erf(x * np.float32(1.0 / np.sqrt(2.0))))


def _dot(a, b):
    return jnp.dot(a, b, preferred_element_type=F32)


def _dot_nt(a, b):
    return lax.dot_general(a, b, (((1,), (1,)), ((), ())), preferred_element_type=F32)


def _mod_kernel(c_ref, w_ref, b_ref, o_ref):
    c = c_ref[...]
    ca = c * _sigmoid(c)
    o_ref[...] = _dot(ca.astype(BF16), w_ref[...].astype(BF16)) + b_ref[...]


def _mod_call(c, w_ada, b_ada):
    n = w_ada.shape[1]
    bn = 1536
    return pl.pallas_call(
        _mod_kernel,
        grid=(n // bn,),
        in_specs=[
            pl.BlockSpec((BATCH, D_MODEL), lambda j: (0, 0)),
            pl.BlockSpec((D_MODEL, bn), lambda j: (0, j)),
            pl.BlockSpec((1, bn), lambda j: (0, j)),
        ],
        out_specs=pl.BlockSpec((BATCH, bn), lambda j: (0, j)),
        out_shape=jax.ShapeDtypeStruct((BATCH, n), F32),
        compiler_params=_params(1),
        name="mod",
    )(c, w_ada, b_ada.reshape(1, n))


def _inproj_kernel(x_ref, sc_ref, sh_ref, ng_ref, w_ref, wvt_ref, qg_ref, kg_ref, lng_ref, lnb_ref, bg_ref, qaug_ref,
                   q_o, k_o, vt_o, u_o, vg_o, gt_o):
    x = x_ref[...]
    ms = jnp.mean(x * x, axis=-1, keepdims=True)
    h = (x * lax.rsqrt(ms + EPS)) * ng_ref[...]
    h = h * (1.0 + sc_ref[0]) + sh_ref[0]
    hb = h.astype(BF16)

    def proj(c0, c1):
        return _dot(hb, w_ref[:, c0:c1])

    low_half = lax.broadcasted_iota(I32, (1, HEAD_SLOT), 1) < HEAD_DIM

    def store_normed_heads(raw, gain_ref, spare_ref, o_ref):
        for p in range(raw.shape[1] // HEAD_SLOT):
            pair = raw[:, p * HEAD_SLOT:(p + 1) * HEAD_SLOT]
            for hd, head in ((2 * p, pair), (2 * p + 1, pltpu.roll(pair, HEAD_DIM, 1))):
                cols = slice(hd * HEAD_SLOT, (hd + 1) * HEAD_SLOT)
                xh = jnp.where(low_half, head, 0.0)
                r = lax.rsqrt(jnp.sum(xh * xh, axis=-1, keepdims=True) * (1.0 / HEAD_DIM) + EPS)
                y = xh * r * gain_ref[:, cols]
                if spare_ref is not None:
                    y = y + spare_ref[:, cols]
                o_ref[:, cols] = y.astype(BF16)

    store_normed_heads(proj(Q0, Q1), qg_ref, qaug_ref, q_o)
    store_normed_heads(proj(K0, K1), kg_ref, None, k_o)
    vt_o[...] = _dot_nt(wvt_ref[...], hb).astype(BF16)
    u_o[...] = _gelu(proj(U0, U1)).astype(BF16)
    vg = _gelu(proj(G0, G1))
    mu = jnp.mean(vg, axis=-1, keepdims=True)
    vc = vg - mu
    var = jnp.mean(vc * vc, axis=-1, keepdims=True)
    vg_o[...] = (vc * lax.rsqrt(var + EPS) * lng_ref[...] + lnb_ref[...]).astype(BF16)
    gt_o[...] = _sigmoid(proj(B0, B1) + bg_ref[...]).astype(BF16)


def _inproj_call(x2, scale1, shift1, norm_gain, w_in, wv_t, q_gain_row, k_gain_row, ln_gain, ln_bias, b_gate, q_aug):
    tm = TM_PROJ
    per_batch = SEQ // tm
    row = lambda n: pl.BlockSpec((1, n), lambda i: (0, 0))
    modrow = pl.BlockSpec((1, 1, D_MODEL), lambda i: (i // per_batch, 0, 0))
    tile = lambda n: pl.BlockSpec((tm, n), lambda i: (i, 0))
    out = lambda n: jax.ShapeDtypeStruct((TOKENS, n), BF16)
    return pl.pallas_call(
        _inproj_kernel,
        grid=(TOKENS // tm,),
        in_specs=[
            tile(D_MODEL), modrow, modrow, row(D_MODEL),
            pl.BlockSpec((D_MODEL, IN_WIDTH), lambda i: (0, 0)),
            pl.BlockSpec((KV_WIDTH, D_MODEL), lambda i: (0, 0)),
            row(N_HEADS * HEAD_SLOT), row(N_KV_HEADS * HEAD_SLOT), row(GMLP_WIDTH), row(GMLP_WIDTH),
            row(2 * D_MODEL), row(N_HEADS * HEAD_SLOT),
        ],
        out_specs=[tile(N_HEADS * HEAD_SLOT), tile(N_KV_HEADS * HEAD_SLOT),
                   pl.BlockSpec((KV_WIDTH, tm), lambda i: (0, i)),
                   tile(GMLP_WIDTH), tile(GMLP_WIDTH), tile(2 * D_MODEL)],
        out_shape=[out(N_HEADS * HEAD_SLOT), out(N_KV_HEADS * HEAD_SLOT),
                   jax.ShapeDtypeStruct((KV_WIDTH, TOKENS), BF16),
                   out(GMLP_WIDTH), out(GMLP_WIDTH), out(2 * D_MODEL)],
        compiler_params=_params(1),
        name="inproj",
    )(x2, scale1, shift1, norm_gain, w_in, wv_t, q_gain_row, k_gain_row, ln_gain, ln_bias, b_gate, q_aug)


def _attn_kernel(sink_ref, q_ref, kc_ref, kp_ref, vc_ref, vp_ref, o_ref):
    i = pl.program_id(1)
    slopes = _alibi_slopes()
    kj = lax.broadcasted_iota(I32, (BLOCK, 2 * BLOCK), 0)
    qi = lax.broadcasted_iota(I32, (BLOCK, 2 * BLOCK), 1) & (BLOCK - 1)
    own = kj <= qi
    from_own = jnp.where(own, 1.0, 0.0).astype(BF16)
    from_prev = jnp.where(own, 0.0, 1.0).astype(BF16)
    has_prev = i > 0
    low_half = lax.broadcasted_iota(I32, (1, HEAD_SLOT), 1) < HEAD_DIM
    k_lane = lax.broadcasted_iota(I32, (2 * BLOCK, HEAD_SLOT), 1)
    k_aug = jnp.where((k_lane >= HEAD_DIM) & (k_lane < HEAD_DIM + ALIBI_PARTS),
                      lax.broadcasted_iota(I32, (2 * BLOCK, HEAD_SLOT), 0), 0).astype(F32).astype(BF16)
    q_pos = (lax.broadcasted_iota(I32, (1, BLOCK), 1) + BLOCK).astype(F32)

    keys_of = {}

    def scores(item):
        sb, kv, pr = item
        r0, r1 = sb * BLOCK, (sb + 1) * BLOCK
        if (sb, kv) not in keys_of:
            cols = slice(kv * HEAD_SLOT, (kv + 1) * HEAD_SLOT)
            k_prev = kp_ref[:, cols] if sb == 0 else kc_ref[r0 - BLOCK:r0, cols]
            keys_of[sb, kv] = jnp.where(low_half, jnp.concatenate([k_prev, kc_ref[r0:r1, cols]], axis=0), k_aug)
        ha = kv * Q_PER_KV + 2 * pr
        queries = jnp.concatenate([q_ref[r0:r1, ha * HEAD_SLOT:(ha + 1) * HEAD_SLOT],
                                   q_ref[r0:r1, (ha + 1) * HEAD_SLOT:(ha + 2) * HEAD_SLOT]], axis=0)
        s = _dot_nt(keys_of[sb, kv], queries)
        s_prev = s[0:BLOCK, :]
        if sb == 0:
            s_prev = jnp.where(has_prev, s_prev, NEG_INF)
        return jnp.where(own, s[BLOCK:2 * BLOCK, :], s_prev)

    def attend(item, s):
        sb, kv, pr = item
        r0, r1 = sb * BLOCK, (sb + 1) * BLOCK
        rows = slice(kv * HEAD_DIM, (kv + 1) * HEAD_DIM)
        v_prev = vp_ref[rows, :] if sb == 0 else vc_ref[rows, r0 - BLOCK:r0]
        vt = jnp.concatenate([v_prev, vc_ref[rows, r0:r1]], axis=1)
        ha = kv * Q_PER_KV + 2 * pr
        hb = ha + 1
        sink = LOG2E * jnp.concatenate(
            [sink_ref[ha] + slopes[ha] * q_pos, sink_ref[hb] + slopes[hb] * q_pos], axis=1)
        m = jnp.maximum(jnp.max(s, axis=0, keepdims=True), sink)
        p = jnp.exp2(s - m)
        den = jnp.sum(p, axis=0, keepdims=True) + jnp.exp2(sink - m)
        pb = p.astype(BF16)
        p_keys = jnp.concatenate([pb * from_prev, pb * from_own], axis=0)
        o = _dot(vt, p_keys) / den
        pair = jnp.concatenate([o[:, 0:BLOCK], o[:, BLOCK:2 * BLOCK]], axis=0).T
        o_ref[r0:r1, ha * HEAD_DIM:(hb + 1) * HEAD_DIM] = pair.astype(BF16)

    items = [(sb, kv, pr) for sb in range(TM_ATTN // BLOCK) for kv in range(N_KV_HEADS)
             for pr in range(Q_PER_KV // 2)]
    pending = []
    for n in range(len(items) + SCORES_AHEAD):
        if n < len(items):
            pending.append(scores(items[n]))
        if n >= SCORES_AHEAD:
            attend(items[n - SCORES_AHEAD], pending.pop(0))


def _attn_call(sinks, q, k, vt):
    tq = TM_ATTN
    per_batch = SEQ // tq
    blocks_per_tile = tq // BLOCK
    q_width = N_HEADS * HEAD_SLOT
    k_width = N_KV_HEADS * HEAD_SLOT
    cur = lambda n: pl.BlockSpec((tq, n), lambda b, i: (b * per_batch + i, 0))
    prev_block = lambda b, i: b * (SEQ // BLOCK) + jnp.maximum(i * blocks_per_tile - 1, 0)
    return pl.pallas_call(
        _attn_kernel,
        grid=(BATCH, per_batch),
        in_specs=[
            pl.BlockSpec(memory_space=pltpu.SMEM),
            cur(q_width), cur(k_width),
            pl.BlockSpec((BLOCK, k_width), lambda b, i: (prev_block(b, i), 0)),
            pl.BlockSpec((KV_WIDTH, tq), lambda b, i: (0, b * per_batch + i)),
            pl.BlockSpec((KV_WIDTH, BLOCK), lambda b, i: (0, prev_block(b, i))),
        ],
        out_specs=cur(ATTN_WIDTH),
        out_shape=jax.ShapeDtypeStruct((TOKENS, ATTN_WIDTH), BF16),
        compiler_params=_params(2),
        name="attn",
    )(sinks, q, k, k, vt, vt)


def _first_argmax(vals):
    m = vals[0]
    for v in vals[1:]:
        m = jnp.maximum(m, v)
    idx = jnp.full(m.shape, len(vals) - 1, I32)
    for k in range(len(vals) - 2, -1, -1):
        idx = jnp.where(vals[k] == m, k, idx)
    return m, idx


def _mixer_kernel(x_ref, ya_ref, u_ref, vg_ref, gt_ref, ws_ref, bmap_ref, woa_ref, wog_ref, wout_ref, gate1_ref,
                  ng_ref, sc_ref, sh_ref, wrt_ref, brt_ref, x1_o, h2_o, bk_o, cnt_o, cnt):
    tm = TM_MIX
    t_idx = lax.broadcasted_iota(I32, (GMLP_CHUNK, GMLP_CHUNK), 0)
    s_idx = lax.broadcasted_iota(I32, (GMLP_CHUNK, GMLP_CHUNK), 1)
    ws = [jnp.where(t_idx >= s_idx, ws_ref[g], 0.0).astype(BF16) for g in range(GMLP_GROUPS)]
    gc = GMLP_WIDTH // GMLP_GROUPS
    chunks = []
    for c in range(tm // GMLP_CHUNK):
        r0, r1 = c * GMLP_CHUNK, (c + 1) * GMLP_CHUNK
        mixed = jnp.concatenate(
            [_dot(ws[g], vg_ref[r0:r1, g * gc:(g + 1) * gc]) for g in range(GMLP_GROUPS)], axis=1)
        mixed = mixed + bmap_ref[...]
        chunks.append((u_ref[r0:r1, :].astype(F32) * mixed).astype(BF16))
    y_gmlp = jnp.concatenate(chunks, axis=0)
    pa = _dot(ya_ref[...], woa_ref[...])
    pg = _dot(y_gmlp, wog_ref[...])
    merged = gt_ref[:, 0:D_MODEL].astype(F32) * pa + gt_ref[:, D_MODEL:2 * D_MODEL].astype(F32) * pg
    x1 = x_ref[...] + gate1_ref[0] * _dot(merged.astype(BF16), wout_ref[...])
    x1_o[...] = x1

    ms = jnp.mean(x1 * x1, axis=-1, keepdims=True)
    h2 = (x1 * lax.rsqrt(ms + EPS)) * ng_ref[...]
    h2 = h2 * (1.0 + sc_ref[0]) + sh_ref[0]
    _store_slabs(h2_o, h2)

    lg = _dot_nt(wrt_ref[...], h2.astype(BF16)) + brt_ref[...]
    _, gi = _first_argmax([lg[r:r + 1, :] for r in range(N_EXPERT_GROUPS)])
    el = [lg[EXPERT_ROW0 + r:EXPERT_ROW0 + r + 1, :] for r in range(N_EXPERTS)]
    eg = []
    for k in range(EXPERTS_PER_GROUP):
        v = el[(N_EXPERT_GROUPS - 1) * EXPERTS_PER_GROUP + k]
        for g in range(N_EXPERT_GROUPS - 2, -1, -1):
            v = jnp.where(gi == g, el[g * EXPERTS_PER_GROUP + k], v)
        eg.append(v)
    _, i1 = _first_argmax(eg)
    _, i2 = _first_argmax([jnp.where(i1 == k, -3e38, eg[k]) for k in range(EXPERTS_PER_GROUP)])
    lo = jnp.minimum(i1, i2)
    hi = jnp.maximum(i1, i2)
    pair = jnp.where(lo == 0, hi - 1, jnp.where(lo == 1, hi + 1, 5))
    bucket = gi * len(PAIRS) + pair
    bk_o[0] = bucket

    @pl.when(pl.program_id(0) == 0)
    def _():
        cnt[...] = jnp.zeros_like(cnt)
    in_bucket = lax.broadcasted_iota(I32, (BUCKET_ROWS, tm), 0) == bucket
    cnt[...] += jnp.sum(jnp.where(in_bucket, 1.0, 0.0), axis=1, keepdims=True)
    cnt_o[...] = cnt[...]


def _mixer_call(x2, y_attn, u, vg, gates, w_spatial, bias_map, wo_attn, wo_gmlp, w_out, gate1, norm_gain, scale2, shift2,
                wr_t, br_t):
    tm = TM_MIX
    per_batch = SEQ // tm
    nt = TOKENS // tm
    tile = lambda n: pl.BlockSpec((tm, n), lambda i: (i, 0))
    full = lambda shape: pl.BlockSpec(shape, lambda i: (0,) * len(shape))
    modrow = pl.BlockSpec((1, 1, D_MODEL), lambda i: (i // per_batch, 0, 0))
    return pl.pallas_call(
        _mixer_kernel,
        grid=(nt,),
        in_specs=[
            tile(D_MODEL), tile(ATTN_WIDTH), tile(GMLP_WIDTH), tile(GMLP_WIDTH), tile(2 * D_MODEL),
            full((GMLP_GROUPS, GMLP_CHUNK, GMLP_CHUNK)), full((GMLP_CHUNK, GMLP_WIDTH)),
            full((ATTN_WIDTH, D_MODEL)), full((GMLP_WIDTH, D_MODEL)), full((D_MODEL, D_MODEL)),
            modrow, full((1, D_MODEL)), modrow, modrow,
            full((BUCKET_ROWS, D_MODEL)), full((BUCKET_ROWS, 1)),
        ],
        out_specs=[tile(D_MODEL), pl.BlockSpec((tm * SLAB, 128), lambda i: (i, 0)),
                   pl.BlockSpec((1, 1, tm), lambda i: (i, 0, 0)), full((BUCKET_ROWS, 128))],
        out_shape=[
            jax.ShapeDtypeStruct((TOKENS, D_MODEL), F32),
            jax.ShapeDtypeStruct((TOKENS * SLAB, 128), F32),
            jax.ShapeDtypeStruct((nt, 1, tm), I32),
            jax.ShapeDtypeStruct((BUCKET_ROWS, 128), F32),
        ],
        scratch_shapes=[pltpu.VMEM((BUCKET_ROWS, 128), F32)],
        compiler_params=_params(1),
        name="mixer",
    )(x2, y_attn, u, vg, gates, w_spatial, bias_map, wo_attn, wo_gmlp, w_out, gate1, norm_gain, scale2, shift2,
      wr_t, br_t)


META_TILE_BUCKET, META_N_TILES, META_PAD_START, META_PAD_END = 0, 1, 2, 3


def _rank_kernel(bk_ref, total_ref, pos_o, meta_o, seen, off):
    tm = TM_MIX
    i = pl.program_id(0)
    bk = bk_ref[0]
    onehot = lax.broadcasted_iota(I32, (BUCKET_ROWS, tm), 0) == bk

    @pl.when(i == 0)
    def _():
        total = total_ref[...]
        tiles = jnp.floor((total + (TM_MOE - 1)) * (1.0 / TM_MOE))
        r = lax.broadcasted_iota(I32, (BUCKET_ROWS, BUCKET_ROWS), 0)
        c = lax.broadcasted_iota(I32, (BUCKET_ROWS, BUCKET_ROWS), 1)
        before = jnp.where(c < r, 1.0, 0.0).astype(BF16)
        first_tile = _dot(before, tiles.astype(BF16))
        first_row = first_tile * TM_MOE
        off[...] = first_row
        seen[...] = jnp.zeros_like(seen)
        end_tile = (first_tile + tiles)[:, 0:1]
        lane = lax.broadcasted_iota(I32, (BUCKET_ROWS, META_LANES), 1)
        bsub = lax.broadcasted_iota(I32, (BUCKET_ROWS, META_LANES), 0)
        is_bucket = bsub < N_BUCKETS
        tile_bucket = jnp.sum(jnp.where((lane.astype(F32) >= end_tile) & is_bucket, 1.0, 0.0), axis=0, keepdims=True)
        as_row = lambda col: jnp.sum(jnp.where((bsub == lane) & is_bucket, col, 0.0), axis=0, keepdims=True)
        n_tiles = jnp.sum(jnp.where(bsub == N_BUCKETS - 1, end_tile, 0.0), axis=0, keepdims=True)
        pad_start = as_row((first_row + total)[:, 0:1])
        pad_end = as_row(end_tile * TM_MOE)
        row = lax.broadcasted_iota(I32, (8, META_LANES), 0)
        meta = jnp.zeros((8, META_LANES), F32)
        for k, v in ((META_TILE_BUCKET, tile_bucket), (META_N_TILES, n_tiles), (META_PAD_START, pad_start),
                     (META_PAD_END, pad_end)):
            meta = jnp.where(row == k, v, meta)
        meta_o[...] = meta.astype(I32)

    s_idx = lax.broadcasted_iota(I32, (tm, tm), 0)
    t_idx = lax.broadcasted_iota(I32, (tm, tm), 1)
    upto = jnp.where(s_idx <= t_idx, 1.0, 0.0).astype(BF16)
    incl = _dot(jnp.where(onehot, 1.0, 0.0).astype(BF16), upto)
    base = off[:, 0:1] + seen[:, 0:1]
    posf = jnp.sum(jnp.where(onehot, base + incl - 1.0, 0.0), axis=0, keepdims=True)
    pos_o[0] = posf.astype(I32)
    seen[...] += incl[:, tm - 1:tm]


def _rank_call(buckets, totals):
    nt, _, tm = buckets.shape
    return pl.pallas_call(
        _rank_kernel,
        grid=(nt,),
        in_specs=[pl.BlockSpec((1, 1, tm), lambda i: (i, 0, 0)),
                  pl.BlockSpec((BUCKET_ROWS, 128), lambda i: (0, 0))],
        out_specs=[
            pl.BlockSpec((1, 1, tm), lambda i: (i, 0, 0)),
            pl.BlockSpec((8, META_LANES), lambda i: (0, 0)),
        ],
        out_shape=[
            jax.ShapeDtypeStruct((nt, 1, tm), I32),
            jax.ShapeDtypeStruct((8, META_LANES), I32),
        ],
        scratch_shapes=[pltpu.VMEM((BUCKET_ROWS, 128), F32), pltpu.VMEM((BUCKET_ROWS, 128), F32)],
        compiler_params=_params(1),
        name="rank",
    )(buckets, totals)


def _store_slabs(ref, x):
    n = x.shape[0]
    for s in range(SLAB):
        ref[pl.ds(s, n, stride=SLAB), :] = x[:, s * 128:(s + 1) * 128]


def _load_slabs(ref, n):
    return jnp.concatenate([ref[pl.ds(s, n, stride=SLAB), :] for s in range(SLAB)], axis=1)


def _row_copy(src, src_row, dst, dst_row, sem):
    return pltpu.make_async_copy(src.at[pl.ds(pl.multiple_of(src_row * SLAB, SLAB), SLAB)],
                                 dst.at[pl.ds(pl.multiple_of(dst_row * SLAB, SLAB), SLAB)], sem)


def _start_tile_rows(copy_of_row):
    def trip(c, carry):
        for k in range(ROW_UNROLL):
            copy_of_row(c * ROW_UNROLL + k).start(priority=k % 2)
        return carry
    lax.fori_loop(0, TM_ROW // ROW_UNROLL, trip, 0)


def _tile_wait(src, dst, sem):
    pltpu.make_async_copy(src.at[pl.ds(0, TM_ROW * SLAB)], dst, sem).wait()


def _scatter_kernel(pos_ref, meta_ref, h_ref, o_ref, zero_tile, sem, pad_sem):
    @pl.when(pl.program_id(0) == 0)
    def _():
        zero_tile[...] = jnp.zeros_like(zero_tile)

        def per_bucket(b, carry):
            lo, hi = meta_ref[META_PAD_START, b], meta_ref[META_PAD_END, b]
            pad_copy = lambda r: _row_copy(zero_tile, 0, o_ref, r, pad_sem)
            lax.fori_loop(lo, hi, lambda r, c: (pad_copy(r).start(), c)[1], 0)
            lax.fori_loop(lo, hi, lambda r, c: (pad_copy(r).wait(), c)[1], 0)
            return carry

        lax.fori_loop(0, N_BUCKETS, per_bucket, 0)

        def tile_copy(j):
            rows = pl.ds(pl.multiple_of(j * (TM_MOE * SLAB), TM_MOE * SLAB), TM_MOE * SLAB)
            return pltpu.make_async_copy(zero_tile, o_ref.at[rows], pad_sem)

        first_unused = meta_ref[META_N_TILES, 0]
        lax.fori_loop(first_unused, NT_MOE, lambda j, c: (tile_copy(j).start(), c)[1], 0)
        lax.fori_loop(first_unused, NT_MOE, lambda j, c: (tile_copy(j).wait(), c)[1], 0)

    _start_tile_rows(lambda r: _row_copy(h_ref, r, o_ref, pos_ref[0, 0, r], sem))
    _tile_wait(h_ref, o_ref.at[pl.ds(0, TM_ROW * SLAB)], sem)


def _scatter_call(pos, meta, h2):
    tm = TM_ROW
    return pl.pallas_call(
        _scatter_kernel,
        grid=(TOKENS // tm,),
        in_specs=[
            pl.BlockSpec((1, 1, tm), lambda i: (i, 0, 0), memory_space=pltpu.SMEM),
            pl.BlockSpec(memory_space=pltpu.SMEM),
            pl.BlockSpec((tm * SLAB, 128), lambda i: (i, 0)),
        ],
        out_specs=pl.BlockSpec(memory_space=pl.ANY),
        out_shape=jax.ShapeDtypeStruct((SORTED_ROWS * SLAB, 128), F32),
        scratch_shapes=[pltpu.VMEM((TM_MOE * SLAB, 128), F32), pltpu.SemaphoreType.DMA(()),
                        pltpu.SemaphoreType.DMA(())],
        compiler_params=_params(1),
        name="scatter",
    )(pos, meta, h2)


def _gather_kernel(pos_ref, pos_next_ref, ys_ref, x1_ref, gate2_ref, o_ref, buf, sem):
    i = pl.program_id(0)
    n = pl.num_programs(0)
    slot = lax.rem(i, 2)

    def fetch(p_ref, s):
        _start_tile_rows(lambda r: _row_copy(ys_ref, p_ref[0, 0, r], buf.at[s], r, sem.at[s]))

    @pl.when(i == 0)
    def _():
        fetch(pos_ref, 0)

    @pl.when(i + 1 < n)
    def _():
        fetch(pos_next_ref, 1 - slot)

    _tile_wait(ys_ref, buf.at[slot], sem.at[slot])
    o_ref[...] = x1_ref[...] + gate2_ref[0] * _load_slabs(buf.at[slot], TM_ROW)


def _gather_call(pos, y_sorted, x1, gate2):
    tm = TM_ROW
    per_batch = SEQ // tm
    n = TOKENS // tm
    return pl.pallas_call(
        _gather_kernel,
        grid=(n,),
        in_specs=[
            pl.BlockSpec((1, 1, tm), lambda i: (i, 0, 0), memory_space=pltpu.SMEM),
            pl.BlockSpec((1, 1, tm), lambda i: (jnp.minimum(i + 1, n - 1), 0, 0), memory_space=pltpu.SMEM),
            pl.BlockSpec(memory_space=pl.ANY),
            pl.BlockSpec((tm, D_MODEL), lambda i: (i, 0)),
            pl.BlockSpec((1, 1, D_MODEL), lambda i: (i // per_batch, 0, 0)),
        ],
        out_specs=pl.BlockSpec((tm, D_MODEL), lambda i: (i, 0)),
        out_shape=jax.ShapeDtypeStruct((TOKENS, D_MODEL), F32),
        scratch_shapes=[pltpu.VMEM((2, tm * SLAB, 128), F32), pltpu.SemaphoreType.DMA((2,))],
        compiler_params=_params(1),
        name="gather",
    )(pos, pos, y_sorted, x1, gate2)


def _moe_kernel(tb_ref, nt_ref, tab_ref, h_ref, wr_ref, brow_ref, wg_lo, wu_lo, wd_lo, wg_hi, wu_hi, wd_hi, o_ref,
                wg_lo_b, wu_lo_b, wd_lo_b, wg_hi_b, wu_hi_b, wd_hi_b):
    j = pl.program_id(0)

    @pl.when(j < nt_ref[0])
    def _():
        b = tb_ref[j]
        e_lo = tab_ref[b]
        e_hi = tab_ref[N_BUCKETS + b]
        g = tab_ref[2 * N_BUCKETS + b]

        b_before = tb_ref[jnp.maximum(j - 1, 0)]
        for e_now, e_before, pairs in (
                (e_lo, tab_ref[b_before], ((wg_lo, wg_lo_b), (wu_lo, wu_lo_b), (wd_lo, wd_lo_b))),
                (e_hi, tab_ref[N_BUCKETS + b_before], ((wg_hi, wg_hi_b), (wu_hi, wu_hi_b), (wd_hi, wd_hi_b)))):
            @pl.when((j == 0) | (e_now != e_before))
            def _():
                for src, dst in pairs:
                    dst[...] = src[...].astype(BF16)

        h = _load_slabs(h_ref, TM_MOE).astype(BF16)
        lg = _dot(h, wr_ref[...]) + brow_ref[...]
        lane = lax.broadcasted_iota(I32, lg.shape, 1)
        is_group = lane < N_EXPERT_GROUPS
        gmax = jnp.max(jnp.where(is_group, lg, NEG_INF), axis=1, keepdims=True)
        ge = jnp.exp(lg - gmax)
        g_w = (jnp.sum(jnp.where(lane == g, ge, 0.0), axis=1, keepdims=True)
               / jnp.sum(jnp.where(is_group, ge, 0.0), axis=1, keepdims=True))
        v_lo = jnp.sum(jnp.where(lane == EXPERT_ROW0 + e_lo, lg, 0.0), axis=1, keepdims=True)
        v_hi = jnp.sum(jnp.where(lane == EXPERT_ROW0 + e_hi, lg, 0.0), axis=1, keepdims=True)
        m = jnp.maximum(v_lo, v_hi)
        x_lo = jnp.exp(v_lo - m)
        x_hi = jnp.exp(v_hi - m)
        w_lo = x_lo / (x_lo + x_hi) * g_w
        w_hi = x_hi / (x_lo + x_hi) * g_w

        def hidden(wg, wu, w):
            a = _dot(h, wg[...])
            return (a * _sigmoid(a) * _dot(h, wu[...]) * w).astype(BF16)

        y = (_dot(hidden(wg_lo_b, wu_lo_b, w_lo), wd_lo_b[...])
             + _dot(hidden(wg_hi_b, wu_hi_b, w_hi), wd_hi_b[...]))
        _store_slabs(o_ref, y)

    @pl.when(j >= nt_ref[0])
    def _():
        o_ref[...] = jnp.zeros_like(o_ref)


def _moe_call(layer, tile_bucket, n_tiles, tables, h_sorted, wr, br_row, w_gate, w_up, w_down):
    tm = TM_MOE

    def last(j, nt):
        return jnp.minimum(j, nt[0] - 1)

    def w_spec(shape, which):
        def index(j, tb, nt, tab):
            expert = tab[which * N_BUCKETS + tb[last(j, nt)]]
            return (layer, expert // EXPERTS_PER_GROUP, expert % EXPERTS_PER_GROUP, 0, 0)
        return pl.BlockSpec((None, None, None) + shape, index)

    up = (D_MODEL, D_EXPERT)
    down = (D_EXPERT, D_MODEL)
    grid_spec = pltpu.PrefetchScalarGridSpec(
        num_scalar_prefetch=3,
        grid=(NT_MOE,),
        in_specs=[
            pl.BlockSpec((tm * SLAB, 128), lambda j, tb, nt, tab: (last(j, nt), 0)),
            pl.BlockSpec((D_MODEL, 128), lambda j, tb, nt, tab: (0, 0)),
            pl.BlockSpec((1, 128), lambda j, tb, nt, tab: (0, 0)),
            w_spec(up, 0), w_spec(up, 0), w_spec(down, 0),
            w_spec(up, 1), w_spec(up, 1), w_spec(down, 1),
        ],
        out_specs=pl.BlockSpec((tm * SLAB, 128), lambda j, tb, nt, tab: (j, 0)),
        scratch_shapes=[pltpu.VMEM(up, BF16), pltpu.VMEM(up, BF16), pltpu.VMEM(down, BF16)] * 2,
    )
    return pl.pallas_call(
        _moe_kernel,
        grid_spec=grid_spec,
        out_shape=jax.ShapeDtypeStruct((SORTED_ROWS * SLAB, 128), F32),
        compiler_params=_params(1),
        name="moe",
    )(tile_bucket, n_tiles, tables, h_sorted, wr, br_row, w_gate, w_up, w_down, w_gate, w_up, w_down)


def _alibi_query_lanes():
    lanes = np.zeros((1, N_HEADS * HEAD_SLOT), np.float32)
    for hd, slope in enumerate(_alibi_slopes()):
        rest = np.float32(slope * LOG2E)
        for part in range(ALIBI_PARTS):
            piece = np.float32(np.asarray(rest).astype(jnp.bfloat16))
            lanes[0, hd * HEAD_SLOT + HEAD_DIM + part] = piece
            rest = np.float32(rest - piece)
    return jnp.asarray(lanes)


def _bucket_tables():
    e_lo, e_hi, grp = [], [], []
    for g in range(N_EXPERT_GROUPS):
        for lo, hi in PAIRS:
            e_lo.append(g * EXPERTS_PER_GROUP + lo)
            e_hi.append(g * EXPERTS_PER_GROUP + hi)
            grp.append(g)
    return jnp.asarray(e_lo + e_hi + grp, I32)


def kernel(x, c, w_ada, b_ada, norm1_gain, w_in, b_branch_gate, q_norm_gain, k_norm_gain, attn_sinks, gmlp_norm_gain, gmlp_norm_bias, gmlp_w_spatial, gmlp_b_spatial, w_o_attn, w_o_gmlp, w_out, norm2_gain, w_group_router, b_group_router, w_expert_router, b_expert_router, w_expert_gate, w_expert_up, w_expert_down):
    depth = w_ada.shape[0]
    x2 = x.reshape(TOKENS, D_MODEL)
    tables = _bucket_tables()
    q_aug = _alibi_query_lanes()
    row = lambda v: v.reshape(1, -1)
    for l in range(depth):
        mod = _mod_call(c, w_ada[l], b_ada[l])
        shift1, scale1, gate1, shift2, scale2, gate2 = [
            m.reshape(BATCH, 1, D_MODEL) for m in jnp.split(mod, 6, axis=-1)]

        slot_pad = jnp.zeros((HEAD_SLOT - HEAD_DIM,), F32)
        q_gain_row = row(jnp.tile(jnp.concatenate([q_norm_gain[l] * (HEAD_DIM ** -0.5 * LOG2E), slot_pad]), N_HEADS))
        k_gain_row = row(jnp.tile(jnp.concatenate([k_norm_gain[l], slot_pad]), N_KV_HEADS))
        q, k, vt, u, vg, gates = _inproj_call(
            x2, scale1, shift1, row(norm1_gain[l]), w_in[l].astype(BF16), w_in[l][:, V0:V1].T.astype(BF16),
            q_gain_row, k_gain_row, row(gmlp_norm_gain[l]), row(gmlp_norm_bias[l]), row(b_branch_gate[l]), q_aug)

        y_attn = _attn_call(attn_sinks[l], q, k, vt)

        bias_map = jnp.repeat(gmlp_b_spatial[l].T, GMLP_WIDTH // GMLP_GROUPS, axis=1)
        e0, e1 = EXPERT_ROW0, EXPERT_ROW0 + N_EXPERTS
        wr = jnp.zeros((D_MODEL, 128), F32)
        wr = wr.at[:, 0:N_EXPERT_GROUPS].set(w_group_router[l]).at[:, e0:e1].set(w_expert_router[l]).astype(BF16)
        br_row = jnp.zeros((1, 128), F32)
        br_row = br_row.at[0, 0:N_EXPERT_GROUPS].set(b_group_router[l]).at[0, e0:e1].set(b_expert_router[l])
        x1, h2p, buckets, totals = _mixer_call(
            x2, y_attn, u, vg, gates, gmlp_w_spatial[l], bias_map, w_o_attn[l].astype(BF16),
            w_o_gmlp[l].astype(BF16), w_out[l].astype(BF16), gate1, row(norm2_gain[l]), scale2, shift2,
            wr[:, 0:BUCKET_ROWS].T, br_row[:, 0:BUCKET_ROWS].T)

        pos, meta = _rank_call(buckets, totals)
        h_sorted = _scatter_call(pos, meta, h2p)

        y_sorted = _moe_call(
            l, meta[META_TILE_BUCKET], meta[META_N_TILES, 0:1], tables, h_sorted, wr, br_row,
            w_expert_gate, w_expert_up, w_expert_down)

        x2 = _gather_call(pos, y_sorted, x1, gate2)
    return x2.reshape(x.shape)
```

```python
import functools

import jax
import jax.numpy as jnp
import numpy as np
from jax import lax
from jax.experimental import pallas as pl
from jax.experimental.pallas import tpu as pltpu

D_MODEL = 1024
BATCH = 8
SEQ = 4096
TOKENS = BATCH * SEQ
N_HEADS = 16
N_KV_HEADS = 4
HEAD_DIM = 64
Q_PER_KV = N_HEADS // N_KV_HEADS
BLOCK = 128
HEAD_SLOT = 128
ALIBI_PARTS = 3
ATTN_WIDTH = N_HEADS * HEAD_DIM
KV_WIDTH = N_KV_HEADS * HEAD_DIM
GMLP_WIDTH = 1024
GMLP_GROUPS = 8
GMLP_CHUNK = 128
N_EXPERT_GROUPS = 4
EXPERTS_PER_GROUP = 4
N_EXPERTS = N_EXPERT_GROUPS * EXPERTS_PER_GROUP
D_EXPERT = 512
IN_WIDTH = ATTN_WIDTH + 2 * KV_WIDTH + 2 * GMLP_WIDTH + 2 * D_MODEL
EPS = 1e-6
NEG_INF = -1e30
LOG2E = float(np.float32(np.log2(np.e)))

Q0, Q1 = 0, ATTN_WIDTH
K0, K1 = Q1, Q1 + KV_WIDTH
V0, V1 = K1, K1 + KV_WIDTH
U0, U1 = V1, V1 + GMLP_WIDTH
G0, G1 = U1, U1 + GMLP_WIDTH
B0, B1 = G1, G1 + 2 * D_MODEL

PAIRS = ((0, 1), (0, 2), (0, 3), (1, 2), (1, 3), (2, 3))
N_BUCKETS = N_EXPERT_GROUPS * len(PAIRS)
BUCKET_ROWS = 32
EXPERT_ROW0 = 8

TM_PROJ = 512
PROJ_CHUNK = 512
TM_ATTN = 512
TM_MIX = 512
TM_MOE = 256
TM_ROW = 512
SCORES_AHEAD = 6
ROW_UNROLL = 16
NT_MOE = TOKENS // TM_MOE + N_BUCKETS
SORTED_ROWS = NT_MOE * TM_MOE
SLAB = D_MODEL // 128
META_LANES = 256
assert NT_MOE <= META_LANES

VMEM_LIMIT = 56 * 1024 * 1024

F32 = jnp.float32
BF16 = jnp.bfloat16
U32 = jnp.uint32
I32 = jnp.int32


def _alibi_slopes():
    return [float(np.float32(2.0 ** (-8.0 * (i + 1) / N_HEADS))) for i in range(N_HEADS)]


def _params(n_axes):
    return pltpu.CompilerParams(dimension_semantics=("arbitrary",) * n_axes, vmem_limit_bytes=VMEM_LIMIT)


def _sigmoid(x):
    return 1.0 / (1.0 + jnp.exp(-x))


def _gelu(x):
    return 0.5 * x * (1.0 + lax.erf(x * np.float32(1.0 / np.sqrt(2.0))))


def _dot(a, b):
    return jnp.dot(a, b, preferred_element_type=F32)


def _dot_nt(a, b):
    return lax.dot_general(a, b, (((1,), (1,)), ((), ())), preferred_element_type=F32)


def _mod_kernel(c_ref, w_ref, b_ref, o_ref):
    c = c_ref[...]
    ca = c * _sigmoid(c)
    o_ref[...] = _dot(ca.astype(BF16), w_ref[...].astype(BF16)) + b_ref[...]


def _mod_call(c, w_ada, b_ada):
    n = w_ada.shape[1]
    bn = 1536
    return pl.pallas_call(
        _mod_kernel,
        grid=(n // bn,),
        in_specs=[
            pl.BlockSpec((BATCH, D_MODEL), lambda j: (0, 0)),
            pl.BlockSpec((D_MODEL, bn), lambda j: (0, j)),
            pl.BlockSpec((1, bn), lambda j: (0, j)),
        ],
        out_specs=pl.BlockSpec((BATCH, bn), lambda j: (0, j)),
        out_shape=jax.ShapeDtypeStruct((BATCH, n), F32),
        compiler_params=_params(1),
        name="mod",
    )(c, w_ada, b_ada.reshape(1, n))


def _round_kernel(w_ref, o_ref):
    o_ref[...] = w_ref[...].astype(BF16)


def _round_call(w, layer):
    _, k, n = w.shape
    bn = 512
    return pl.pallas_call(
        _round_kernel,
        grid=(n // bn,),
        in_specs=[pl.BlockSpec((None, k, bn), lambda j: (layer, 0, j))],
        out_specs=pl.BlockSpec((k, bn), lambda j: (0, j)),
        out_shape=jax.ShapeDtypeStruct((k, n), BF16),
        compiler_params=_params(1),
        name="round_weight",
    )(w)


def _inproj_kernel(x_ref, sc_ref, sh_ref, ng_ref, w_ref, wvt_ref, qg_ref, kg_ref, lng_ref, lnb_ref, bg_ref, qaug_ref,
                   q_o, k_o, vt_o, u_o, vg_o, gt_o):
    x = x_ref[...]
    ms = jnp.mean(x * x, axis=-1, keepdims=True)
    h = (x * lax.rsqrt(ms + EPS)) * ng_ref[...]
    h = h * (1.0 + sc_ref[0]) + sh_ref[0]
    hb = h.astype(BF16)

    def proj(c0, c1):
        return _dot(hb, w_ref[:, c0:c1])

    low_half = lax.broadcasted_iota(I32, (1, HEAD_SLOT), 1) < HEAD_DIM

    def store_normed_heads(raw, gain_ref, spare_ref, o_ref):
        for p in range(raw.shape[1] // HEAD_SLOT):
            pair = raw[:, p * HEAD_SLOT:(p + 1) * HEAD_SLOT]
            for hd, head in ((2 * p, pair), (2 * p + 1, pltpu.roll(pair, HEAD_DIM, 1))):
                cols = slice(hd * HEAD_SLOT, (hd + 1) * HEAD_SLOT)
                xh = jnp.where(low_half, head, 0.0)
                r = lax.rsqrt(jnp.sum(xh * xh, axis=-1, keepdims=True) * (1.0 / HEAD_DIM) + EPS)
                y = xh * r * gain_ref[:, cols]
                if spare_ref is not None:
                    y = y + spare_ref[:, cols]
                o_ref[:, cols] = y.astype(BF16)

    store_normed_heads(proj(Q0, Q1), qg_ref, qaug_ref, q_o)
    store_normed_heads(proj(K0, K1), kg_ref, None, k_o)
    vt_o[...] = _dot_nt(wvt_ref[...], hb).astype(BF16)

    cw = PROJ_CHUNK

    def gate_chunk(c):
        cols = slice(c * cw, (c + 1) * cw)
        gt_o[:, cols] = _sigmoid(proj(B0 + c * cw, B0 + (c + 1) * cw) + bg_ref[:, cols]).astype(BF16)

    vg_parts = []
    for c in range(GMLP_WIDTH // cw):
        u_o[:, c * cw:(c + 1) * cw] = _gelu(proj(U0 + c * cw, U0 + (c + 1) * cw)).astype(BF16)
        gate_chunk(2 * c)
        vg_parts.append(_gelu(proj(G0 + c * cw, G0 + (c + 1) * cw)))
        gate_chunk(2 * c + 1)
    vg = jnp.concatenate(vg_parts, axis=1)
    mu = jnp.mean(vg, axis=-1, keepdims=True)
    vc = vg - mu
    var = jnp.mean(vc * vc, axis=-1, keepdims=True)
    vg_o[...] = (vc * lax.rsqrt(var + EPS) * lng_ref[...] + lnb_ref[...]).astype(BF16)


def _inproj_call(x2, scale1, shift1, norm_gain, w_in, wv_t, q_gain_row, k_gain_row, ln_gain, ln_bias, b_gate, q_aug):
    tm = TM_PROJ
    per_batch = SEQ // tm
    row = lambda n: pl.BlockSpec((1, n), lambda i: (0, 0))
    modrow = pl.BlockSpec((1, 1, D_MODEL), lambda i: (i // per_batch, 0, 0))
    tile = lambda n: pl.BlockSpec((tm, n), lambda i: (i, 0))
    out = lambda n: jax.ShapeDtypeStruct((TOKENS, n), BF16)
    return pl.pallas_call(
        _inproj_kernel,
        grid=(TOKENS // tm,),
        in_specs=[
            tile(D_MODEL), modrow, modrow, row(D_MODEL),
            pl.BlockSpec((D_MODEL, IN_WIDTH), lambda i: (0, 0)),
            pl.BlockSpec((KV_WIDTH, D_MODEL), lambda i: (0, 0)),
            row(N_HEADS * HEAD_SLOT), row(N_KV_HEADS * HEAD_SLOT), row(GMLP_WIDTH), row(GMLP_WIDTH),
            row(2 * D_MODEL), row(N_HEADS * HEAD_SLOT),
        ],
        out_specs=[tile(N_HEADS * HEAD_SLOT), tile(N_KV_HEADS * HEAD_SLOT),
                   pl.BlockSpec((KV_WIDTH, tm), lambda i: (0, i)),
                   tile(GMLP_WIDTH), tile(GMLP_WIDTH), tile(2 * D_MODEL)],
        out_shape=[out(N_HEADS * HEAD_SLOT), out(N_KV_HEADS * HEAD_SLOT),
                   jax.ShapeDtypeStruct((KV_WIDTH, TOKENS), BF16),
                   out(GMLP_WIDTH), out(GMLP_WIDTH), out(2 * D_MODEL)],
        compiler_params=_params(1),
        name="inproj",
    )(x2, scale1, shift1, norm_gain, w_in, wv_t, q_gain_row, k_gain_row, ln_gain, ln_bias, b_gate, q_aug)


def _attn_kernel(sink_ref, q_ref, kc_ref, kp_ref, vc_ref, vp_ref, o_ref):
    i = pl.program_id(1)
    slopes = _alibi_slopes()
    kj = lax.broadcasted_iota(I32, (BLOCK, 2 * BLOCK), 0)
    qi = lax.broadcasted_iota(I32, (BLOCK, 2 * BLOCK), 1) & (BLOCK - 1)
    own = kj <= qi
    from_own = jnp.where(own, 1.0, 0.0).astype(BF16)
    from_prev = jnp.where(own, 0.0, 1.0).astype(BF16)
    has_prev = i > 0
    low_half = lax.broadcasted_iota(I32, (1, HEAD_SLOT), 1) < HEAD_DIM
    k_lane = lax.broadcasted_iota(I32, (2 * BLOCK, HEAD_SLOT), 1)
    k_aug = jnp.where((k_lane >= HEAD_DIM) & (k_lane < HEAD_DIM + ALIBI_PARTS),
                      lax.broadcasted_iota(I32, (2 * BLOCK, HEAD_SLOT), 0), 0).astype(F32).astype(BF16)
    q_pos = (lax.broadcasted_iota(I32, (1, BLOCK), 1) + BLOCK).astype(F32)

    keys_of = {}

    def scores(item):
        sb, kv, pr = item
        r0, r1 = sb * BLOCK, (sb + 1) * BLOCK
        if (sb, kv) not in keys_of:
            cols = slice(kv * HEAD_SLOT, (kv + 1) * HEAD_SLOT)
            k_prev = kp_ref[:, cols] if sb == 0 else kc_ref[r0 - BLOCK:r0, cols]
            keys_of[sb, kv] = jnp.where(low_half, jnp.concatenate([k_prev, kc_ref[r0:r1, cols]], axis=0), k_aug)
        ha = kv * Q_PER_KV + 2 * pr
        queries = jnp.concatenate([q_ref[r0:r1, ha * HEAD_SLOT:(ha + 1) * HEAD_SLOT],
                                   q_ref[r0:r1, (ha + 1) * HEAD_SLOT:(ha + 2) * HEAD_SLOT]], axis=0)
        s = _dot_nt(keys_of[sb, kv], queries)
        s_prev = s[0:BLOCK, :]
        if sb == 0:
            s_prev = jnp.where(has_prev, s_prev, NEG_INF)
        return jnp.where(own, s[BLOCK:2 * BLOCK, :], s_prev)

    def attend(item, s):
        sb, kv, pr = item
        r0, r1 = sb * BLOCK, (sb + 1) * BLOCK
        rows = slice(kv * HEAD_DIM, (kv + 1) * HEAD_DIM)
        v_prev = vp_ref[rows, :] if sb == 0 else vc_ref[rows, r0 - BLOCK:r0]
        vt = jnp.concatenate([v_prev, vc_ref[rows, r0:r1]], axis=1)
        ha = kv * Q_PER_KV + 2 * pr
        hb = ha + 1
        sink = LOG2E * jnp.concatenate(
            [sink_ref[ha] + slopes[ha] * q_pos, sink_ref[hb] + slopes[hb] * q_pos], axis=1)
        m = jnp.maximum(jnp.max(s, axis=0, keepdims=True), sink)
        p = jnp.exp2(s - m)
        den = jnp.sum(p, axis=0, keepdims=True) + jnp.exp2(sink - m)
        pb = p.astype(BF16)
        p_keys = jnp.concatenate([pb * from_prev, pb * from_own], axis=0)
        o = _dot(vt, p_keys) / den
        pair = jnp.concatenate([o[:, 0:BLOCK], o[:, BLOCK:2 * BLOCK]], axis=0).T
        o_ref[r0:r1, ha * HEAD_DIM:(hb + 1) * HEAD_DIM] = pair.astype(BF16)

    items = [(sb, kv, pr) for sb in range(TM_ATTN // BLOCK) for kv in range(N_KV_HEADS)
             for pr in range(Q_PER_KV // 2)]
    pending = []
    for n in range(len(items) + SCORES_AHEAD):
        if n < len(items):
            pending.append(scores(items[n]))
        if n >= SCORES_AHEAD:
            attend(items[n - SCORES_AHEAD], pending.pop(0))


def _attn_call(sinks, q, k, vt):
    tq = TM_ATTN
    per_batch = SEQ // tq
    blocks_per_tile = tq // BLOCK
    q_width = N_HEADS * HEAD_SLOT
    k_width = N_KV_HEADS * HEAD_SLOT
    cur = lambda n: pl.BlockSpec((tq, n), lambda b, i: (b * per_batch + i, 0))
    prev_block = lambda b, i: b * (SEQ // BLOCK) + jnp.maximum(i * blocks_per_tile - 1, 0)
    return pl.pallas_call(
        _attn_kernel,
        grid=(BATCH, per_batch),
        in_specs=[
            pl.BlockSpec(memory_space=pltpu.SMEM),
            cur(q_width), cur(k_width),
            pl.BlockSpec((BLOCK, k_width), lambda b, i: (prev_block(b, i), 0)),
            pl.BlockSpec((KV_WIDTH, tq), lambda b, i: (0, b * per_batch + i)),
            pl.BlockSpec((KV_WIDTH, BLOCK), lambda b, i: (0, prev_block(b, i))),
        ],
        out_specs=cur(ATTN_WIDTH),
        out_shape=jax.ShapeDtypeStruct((TOKENS, ATTN_WIDTH), BF16),
        compiler_params=_params(2),
        name="attn",
    )(sinks, q, k, k, vt, vt)


def _first_argmax(vals):
    m = vals[0]
    for v in vals[1:]:
        m = jnp.maximum(m, v)
    idx = jnp.full(m.shape, len(vals) - 1, I32)
    for k in range(len(vals) - 2, -1, -1):
        idx = jnp.where(vals[k] == m, k, idx)
    return m, idx


def _mixer_kernel(x_ref, ya_ref, u_ref, vg_ref, gt_ref, ws_ref, bmap_ref, woa_ref, wog_ref, wout_ref, gate1_ref,
                  ng_ref, sc_ref, sh_ref, wrt_ref, brt_ref, x1_o, h2_o, bk_o, cnt_o, cnt):
    tm = TM_MIX
    t_idx = lax.broadcasted_iota(I32, (GMLP_CHUNK, GMLP_CHUNK), 0)
    s_idx = lax.broadcasted_iota(I32, (GMLP_CHUNK, GMLP_CHUNK), 1)
    ws = [jnp.where(t_idx >= s_idx, ws_ref[g], 0.0).astype(BF16) for g in range(GMLP_GROUPS)]
    gc = GMLP_WIDTH // GMLP_GROUPS
    chunks = []
    for c in range(tm // GMLP_CHUNK):
        r0, r1 = c * GMLP_CHUNK, (c + 1) * GMLP_CHUNK
        mixed = jnp.concatenate(
            [_dot(ws[g], vg_ref[r0:r1, g * gc:(g + 1) * gc]) for g in range(GMLP_GROUPS)], axis=1)
        mixed = mixed + bmap_ref[...]
        chunks.append((u_ref[r0:r1, :].astype(F32) * mixed).astype(BF16))
    y_gmlp = jnp.concatenate(chunks, axis=0)
    pa = _dot(ya_ref[...], woa_ref[...])
    pg = _dot(y_gmlp, wog_ref[...])
    merged = gt_ref[:, 0:D_MODEL].astype(F32) * pa + gt_ref[:, D_MODEL:2 * D_MODEL].astype(F32) * pg
    x1 = x_ref[...] + gate1_ref[0] * _dot(merged.astype(BF16), wout_ref[...])
    x1_o[...] = x1

    ms = jnp.mean(x1 * x1, axis=-1, keepdims=True)
    h2 = (x1 * lax.rsqrt(ms + EPS)) * ng_ref[...]
    h2 = h2 * (1.0 + sc_ref[0]) + sh_ref[0]
    _store_slabs(h2_o, h2)

    lg = _dot_nt(wrt_ref[...], h2.astype(BF16)) + brt_ref[...]
    _, gi = _first_argmax([lg[r:r + 1, :] for r in range(N_EXPERT_GROUPS)])
    el = [lg[EXPERT_ROW0 + r:EXPERT_ROW0 + r + 1, :] for r in range(N_EXPERTS)]
    eg = []
    for k in range(EXPERTS_PER_GROUP):
        v = el[(N_EXPERT_GROUPS - 1) * EXPERTS_PER_GROUP + k]
        for g in range(N_EXPERT_GROUPS - 2, -1, -1):
            v = jnp.where(gi == g, el[g * EXPERTS_PER_GROUP + k], v)
        eg.append(v)
    _, i1 = _first_argmax(eg)
    _, i2 = _first_argmax([jnp.where(i1 == k, -3e38, eg[k]) for k in range(EXPERTS_PER_GROUP)])
    lo = jnp.minimum(i1, i2)
    hi = jnp.maximum(i1, i2)
    pair = jnp.where(lo == 0, hi - 1, jnp.where(lo == 1, hi + 1, 5))
    bucket = gi * len(PAIRS) + pair
    bk_o[0] = bucket

    @pl.when(pl.program_id(0) == 0)
    def _():
        cnt[...] = jnp.zeros_like(cnt)
    in_bucket = lax.broadcasted_iota(I32, (BUCKET_ROWS, tm), 0) == bucket
    cnt[...] += jnp.sum(jnp.where(in_bucket, 1.0, 0.0), axis=1, keepdims=True)
    cnt_o[...] = cnt[...]


def _mixer_call(x2, y_attn, u, vg, gates, w_spatial, bias_map, wo_attn, wo_gmlp, w_out, gate1, norm_gain, scale2, shift2,
                wr_t, br_t):
    tm = TM_MIX
    per_batch = SEQ // tm
    nt = TOKENS // tm
    tile = lambda n: pl.BlockSpec((tm, n), lambda i: (i, 0))
    full = lambda shape: pl.BlockSpec(shape, lambda i: (0,) * len(shape))
    modrow = pl.BlockSpec((1, 1, D_MODEL), lambda i: (i // per_batch, 0, 0))
    return pl.pallas_call(
        _mixer_kernel,
        grid=(nt,),
        in_specs=[
            tile(D_MODEL), tile(ATTN_WIDTH), tile(GMLP_WIDTH), tile(GMLP_WIDTH), tile(2 * D_MODEL),
            full((GMLP_GROUPS, GMLP_CHUNK, GMLP_CHUNK)), full((GMLP_CHUNK, GMLP_WIDTH)),
            full((ATTN_WIDTH, D_MODEL)), full((GMLP_WIDTH, D_MODEL)), full((D_MODEL, D_MODEL)),
            modrow, full((1, D_MODEL)), modrow, modrow,
            full((BUCKET_ROWS, D_MODEL)), full((BUCKET_ROWS, 1)),
        ],
        out_specs=[tile(D_MODEL), pl.BlockSpec((tm * SLAB, 128), lambda i: (i, 0)),
                   pl.BlockSpec((1, 1, tm), lambda i: (i, 0, 0)), full((BUCKET_ROWS, 128))],
        out_shape=[
            jax.ShapeDtypeStruct((TOKENS, D_MODEL), F32),
            jax.ShapeDtypeStruct((TOKENS * SLAB, 128), F32),
            jax.ShapeDtypeStruct((nt, 1, tm), I32),
            jax.ShapeDtypeStruct((BUCKET_ROWS, 128), F32),
        ],
        scratch_shapes=[pltpu.VMEM((BUCKET_ROWS, 128), F32)],
        compiler_params=_params(1),
        name="mixer",
    )(x2, y_attn, u, vg, gates, w_spatial, bias_map, wo_attn, wo_gmlp, w_out, gate1, norm_gain, scale2, shift2,
      wr_t, br_t)


META_TILE_BUCKET, META_N_TILES, META_PAD_START, META_PAD_END = 0, 1, 2, 3


def _rank_kernel(bk_ref, total_ref, pos_o, meta_o, seen, off):
    tm = TM_MIX
    i = pl.program_id(0)
    bk = bk_ref[0]
    onehot = lax.broadcasted_iota(I32, (BUCKET_ROWS, tm), 0) == bk

    @pl.when(i == 0)
    def _():
        total = total_ref[...]
        tiles = jnp.floor((total + (TM_MOE - 1)) * (1.0 / TM_MOE))
        r = lax.broadcasted_iota(I32, (BUCKET_ROWS, BUCKET_ROWS), 0)
        c = lax.broadcasted_iota(I32, (BUCKET_ROWS, BUCKET_ROWS), 1)
        before = jnp.where(c < r, 1.0, 0.0).astype(BF16)
        first_tile = _dot(before, tiles.astype(BF16))
        first_row = first_tile * TM_MOE
        off[...] = first_row
        seen[...] = jnp.zeros_like(seen)
        end_tile = (first_tile + tiles)[:, 0:1]
        lane = lax.broadcasted_iota(I32, (BUCKET_ROWS, META_LANES), 1)
        bsub = lax.broadcasted_iota(I32, (BUCKET_ROWS, META_LANES), 0)
        is_bucket = bsub < N_BUCKETS
        tile_bucket = jnp.sum(jnp.where((lane.astype(F32) >= end_tile) & is_bucket, 1.0, 0.0), axis=0, keepdims=True)
        as_row = lambda col: jnp.sum(jnp.where((bsub == lane) & is_bucket, col, 0.0), axis=0, keepdims=True)
        n_tiles = jnp.sum(jnp.where(bsub == N_BUCKETS - 1, end_tile, 0.0), axis=0, keepdims=True)
        pad_start = as_row((first_row + total)[:, 0:1])
        pad_end = as_row(end_tile * TM_MOE)
        row = lax.broadcasted_iota(I32, (8, META_LANES), 0)
        meta = jnp.zeros((8, META_LANES), F32)
        for k, v in ((META_TILE_BUCKET, tile_bucket), (META_N_TILES, n_tiles), (META_PAD_START, pad_start),
                     (META_PAD_END, pad_end)):
            meta = jnp.where(row == k, v, meta)
        meta_o[...] = meta.astype(I32)

    s_idx = lax.broadcasted_iota(I32, (tm, tm), 0)
    t_idx = lax.broadcasted_iota(I32, (tm, tm), 1)
    upto = jnp.where(s_idx <= t_idx, 1.0, 0.0).astype(BF16)
    incl = _dot(jnp.where(onehot, 1.0, 0.0).astype(BF16), upto)
    base = off[:, 0:1] + seen[:, 0:1]
    posf = jnp.sum(jnp.where(onehot, base + incl - 1.0, 0.0), axis=0, keepdims=True)
    pos_o[0] = posf.astype(I32)
    seen[...] += incl[:, tm - 1:tm]


def _rank_call(buckets, totals):
    nt, _, tm = buckets.shape
    return pl.pallas_call(
        _rank_kernel,
        grid=(nt,),
        in_specs=[pl.BlockSpec((1, 1, tm), lambda i: (i, 0, 0)),
                  pl.BlockSpec((BUCKET_ROWS, 128), lambda i: (0, 0))],
        out_specs=[
            pl.BlockSpec((1, 1, tm), lambda i: (i, 0, 0)),
            pl.BlockSpec((8, META_LANES), lambda i: (0, 0)),
        ],
        out_shape=[
            jax.ShapeDtypeStruct((nt, 1, tm), I32),
            jax.ShapeDtypeStruct((8, META_LANES), I32),
        ],
        scratch_shapes=[pltpu.VMEM((BUCKET_ROWS, 128), F32), pltpu.VMEM((BUCKET_ROWS, 128), F32)],
        compiler_params=_params(1),
        name="rank",
    )(buckets, totals)


def _store_slabs(ref, x):
    n = x.shape[0]
    for s in range(SLAB):
        ref[pl.ds(s, n, stride=SLAB), :] = x[:, s * 128:(s + 1) * 128]


def _load_slabs(ref, n):
    return jnp.concatenate([ref[pl.ds(s, n, stride=SLAB), :] for s in range(SLAB)], axis=1)


def _row_copy(src, src_row, dst, dst_row, sem):
    return pltpu.make_async_copy(src.at[pl.ds(pl.multiple_of(src_row * SLAB, SLAB), SLAB)],
                                 dst.at[pl.ds(pl.multiple_of(dst_row * SLAB, SLAB), SLAB)], sem)


def _start_tile_rows(copy_of_row):
    def trip(c, carry):
        for k in range(ROW_UNROLL):
            copy_of_row(c * ROW_UNROLL + k).start(priority=k % 2)
        return carry
    lax.fori_loop(0, TM_ROW // ROW_UNROLL, trip, 0)


def _tile_wait(src, dst, sem):
    pltpu.make_async_copy(src.at[pl.ds(0, TM_ROW * SLAB)], dst, sem).wait()


def _scatter_kernel(pos_ref, meta_ref, h_ref, o_ref, zero_tile, sem, pad_sem):
    first_step = pl.program_id(0) == 0
    pad_copy = lambda r: _row_copy(zero_tile, 0, o_ref, r, pad_sem)

    def tile_copy(j):
        rows = pl.ds(pl.multiple_of(j * (TM_MOE * SLAB), TM_MOE * SLAB), TM_MOE * SLAB)
        return pltpu.make_async_copy(zero_tile, o_ref.at[rows], pad_sem)

    def for_all_zero_copies(act):
        def per_bucket(b, carry):
            lo, hi = meta_ref[META_PAD_START, b], meta_ref[META_PAD_END, b]
            return lax.fori_loop(lo, hi, lambda r, c: (act(pad_copy(r)), c)[1], carry)
        lax.fori_loop(0, N_BUCKETS, per_bucket, 0)
        lax.fori_loop(meta_ref[META_N_TILES, 0], NT_MOE, lambda j, c: (act(tile_copy(j)), c)[1], 0)

    @pl.when(first_step)
    def _():
        zero_tile[...] = jnp.zeros_like(zero_tile)
        for_all_zero_copies(lambda cp: cp.start())

    _start_tile_rows(lambda r: _row_copy(h_ref, r, o_ref, pos_ref[0, 0, r], sem))

    @pl.when(first_step)
    def _():
        for_all_zero_copies(lambda cp: cp.wait())

    _tile_wait(h_ref, o_ref.at[pl.ds(0, TM_ROW * SLAB)], sem)


def _scatter_call(pos, meta, h2):
    tm = TM_ROW
    return pl.pallas_call(
        _scatter_kernel,
        grid=(TOKENS // tm,),
        in_specs=[
            pl.BlockSpec((1, 1, tm), lambda i: (i, 0, 0), memory_space=pltpu.SMEM),
            pl.BlockSpec(memory_space=pltpu.SMEM),
            pl.BlockSpec((tm * SLAB, 128), lambda i: (i, 0)),
        ],
        out_specs=pl.BlockSpec(memory_space=pl.ANY),
        out_shape=jax.ShapeDtypeStruct((SORTED_ROWS * SLAB, 128), F32),
        scratch_shapes=[pltpu.VMEM((TM_MOE * SLAB, 128), F32), pltpu.SemaphoreType.DMA(()),
                        pltpu.SemaphoreType.DMA(())],
        compiler_params=_params(1),
        name="scatter",
    )(pos, meta, h2)


def _gather_kernel(pos_ref, pos_next_ref, ys_ref, x1_ref, gate2_ref, o_ref, buf, sem):
    i = pl.program_id(0)
    n = pl.num_programs(0)
    slot = lax.rem(i, 2)

    def fetch(p_ref, s):
        _start_tile_rows(lambda r: _row_copy(ys_ref, p_ref[0, 0, r], buf.at[s], r, sem.at[s]))

    @pl.when(i == 0)
    def _():
        fetch(pos_ref, 0)

    @pl.when(i + 1 < n)
    def _():
        fetch(pos_next_ref, 1 - slot)

    _tile_wait(ys_ref, buf.at[slot], sem.at[slot])
    o_ref[...] = x1_ref[...] + gate2_ref[0] * _load_slabs(buf.at[slot], TM_ROW)


def _gather_call(pos, y_sorted, x1, gate2):
    tm = TM_ROW
    per_batch = SEQ // tm
    n = TOKENS // tm
    return pl.pallas_call(
        _gather_kernel,
        grid=(n,),
        in_specs=[
            pl.BlockSpec((1, 1, tm), lambda i: (i, 0, 0), memory_space=pltpu.SMEM),
            pl.BlockSpec((1, 1, tm), lambda i: (jnp.minimum(i + 1, n - 1), 0, 0), memory_space=pltpu.SMEM),
            pl.BlockSpec(memory_space=pl.ANY),
            pl.BlockSpec((tm, D_MODEL), lambda i: (i, 0)),
            pl.BlockSpec((1, 1, D_MODEL), lambda i: (i // per_batch, 0, 0)),
        ],
        out_specs=pl.BlockSpec((tm, D_MODEL), lambda i: (i, 0)),
        out_shape=jax.ShapeDtypeStruct((TOKENS, D_MODEL), F32),
        scratch_shapes=[pltpu.VMEM((2, tm * SLAB, 128), F32), pltpu.SemaphoreType.DMA((2,))],
        compiler_params=_params(1),
        name="gather",
    )(pos, pos, y_sorted, x1, gate2)


def _moe_kernel(tb_ref, nt_ref, tab_ref, h_ref, wr_ref, brow_ref, wg_lo, wu_lo, wd_lo, wg_hi, wu_hi, wd_hi, o_ref):
    j = pl.program_id(0)

    @pl.when(j < nt_ref[0])
    def _():
        b = tb_ref[j]
        e_lo = tab_ref[b]
        e_hi = tab_ref[N_BUCKETS + b]
        g = tab_ref[2 * N_BUCKETS + b]
        h = _load_slabs(h_ref, TM_MOE).astype(BF16)
        lg = _dot(h, wr_ref[...]) + brow_ref[...]
        lane = lax.broadcasted_iota(I32, lg.shape, 1)
        is_group = lane < N_EXPERT_GROUPS
        gmax = jnp.max(jnp.where(is_group, lg, NEG_INF), axis=1, keepdims=True)
        ge = jnp.exp(lg - gmax)
        g_w = (jnp.sum(jnp.where(lane == g, ge, 0.0), axis=1, keepdims=True)
               / jnp.sum(jnp.where(is_group, ge, 0.0), axis=1, keepdims=True))
        v_lo = jnp.sum(jnp.where(lane == EXPERT_ROW0 + e_lo, lg, 0.0), axis=1, keepdims=True)
        v_hi = jnp.sum(jnp.where(lane == EXPERT_ROW0 + e_hi, lg, 0.0), axis=1, keepdims=True)
        m = jnp.maximum(v_lo, v_hi)
        x_lo = jnp.exp(v_lo - m)
        x_hi = jnp.exp(v_hi - m)
        w_lo = x_lo / (x_lo + x_hi) * g_w
        w_hi = x_hi / (x_lo + x_hi) * g_w

        def hidden(wg, wu, w):
            a = _dot(h, wg[...])
            return (a * _sigmoid(a) * _dot(h, wu[...]) * w).astype(BF16)

        y = _dot(hidden(wg_lo, wu_lo, w_lo), wd_lo[...]) + _dot(hidden(wg_hi, wu_hi, w_hi), wd_hi[...])
        _store_slabs(o_ref, y)

    @pl.when(j >= nt_ref[0])
    def _():
        o_ref[...] = jnp.zeros_like(o_ref)


def _moe_call(tile_bucket, n_tiles, tables, h_sorted, wr, br_row, w_gate, w_up, w_down):
    tm = TM_MOE

    def last(j, nt):
        return jnp.minimum(j, nt[0] - 1)

    def w_spec(shape, which):
        return pl.BlockSpec(
            (None,) + shape, lambda j, tb, nt, tab: (tab[which * N_BUCKETS + tb[last(j, nt)]], 0, 0))

    up = (D_MODEL, D_EXPERT)
    down = (D_EXPERT, D_MODEL)
    grid_spec = pltpu.PrefetchScalarGridSpec(
        num_scalar_prefetch=3,
        grid=(NT_MOE,),
        in_specs=[
            pl.BlockSpec((tm * SLAB, 128), lambda j, tb, nt, tab: (last(j, nt), 0)),
            pl.BlockSpec((D_MODEL, 128), lambda j, tb, nt, tab: (0, 0)),
            pl.BlockSpec((1, 128), lambda j, tb, nt, tab: (0, 0)),
            w_spec(up, 0), w_spec(up, 0), w_spec(down, 0),
            w_spec(up, 1), w_spec(up, 1), w_spec(down, 1),
        ],
        out_specs=pl.BlockSpec((tm * SLAB, 128), lambda j, tb, nt, tab: (j, 0)),
    )
    return pl.pallas_call(
        _moe_kernel,
        grid_spec=grid_spec,
        out_shape=jax.ShapeDtypeStruct((SORTED_ROWS * SLAB, 128), F32),
        compiler_params=_params(1),
        name="moe",
    )(tile_bucket, n_tiles, tables, h_sorted, wr, br_row, w_gate, w_up, w_down, w_gate, w_up, w_down)


def _alibi_query_lanes():
    lanes = np.zeros((1, N_HEADS * HEAD_SLOT), np.float32)
    for hd, slope in enumerate(_alibi_slopes()):
        rest = np.float32(slope * LOG2E)
        for part in range(ALIBI_PARTS):
            piece = np.float32(np.asarray(rest).astype(jnp.bfloat16))
            lanes[0, hd * HEAD_SLOT + HEAD_DIM + part] = piece
            rest = np.float32(rest - piece)
    return jnp.asarray(lanes)


def _bucket_tables():
    e_lo, e_hi, grp = [], [], []
    for g in range(N_EXPERT_GROUPS):
        for lo, hi in PAIRS:
            e_lo.append(g * EXPERTS_PER_GROUP + lo)
            e_hi.append(g * EXPERTS_PER_GROUP + hi)
            grp.append(g)
    return jnp.asarray(e_lo + e_hi + grp, I32)


def kernel(x, c, w_ada, b_ada, norm1_gain, w_in, b_branch_gate, q_norm_gain, k_norm_gain, attn_sinks, gmlp_norm_gain, gmlp_norm_bias, gmlp_w_spatial, gmlp_b_spatial, w_o_attn, w_o_gmlp, w_out, norm2_gain, w_group_router, b_group_router, w_expert_router, b_expert_router, w_expert_gate, w_expert_up, w_expert_down):
    depth = w_ada.shape[0]
    x2 = x.reshape(TOKENS, D_MODEL)
    tables = _bucket_tables()
    q_aug = _alibi_query_lanes()
    row = lambda v: v.reshape(1, -1)
    for l in range(depth):
        mod = _mod_call(c, w_ada[l], b_ada[l])
        shift1, scale1, gate1, shift2, scale2, gate2 = [
            m.reshape(BATCH, 1, D_MODEL) for m in jnp.split(mod, 6, axis=-1)]

        slot_pad = jnp.zeros((HEAD_SLOT - HEAD_DIM,), F32)
        q_gain_row = row(jnp.tile(jnp.concatenate([q_norm_gain[l] * (HEAD_DIM ** -0.5 * LOG2E), slot_pad]), N_HEADS))
        k_gain_row = row(jnp.tile(jnp.concatenate([k_norm_gain[l], slot_pad]), N_KV_HEADS))
        q, k, vt, u, vg, gates = _inproj_call(
            x2, scale1, shift1, row(norm1_gain[l]), _round_call(w_in, l), w_in[l][:, V0:V1].T.astype(BF16),
            q_gain_row, k_gain_row, row(gmlp_norm_gain[l]), row(gmlp_norm_bias[l]), row(b_branch_gate[l]), q_aug)

        y_attn = _attn_call(attn_sinks[l], q, k, vt)

        bias_map = jnp.repeat(gmlp_b_spatial[l].T, GMLP_WIDTH // GMLP_GROUPS, axis=1)
        e0, e1 = EXPERT_ROW0, EXPERT_ROW0 + N_EXPERTS
        wr = jnp.zeros((D_MODEL, 128), F32)
        wr = wr.at[:, 0:N_EXPERT_GROUPS].set(w_group_router[l]).at[:, e0:e1].set(w_expert_router[l]).astype(BF16)
        br_row = jnp.zeros((1, 128), F32)
        br_row = br_row.at[0, 0:N_EXPERT_GROUPS].set(b_group_router[l]).at[0, e0:e1].set(b_expert_router[l])
        x1, h2p, buckets, totals = _mixer_call(
            x2, y_attn, u, vg, gates, gmlp_w_spatial[l], bias_map, _round_call(w_o_attn, l),
            _round_call(w_o_gmlp, l), _round_call(w_out, l), gate1, row(norm2_gain[l]), scale2, shift2,
            wr[:, 0:BUCKET_ROWS].T, br_row[:, 0:BUCKET_ROWS].T)

        pos, meta = _rank_call(buckets, totals)
        h_sorted = _scatter_call(pos, meta, h2p)

        flat = lambda w: w.reshape((N_EXPERTS,) + w.shape[2:]).astype(BF16)
        y_sorted = _moe_call(
            meta[META_TILE_BUCKET], meta[META_N_TILES, 0:1], tables, h_sorted, wr, br_row,
            flat(w_expert_gate[l]), flat(w_expert_up[l]), flat(w_expert_down[l]))

        x2 = _gather_call(pos, y_sorted, x1, gate2)
    return x2.reshape(x.shape)
```

```python
import functools

import jax
import jax.numpy as jnp
import numpy as np
from jax import lax
from jax.experimental import pallas as pl
from jax.experimental.pallas import tpu as pltpu

D_MODEL = 1024
BATCH = 8
SEQ = 4096
TOKENS = BATCH * SEQ
N_HEADS = 16
N_KV_HEADS = 4
HEAD_DIM = 64
Q_PER_KV = N_HEADS // N_KV_HEADS
BLOCK = 128
HEAD_SLOT = 128
ALIBI_PARTS = 3
ATTN_WIDTH = N_HEADS * HEAD_DIM
KV_WIDTH = N_KV_HEADS * HEAD_DIM
GMLP_WIDTH = 1024
GMLP_GROUPS = 8
GMLP_CHUNK = 128
N_EXPERT_GROUPS = 4
EXPERTS_PER_GROUP = 4
N_EXPERTS = N_EXPERT_GROUPS * EXPERTS_PER_GROUP
D_EXPERT = 512
IN_WIDTH = ATTN_WIDTH + 2 * KV_WIDTH + 2 * GMLP_WIDTH + 2 * D_MODEL
EPS = 1e-6
NEG_INF = -1e30
LOG2E = float(np.float32(np.log2(np.e)))

Q0, Q1 = 0, ATTN_WIDTH
K0, K1 = Q1, Q1 + KV_WIDTH
V0, V1 = K1, K1 + KV_WIDTH
U0, U1 = V1, V1 + GMLP_WIDTH
G0, G1 = U1, U1 + GMLP_WIDTH
B0, B1 = G1, G1 + 2 * D_MODEL

PAIRS = ((0, 1), (0, 2), (0, 3), (1, 2), (1, 3), (2, 3))
N_BUCKETS = N_EXPERT_GROUPS * len(PAIRS)
BUCKET_ROWS = 32
EXPERT_ROW0 = 8

TM_PROJ = 512
PROJ_CHUNK = 512
TM_ATTN = 512
TM_MIX = 512
TM_MOE = 256
TM_ROW = 512
SCORES_AHEAD = 6
ROW_UNROLL = 16
NT_MOE = TOKENS // TM_MOE + N_BUCKETS
SORTED_ROWS = NT_MOE * TM_MOE
SLAB = D_MODEL // 128
META_LANES = 256
assert NT_MOE <= META_LANES

VMEM_LIMIT = 56 * 1024 * 1024

F32 = jnp.float32
BF16 = jnp.bfloat16
U32 = jnp.uint32
I32 = jnp.int32


def _alibi_slopes():
    return [float(np.float32(2.0 ** (-8.0 * (i + 1) / N_HEADS))) for i in range(N_HEADS)]


def _params(n_axes):
    return pltpu.CompilerParams(dimension_semantics=("arbitrary",) * n_axes, vmem_limit_bytes=VMEM_LIMIT)


def _sigmoid(x):
    return 1.0 / (1.0 + jnp.exp(-x))


def _gelu(x):
    return 0.5 * x * (1.0 + lax.erf(x * np.float32(1.0 / np.sqrt(2.0))))


def _dot(a, b):
    return jnp.dot(a, b, preferred_element_type=F32)


def _dot_nt(a, b):
    return lax.dot_general(a, b, (((1,), (1,)), ((), ())), preferred_element_type=F32)


def _mod_kernel(c_ref, w_ref, b_ref, o_ref):
    c = c_ref[...]
    ca = c * _sigmoid(c)
    o_ref[...] = _dot(ca.astype(BF16), w_ref[...].astype(BF16)) + b_ref[...]


def _mod_call(c, w_ada, b_ada):
    n = w_ada.shape[1]
    bn = 1536
    return pl.pallas_call(
        _mod_kernel,
        grid=(n // bn,),
        in_specs=[
            pl.BlockSpec((BATCH, D_MODEL), lambda j: (0, 0)),
            pl.BlockSpec((D_MODEL, bn), lambda j: (0, j)),
            pl.BlockSpec((1, bn), lambda j: (0, j)),
        ],
        out_specs=pl.BlockSpec((BATCH, bn), lambda j: (0, j)),
        out_shape=jax.ShapeDtypeStruct((BATCH, n), F32),
        compiler_params=_params(1),
        name="mod",
    )(c, w_ada, b_ada.reshape(1, n))


def _round_kernel(w_ref, o_ref):
    o_ref[...] = w_ref[...].astype(BF16)


def _round_call(w, layer):
    _, k, n = w.shape
    bn = 512
    return pl.pallas_call(
        _round_kernel,
        grid=(n // bn,),
        in_specs=[pl.BlockSpec((None, k, bn), lambda j: (layer, 0, j))],
        out_specs=pl.BlockSpec((k, bn), lambda j: (0, j)),
        out_shape=jax.ShapeDtypeStruct((k, n), BF16),
        compiler_params=_params(1),
        name="round_weight",
    )(w)


def _inproj_kernel(x_ref, sc_ref, sh_ref, ng_ref, w_ref, wvt_ref, qg_ref, kg_ref, lng_ref, lnb_ref, bg_ref, qaug_ref,
                   q_o, k_o, vt_o, u_o, vg_o, gt_o):
    x = x_ref[...]
    ms = jnp.mean(x * x, axis=-1, keepdims=True)
    h = (x * lax.rsqrt(ms + EPS)) * ng_ref[...]
    h = h * (1.0 + sc_ref[0]) + sh_ref[0]
    hb = h.astype(BF16)

    def proj(c0, c1):
        return _dot(hb, w_ref[:, c0:c1])

    low_half = lax.broadcasted_iota(I32, (1, HEAD_SLOT), 1) < HEAD_DIM

    def store_normed_heads(raw, gain_ref, spare_ref, o_ref):
        for p in range(raw.shape[1] // HEAD_SLOT):
            pair = raw[:, p * HEAD_SLOT:(p + 1) * HEAD_SLOT]
            for hd, head in ((2 * p, pair), (2 * p + 1, pltpu.roll(pair, HEAD_DIM, 1))):
                cols = slice(hd * HEAD_SLOT, (hd + 1) * HEAD_SLOT)
                xh = jnp.where(low_half, head, 0.0)
                r = lax.rsqrt(jnp.sum(xh * xh, axis=-1, keepdims=True) * (1.0 / HEAD_DIM) + EPS)
                y = xh * r * gain_ref[:, cols]
                if spare_ref is not None:
                    y = y + spare_ref[:, cols]
                o_ref[:, cols] = y.astype(BF16)

    store_normed_heads(proj(Q0, Q1), qg_ref, qaug_ref, q_o)
    store_normed_heads(proj(K0, K1), kg_ref, None, k_o)
    vt_o[...] = _dot_nt(wvt_ref[...], hb).astype(BF16)

    cw = PROJ_CHUNK

    def gate_chunk(c):
        cols = slice(c * cw, (c + 1) * cw)
        gt_o[:, cols] = _sigmoid(proj(B0 + c * cw, B0 + (c + 1) * cw) + bg_ref[:, cols]).astype(BF16)

    vg_parts = []
    for c in range(GMLP_WIDTH // cw):
        u_o[:, c * cw:(c + 1) * cw] = _gelu(proj(U0 + c * cw, U0 + (c + 1) * cw)).astype(BF16)
        gate_chunk(2 * c)
        vg_parts.append(_gelu(proj(G0 + c * cw, G0 + (c + 1) * cw)))
        gate_chunk(2 * c + 1)
    vg = jnp.concatenate(vg_parts, axis=1)
    mu = jnp.mean(vg, axis=-1, keepdims=True)
    vc = vg - mu
    var = jnp.mean(vc * vc, axis=-1, keepdims=True)
    vg_o[...] = (vc * lax.rsqrt(var + EPS) * lng_ref[...] + lnb_ref[...]).astype(BF16)


def _inproj_call(x2, scale1, shift1, norm_gain, w_in, wv_t, q_gain_row, k_gain_row, ln_gain, ln_bias, b_gate, q_aug):
    tm = TM_PROJ
    per_batch = SEQ // tm
    row = lambda n: pl.BlockSpec((1, n), lambda i: (0, 0))
    modrow = pl.BlockSpec((1, 1, D_MODEL), lambda i: (i // per_batch, 0, 0))
    tile = lambda n: pl.BlockSpec((tm, n), lambda i: (i, 0))
    out = lambda n: jax.ShapeDtypeStruct((TOKENS, n), BF16)
    return pl.pallas_call(
        _inproj_kernel,
        grid=(TOKENS // tm,),
        in_specs=[
            tile(D_MODEL), modrow, modrow, row(D_MODEL),
            pl.BlockSpec((D_MODEL, IN_WIDTH), lambda i: (0, 0)),
            pl.BlockSpec((KV_WIDTH, D_MODEL), lambda i: (0, 0)),
            row(N_HEADS * HEAD_SLOT), row(N_KV_HEADS * HEAD_SLOT), row(GMLP_WIDTH), row(GMLP_WIDTH),
            row(2 * D_MODEL), row(N_HEADS * HEAD_SLOT),
        ],
        out_specs=[tile(N_HEADS * HEAD_SLOT), tile(N_KV_HEADS * HEAD_SLOT),
                   pl.BlockSpec((KV_WIDTH, tm), lambda i: (0, i)),
                   tile(GMLP_WIDTH), tile(GMLP_WIDTH), tile(2 * D_MODEL)],
        out_shape=[out(N_HEADS * HEAD_SLOT), out(N_KV_HEADS * HEAD_SLOT),
                   jax.ShapeDtypeStruct((KV_WIDTH, TOKENS), BF16),
                   out(GMLP_WIDTH), out(GMLP_WIDTH), out(2 * D_MODEL)],
        compiler_params=_params(1),
        name="inproj",
    )(x2, scale1, shift1, norm_gain, w_in, wv_t, q_gain_row, k_gain_row, ln_gain, ln_bias, b_gate, q_aug)


def _attn_kernel(sink_ref, q_ref, kc_ref, kp_ref, vc_ref, vp_ref, o_ref):
    i = pl.program_id(1)
    slopes = _alibi_slopes()
    kj = lax.broadcasted_iota(I32, (BLOCK, 2 * BLOCK), 0)
    qi = lax.broadcasted_iota(I32, (BLOCK, 2 * BLOCK), 1) & (BLOCK - 1)
    own = kj <= qi
    from_own = jnp.where(own, 1.0, 0.0).astype(BF16)
    from_prev = jnp.where(own, 0.0, 1.0).astype(BF16)
    has_prev = i > 0
    low_half = lax.broadcasted_iota(I32, (1, HEAD_SLOT), 1) < HEAD_DIM
    k_lane = lax.broadcasted_iota(I32, (2 * BLOCK, HEAD_SLOT), 1)
    k_aug = jnp.where((k_lane >= HEAD_DIM) & (k_lane < HEAD_DIM + ALIBI_PARTS),
                      lax.broadcasted_iota(I32, (2 * BLOCK, HEAD_SLOT), 0), 0).astype(F32).astype(BF16)
    q_pos = (lax.broadcasted_iota(I32, (1, BLOCK), 1) + BLOCK).astype(F32)

    keys_of = {}

    def scores(item):
        sb, kv, pr = item
        r0, r1 = sb * BLOCK, (sb + 1) * BLOCK
        if (sb, kv) not in keys_of:
            cols = slice(kv * HEAD_SLOT, (kv + 1) * HEAD_SLOT)
            k_prev = kp_ref[:, cols] if sb == 0 else kc_ref[r0 - BLOCK:r0, cols]
            keys_of[sb, kv] = jnp.where(low_half, jnp.concatenate([k_prev, kc_ref[r0:r1, cols]], axis=0), k_aug)
        ha = kv * Q_PER_KV + 2 * pr
        queries = jnp.concatenate([q_ref[r0:r1, ha * HEAD_SLOT:(ha + 1) * HEAD_SLOT],
                                   q_ref[r0:r1, (ha + 1) * HEAD_SLOT:(ha + 2) * HEAD_SLOT]], axis=0)
        s = _dot_nt(keys_of[sb, kv], queries)
        s_prev = s[0:BLOCK, :]
        if sb == 0:
            s_prev = jnp.where(has_prev, s_prev, NEG_INF)
        return jnp.where(own, s[BLOCK:2 * BLOCK, :], s_prev)

    def attend(item, s):
        sb, kv, pr = item
        r0, r1 = sb * BLOCK, (sb + 1) * BLOCK
        rows = slice(kv * HEAD_DIM, (kv + 1) * HEAD_DIM)
        v_prev = vp_ref[rows, :] if sb == 0 else vc_ref[rows, r0 - BLOCK:r0]
        vt = jnp.concatenate([v_prev, vc_ref[rows, r0:r1]], axis=1)
        ha = kv * Q_PER_KV + 2 * pr
        hb = ha + 1
        sink = LOG2E * jnp.concatenate(
            [sink_ref[ha] + slopes[ha] * q_pos, sink_ref[hb] + slopes[hb] * q_pos], axis=1)
        m = jnp.maximum(jnp.max(s, axis=0, keepdims=True), sink)
        p = jnp.exp2(s - m)
        den = jnp.sum(p, axis=0, keepdims=True) + jnp.exp2(sink - m)
        pb = p.astype(BF16)
        p_keys = jnp.concatenate([pb * from_prev, pb * from_own], axis=0)
        o = _dot(vt, p_keys) / den
        pair = jnp.concatenate([o[:, 0:BLOCK], o[:, BLOCK:2 * BLOCK]], axis=0).T
        o_ref[r0:r1, ha * HEAD_DIM:(hb + 1) * HEAD_DIM] = pair.astype(BF16)

    items = [(sb, kv, pr) for sb in range(TM_ATTN // BLOCK) for kv in range(N_KV_HEADS)
             for pr in range(Q_PER_KV // 2)]
    pending = []
    for n in range(len(items) + SCORES_AHEAD):
        if n < len(items):
            pending.append(scores(items[n]))
        if n >= SCORES_AHEAD:
            attend(items[n - SCORES_AHEAD], pending.pop(0))


def _attn_call(sinks, q, k, vt):
    tq = TM_ATTN
    per_batch = SEQ // tq
    blocks_per_tile = tq // BLOCK
    q_width = N_HEADS * HEAD_SLOT
    k_width = N_KV_HEADS * HEAD_SLOT
    cur = lambda n: pl.BlockSpec((tq, n), lambda b, i: (b * per_batch + i, 0))
    prev_block = lambda b, i: b * (SEQ // BLOCK) + jnp.maximum(i * blocks_per_tile - 1, 0)
    return pl.pallas_call(
        _attn_kernel,
        grid=(BATCH, per_batch),
        in_specs=[
            pl.BlockSpec(memory_space=pltpu.SMEM),
            cur(q_width), cur(k_width),
            pl.BlockSpec((BLOCK, k_width), lambda b, i: (prev_block(b, i), 0)),
            pl.BlockSpec((KV_WIDTH, tq), lambda b, i: (0, b * per_batch + i)),
            pl.BlockSpec((KV_WIDTH, BLOCK), lambda b, i: (0, prev_block(b, i))),
        ],
        out_specs=cur(ATTN_WIDTH),
        out_shape=jax.ShapeDtypeStruct((TOKENS, ATTN_WIDTH), BF16),
        compiler_params=_params(2),
        name="attn",
    )(sinks, q, k, k, vt, vt)


def _first_argmax(vals):
    m = vals[0]
    for v in vals[1:]:
        m = jnp.maximum(m, v)
    idx = jnp.full(m.shape, len(vals) - 1, I32)
    for k in range(len(vals) - 2, -1, -1):
        idx = jnp.where(vals[k] == m, k, idx)
    return m, idx


def _mixer_kernel(x_ref, ya_ref, u_ref, vg_ref, gt_ref, ws_ref, bmap_ref, woa_ref, wog_ref, wout_ref, gate1_ref,
                  ng_ref, sc_ref, sh_ref, wrt_ref, brt_ref, x1_o, h2_o, bk_o, cnt_o, cnt):
    tm = TM_MIX
    t_idx = lax.broadcasted_iota(I32, (GMLP_CHUNK, GMLP_CHUNK), 0)
    s_idx = lax.broadcasted_iota(I32, (GMLP_CHUNK, GMLP_CHUNK), 1)
    ws = [jnp.where(t_idx >= s_idx, ws_ref[g], 0.0).astype(BF16) for g in range(GMLP_GROUPS)]
    gc = GMLP_WIDTH // GMLP_GROUPS
    chunks = []
    for c in range(tm // GMLP_CHUNK):
        r0, r1 = c * GMLP_CHUNK, (c + 1) * GMLP_CHUNK
        mixed = jnp.concatenate(
            [_dot(ws[g], vg_ref[r0:r1, g * gc:(g + 1) * gc]) for g in range(GMLP_GROUPS)], axis=1)
        mixed = mixed + bmap_ref[...]
        chunks.append((u_ref[r0:r1, :].astype(F32) * mixed).astype(BF16))
    y_gmlp = jnp.concatenate(chunks, axis=0)
    pa = _dot(ya_ref[...], woa_ref[...])
    pg = _dot(y_gmlp, wog_ref[...])
    merged = gt_ref[:, 0:D_MODEL].astype(F32) * pa + gt_ref[:, D_MODEL:2 * D_MODEL].astype(F32) * pg
    x1 = x_ref[...] + gate1_ref[0] * _dot(merged.astype(BF16), wout_ref[...])
    x1_o[...] = x1

    ms = jnp.mean(x1 * x1, axis=-1, keepdims=True)
    h2 = (x1 * lax.rsqrt(ms + EPS)) * ng_ref[...]
    h2 = h2 * (1.0 + sc_ref[0]) + sh_ref[0]
    _store_slabs(h2_o, h2)

    lg = _dot_nt(wrt_ref[...], h2.astype(BF16)) + brt_ref[...]
    _, gi = _first_argmax([lg[r:r + 1, :] for r in range(N_EXPERT_GROUPS)])
    el = [lg[EXPERT_ROW0 + r:EXPERT_ROW0 + r + 1, :] for r in range(N_EXPERTS)]
    eg = []
    for k in range(EXPERTS_PER_GROUP):
        v = el[(N_EXPERT_GROUPS - 1) * EXPERTS_PER_GROUP + k]
        for g in range(N_EXPERT_GROUPS - 2, -1, -1):
            v = jnp.where(gi == g, el[g * EXPERTS_PER_GROUP + k], v)
        eg.append(v)
    _, i1 = _first_argmax(eg)
    _, i2 = _first_argmax([jnp.where(i1 == k, -3e38, eg[k]) for k in range(EXPERTS_PER_GROUP)])
    lo = jnp.minimum(i1, i2)
    hi = jnp.maximum(i1, i2)
    pair = jnp.where(lo == 0, hi - 1, jnp.where(lo == 1, hi + 1, 5))
    bucket = gi * len(PAIRS) + pair
    bk_o[0] = bucket

    @pl.when(pl.program_id(0) == 0)
    def _():
        cnt[...] = jnp.zeros_like(cnt)
    in_bucket = lax.broadcasted_iota(I32, (BUCKET_ROWS, tm), 0) == bucket
    cnt[...] += jnp.sum(jnp.where(in_bucket, 1.0, 0.0), axis=1, keepdims=True)
    cnt_o[...] = cnt[...]


def _mixer_call(x2, y_attn, u, vg, gates, w_spatial, bias_map, wo_attn, wo_gmlp, w_out, gate1, norm_gain, scale2, shift2,
                wr_t, br_t):
    tm = TM_MIX
    per_batch = SEQ // tm
    nt = TOKENS // tm
    tile = lambda n: pl.BlockSpec((tm, n), lambda i: (i, 0))
    full = lambda shape: pl.BlockSpec(shape, lambda i: (0,) * len(shape))
    modrow = pl.BlockSpec((1, 1, D_MODEL), lambda i: (i // per_batch, 0, 0))
    return pl.pallas_call(
        _mixer_kernel,
        grid=(nt,),
        in_specs=[
            tile(D_MODEL), tile(ATTN_WIDTH), tile(GMLP_WIDTH), tile(GMLP_WIDTH), tile(2 * D_MODEL),
            full((GMLP_GROUPS, GMLP_CHUNK, GMLP_CHUNK)), full((GMLP_CHUNK, GMLP_WIDTH)),
            full((ATTN_WIDTH, D_MODEL)), full((GMLP_WIDTH, D_MODEL)), full((D_MODEL, D_MODEL)),
            modrow, full((1, D_MODEL)), modrow, modrow,
            full((BUCKET_ROWS, D_MODEL)), full((BUCKET_ROWS, 1)),
        ],
        out_specs=[tile(D_MODEL), pl.BlockSpec((tm * SLAB, 128), lambda i: (i, 0)),
                   pl.BlockSpec((1, 1, tm), lambda i: (i, 0, 0)), full((BUCKET_ROWS, 128))],
        out_shape=[
            jax.ShapeDtypeStruct((TOKENS, D_MODEL), F32),
            jax.ShapeDtypeStruct((TOKENS * SLAB, 128), F32),
            jax.ShapeDtypeStruct((nt, 1, tm), I32),
            jax.ShapeDtypeStruct((BUCKET_ROWS, 128), F32),
        ],
        scratch_shapes=[pltpu.VMEM((BUCKET_ROWS, 128), F32)],
        compiler_params=_params(1),
        name="mixer",
    )(x2, y_attn, u, vg, gates, w_spatial, bias_map, wo_attn, wo_gmlp, w_out, gate1, norm_gain, scale2, shift2,
      wr_t, br_t)


META_TILE_BUCKET, META_N_TILES, META_PAD_START, META_PAD_END = 0, 1, 2, 3


def _rank_kernel(bk_ref, total_ref, pos_o, meta_o, seen, off):
    tm = TM_MIX
    i = pl.program_id(0)
    bk = bk_ref[0]
    onehot = lax.broadcasted_iota(I32, (BUCKET_ROWS, tm), 0) == bk

    @pl.when(i == 0)
    def _():
        total = total_ref[...]
        tiles = jnp.floor((total + (TM_MOE - 1)) * (1.0 / TM_MOE))
        r = lax.broadcasted_iota(I32, (BUCKET_ROWS, BUCKET_ROWS), 0)
        c = lax.broadcasted_iota(I32, (BUCKET_ROWS, BUCKET_ROWS), 1)
        before = jnp.where(c < r, 1.0, 0.0).astype(BF16)
        first_tile = _dot(before, tiles.astype(BF16))
        first_row = first_tile * TM_MOE
        off[...] = first_row
        seen[...] = jnp.zeros_like(seen)
        end_tile = (first_tile + tiles)[:, 0:1]
        lane = lax.broadcasted_iota(I32, (BUCKET_ROWS, META_LANES), 1)
        bsub = lax.broadcasted_iota(I32, (BUCKET_ROWS, META_LANES), 0)
        is_bucket = bsub < N_BUCKETS
        tile_bucket = jnp.sum(jnp.where((lane.astype(F32) >= end_tile) & is_bucket, 1.0, 0.0), axis=0, keepdims=True)
        as_row = lambda col: jnp.sum(jnp.where((bsub == lane) & is_bucket, col, 0.0), axis=0, keepdims=True)
        n_tiles = jnp.sum(jnp.where(bsub == N_BUCKETS - 1, end_tile, 0.0), axis=0, keepdims=True)
        pad_start = as_row(end_tile - jnp.minimum(tiles[:, 0:1], 1.0))
        pad_end = as_row(end_tile)
        row = lax.broadcasted_iota(I32, (8, META_LANES), 0)
        meta = jnp.zeros((8, META_LANES), F32)
        for k, v in ((META_TILE_BUCKET, tile_bucket), (META_N_TILES, n_tiles), (META_PAD_START, pad_start),
                     (META_PAD_END, pad_end)):
            meta = jnp.where(row == k, v, meta)
        meta_o[...] = meta.astype(I32)

    s_idx = lax.broadcasted_iota(I32, (tm, tm), 0)
    t_idx = lax.broadcasted_iota(I32, (tm, tm), 1)
    upto = jnp.where(s_idx <= t_idx, 1.0, 0.0).astype(BF16)
    incl = _dot(jnp.where(onehot, 1.0, 0.0).astype(BF16), upto)
    base = off[:, 0:1] + seen[:, 0:1]
    posf = jnp.sum(jnp.where(onehot, base + incl - 1.0, 0.0), axis=0, keepdims=True)
    pos_o[0] = posf.astype(I32)
    seen[...] += incl[:, tm - 1:tm]


def _rank_call(buckets, totals):
    nt, _, tm = buckets.shape
    return pl.pallas_call(
        _rank_kernel,
        grid=(nt,),
        in_specs=[pl.BlockSpec((1, 1, tm), lambda i: (i, 0, 0)),
                  pl.BlockSpec((BUCKET_ROWS, 128), lambda i: (0, 0))],
        out_specs=[
            pl.BlockSpec((1, 1, tm), lambda i: (i, 0, 0)),
            pl.BlockSpec((8, META_LANES), lambda i: (0, 0)),
        ],
        out_shape=[
            jax.ShapeDtypeStruct((nt, 1, tm), I32),
            jax.ShapeDtypeStruct((8, META_LANES), I32),
        ],
        scratch_shapes=[pltpu.VMEM((BUCKET_ROWS, 128), F32), pltpu.VMEM((BUCKET_ROWS, 128), F32)],
        compiler_params=_params(1),
        name="rank",
    )(buckets, totals)


def _store_slabs(ref, x):
    n = x.shape[0]
    for s in range(SLAB):
        ref[pl.ds(s, n, stride=SLAB), :] = x[:, s * 128:(s + 1) * 128]


def _load_slabs(ref, n):
    return jnp.concatenate([ref[pl.ds(s, n, stride=SLAB), :] for s in range(SLAB)], axis=1)


def _row_copy(src, src_row, dst, dst_row, sem):
    return pltpu.make_async_copy(src.at[pl.ds(pl.multiple_of(src_row * SLAB, SLAB), SLAB)],
                                 dst.at[pl.ds(pl.multiple_of(dst_row * SLAB, SLAB), SLAB)], sem)


def _start_tile_rows(copy_of_row):
    def trip(c, carry):
        for k in range(ROW_UNROLL):
            copy_of_row(c * ROW_UNROLL + k).start(priority=k % 2)
        return carry
    lax.fori_loop(0, TM_ROW // ROW_UNROLL, trip, 0)


def _tile_wait(src, dst, sem):
    pltpu.make_async_copy(src.at[pl.ds(0, TM_ROW * SLAB)], dst, sem).wait()


def _scatter_kernel(pos_ref, meta_ref, h_ref, o_ref, zero_tile, sem, pad_sem):
    @pl.when(pl.program_id(0) == 0)
    def _():
        zero_tile[...] = jnp.zeros_like(zero_tile)

        def tile_copy(j):
            rows = pl.ds(pl.multiple_of(j * (TM_MOE * SLAB), TM_MOE * SLAB), TM_MOE * SLAB)
            return pltpu.make_async_copy(zero_tile, o_ref.at[rows], pad_sem)

        def for_all_zero_copies(act):
            def per_bucket(b, carry):
                lo, hi = meta_ref[META_PAD_START, b], meta_ref[META_PAD_END, b]
                return lax.fori_loop(lo, hi, lambda j, c: (act(tile_copy(j)), c)[1], carry)
            lax.fori_loop(0, N_BUCKETS, per_bucket, 0)
            lax.fori_loop(meta_ref[META_N_TILES, 0], NT_MOE, lambda j, c: (act(tile_copy(j)), c)[1], 0)

        for_all_zero_copies(lambda cp: cp.start())
        for_all_zero_copies(lambda cp: cp.wait())

    _start_tile_rows(lambda r: _row_copy(h_ref, r, o_ref, pos_ref[0, 0, r], sem))
    _tile_wait(h_ref, o_ref.at[pl.ds(0, TM_ROW * SLAB)], sem)


def _scatter_call(pos, meta, h2):
    tm = TM_ROW
    return pl.pallas_call(
        _scatter_kernel,
        grid=(TOKENS // tm,),
        in_specs=[
            pl.BlockSpec((1, 1, tm), lambda i: (i, 0, 0), memory_space=pltpu.SMEM),
            pl.BlockSpec(memory_space=pltpu.SMEM),
            pl.BlockSpec((tm * SLAB, 128), lambda i: (i, 0)),
        ],
        out_specs=pl.BlockSpec(memory_space=pl.ANY),
        out_shape=jax.ShapeDtypeStruct((SORTED_ROWS * SLAB, 128), F32),
        scratch_shapes=[pltpu.VMEM((TM_MOE * SLAB, 128), F32), pltpu.SemaphoreType.DMA(()),
                        pltpu.SemaphoreType.DMA(())],
        compiler_params=_params(1),
        name="scatter",
    )(pos, meta, h2)


def _gather_kernel(pos_ref, pos_next_ref, ys_ref, x1_ref, gate2_ref, o_ref, buf, sem):
    i = pl.program_id(0)
    n = pl.num_programs(0)
    slot = lax.rem(i, 2)

    def fetch(p_ref, s):
        _start_tile_rows(lambda r: _row_copy(ys_ref, p_ref[0, 0, r], buf.at[s], r, sem.at[s]))

    @pl.when(i == 0)
    def _():
        fetch(pos_ref, 0)

    @pl.when(i + 1 < n)
    def _():
        fetch(pos_next_ref, 1 - slot)

    _tile_wait(ys_ref, buf.at[slot], sem.at[slot])
    o_ref[...] = x1_ref[...] + gate2_ref[0] * _load_slabs(buf.at[slot], TM_ROW)


def _gather_call(pos, y_sorted, x1, gate2):
    tm = TM_ROW
    per_batch = SEQ // tm
    n = TOKENS // tm
    return pl.pallas_call(
        _gather_kernel,
        grid=(n,),
        in_specs=[
            pl.BlockSpec((1, 1, tm), lambda i: (i, 0, 0), memory_space=pltpu.SMEM),
            pl.BlockSpec((1, 1, tm), lambda i: (jnp.minimum(i + 1, n - 1), 0, 0), memory_space=pltpu.SMEM),
            pl.BlockSpec(memory_space=pl.ANY),
            pl.BlockSpec((tm, D_MODEL), lambda i: (i, 0)),
            pl.BlockSpec((1, 1, D_MODEL), lambda i: (i // per_batch, 0, 0)),
        ],
        out_specs=pl.BlockSpec((tm, D_MODEL), lambda i: (i, 0)),
        out_shape=jax.ShapeDtypeStruct((TOKENS, D_MODEL), F32),
        scratch_shapes=[pltpu.VMEM((2, tm * SLAB, 128), F32), pltpu.SemaphoreType.DMA((2,))],
        compiler_params=_params(1),
        name="gather",
    )(pos, pos, y_sorted, x1, gate2)


def _moe_kernel(tb_ref, nt_ref, tab_ref, h_ref, wr_ref, brow_ref, wg_lo, wu_lo, wd_lo, wg_hi, wu_hi, wd_hi, o_ref):
    j = pl.program_id(0)

    @pl.when(j < nt_ref[0])
    def _():
        b = tb_ref[j]
        e_lo = tab_ref[b]
        e_hi = tab_ref[N_BUCKETS + b]
        g = tab_ref[2 * N_BUCKETS + b]
        h = _load_slabs(h_ref, TM_MOE).astype(BF16)
        lg = _dot(h, wr_ref[...]) + brow_ref[...]
        lane = lax.broadcasted_iota(I32, lg.shape, 1)
        is_group = lane < N_EXPERT_GROUPS
        gmax = jnp.max(jnp.where(is_group, lg, NEG_INF), axis=1, keepdims=True)
        ge = jnp.exp(lg - gmax)
        g_w = (jnp.sum(jnp.where(lane == g, ge, 0.0), axis=1, keepdims=True)
               / jnp.sum(jnp.where(is_group, ge, 0.0), axis=1, keepdims=True))
        v_lo = jnp.sum(jnp.where(lane == EXPERT_ROW0 + e_lo, lg, 0.0), axis=1, keepdims=True)
        v_hi = jnp.sum(jnp.where(lane == EXPERT_ROW0 + e_hi, lg, 0.0), axis=1, keepdims=True)
        m = jnp.maximum(v_lo, v_hi)
        x_lo = jnp.exp(v_lo - m)
        x_hi = jnp.exp(v_hi - m)
        w_lo = x_lo / (x_lo + x_hi) * g_w
        w_hi = x_hi / (x_lo + x_hi) * g_w

        def hidden(wg, wu, w):
            a = _dot(h, wg[...])
            return (a * _sigmoid(a) * _dot(h, wu[...]) * w).astype(BF16)

        y = _dot(hidden(wg_lo, wu_lo, w_lo), wd_lo[...]) + _dot(hidden(wg_hi, wu_hi, w_hi), wd_hi[...])
        _store_slabs(o_ref, y)

    @pl.when(j >= nt_ref[0])
    def _():
        o_ref[...] = jnp.zeros_like(o_ref)


def _moe_call(tile_bucket, n_tiles, tables, h_sorted, wr, br_row, w_gate, w_up, w_down):
    tm = TM_MOE

    def last(j, nt):
        return jnp.minimum(j, nt[0] - 1)

    def w_spec(shape, which):
        return pl.BlockSpec(
            (None,) + shape, lambda j, tb, nt, tab: (tab[which * N_BUCKETS + tb[last(j, nt)]], 0, 0))

    up = (D_MODEL, D_EXPERT)
    down = (D_EXPERT, D_MODEL)
    grid_spec = pltpu.PrefetchScalarGridSpec(
        num_scalar_prefetch=3,
        grid=(NT_MOE,),
        in_specs=[
            pl.BlockSpec((tm * SLAB, 128), lambda j, tb, nt, tab: (last(j, nt), 0)),
            pl.BlockSpec((D_MODEL, 128), lambda j, tb, nt, tab: (0, 0)),
            pl.BlockSpec((1, 128), lambda j, tb, nt, tab: (0, 0)),
            w_spec(up, 0), w_spec(up, 0), w_spec(down, 0),
            w_spec(up, 1), w_spec(up, 1), w_spec(down, 1),
        ],
        out_specs=pl.BlockSpec((tm * SLAB, 128), lambda j, tb, nt, tab: (j, 0)),
    )
    return pl.pallas_call(
        _moe_kernel,
        grid_spec=grid_spec,
        out_shape=jax.ShapeDtypeStruct((SORTED_ROWS * SLAB, 128), F32),
        compiler_params=_params(1),
        name="moe",
    )(tile_bucket, n_tiles, tables, h_sorted, wr, br_row, w_gate, w_up, w_down, w_gate, w_up, w_down)


def _alibi_query_lanes():
    lanes = np.zeros((1, N_HEADS * HEAD_SLOT), np.float32)
    for hd, slope in enumerate(_alibi_slopes()):
        rest = np.float32(slope * LOG2E)
        for part in range(ALIBI_PARTS):
            piece = np.float32(np.asarray(rest).astype(jnp.bfloat16))
            lanes[0, hd * HEAD_SLOT + HEAD_DIM + part] = piece
            rest = np.float32(rest - piece)
    return jnp.asarray(lanes)


def _bucket_tables():
    e_lo, e_hi, grp = [], [], []
    for g in range(N_EXPERT_GROUPS):
        for lo, hi in PAIRS:
            e_lo.append(g * EXPERTS_PER_GROUP + lo)
            e_hi.append(g * EXPERTS_PER_GROUP + hi)
            grp.append(g)
    return jnp.asarray(e_lo + e_hi + grp, I32)


def kernel(x, c, w_ada, b_ada, norm1_gain, w_in, b_branch_gate, q_norm_gain, k_norm_gain, attn_sinks, gmlp_norm_gain, gmlp_norm_bias, gmlp_w_spatial, gmlp_b_spatial, w_o_attn, w_o_gmlp, w_out, norm2_gain, w_group_router, b_group_router, w_expert_router, b_expert_router, w_expert_gate, w_expert_up, w_expert_down):
    depth = w_ada.shape[0]
    x2 = x.reshape(TOKENS, D_MODEL)
    tables = _bucket_tables()
    q_aug = _alibi_query_lanes()
    row = lambda v: v.reshape(1, -1)
    for l in range(depth):
        mod = _mod_call(c, w_ada[l], b_ada[l])
        shift1, scale1, gate1, shift2, scale2, gate2 = [
            m.reshape(BATCH, 1, D_MODEL) for m in jnp.split(mod, 6, axis=-1)]

        slot_pad = jnp.zeros((HEAD_SLOT - HEAD_DIM,), F32)
        q_gain_row = row(jnp.tile(jnp.concatenate([q_norm_gain[l] * (HEAD_DIM ** -0.5 * LOG2E), slot_pad]), N_HEADS))
        k_gain_row = row(jnp.tile(jnp.concatenate([k_norm_gain[l], slot_pad]), N_KV_HEADS))
        w_in_b = _round_call(w_in, l)
        q, k, vt, u, vg, gates = _inproj_call(
            x2, scale1, shift1, row(norm1_gain[l]), w_in_b, w_in_b[:, V0:V1].T,
            q_gain_row, k_gain_row, row(gmlp_norm_gain[l]), row(gmlp_norm_bias[l]), row(b_branch_gate[l]), q_aug)

        y_attn = _attn_call(attn_sinks[l], q, k, vt)

        bias_map = jnp.repeat(gmlp_b_spatial[l].T, GMLP_WIDTH // GMLP_GROUPS, axis=1)
        e0, e1 = EXPERT_ROW0, EXPERT_ROW0 + N_EXPERTS
        wr = jnp.zeros((D_MODEL, 128), F32)
        wr = wr.at[:, 0:N_EXPERT_GROUPS].set(w_group_router[l]).at[:, e0:e1].set(w_expert_router[l]).astype(BF16)
        br_row = jnp.zeros((1, 128), F32)
        br_row = br_row.at[0, 0:N_EXPERT_GROUPS].set(b_group_router[l]).at[0, e0:e1].set(b_expert_router[l])
        x1, h2p, buckets, totals = _mixer_call(
            x2, y_attn, u, vg, gates, gmlp_w_spatial[l], bias_map, _round_call(w_o_attn, l),
            _round_call(w_o_gmlp, l), _round_call(w_out, l), gate1, row(norm2_gain[l]), scale2, shift2,
            wr[:, 0:BUCKET_ROWS].T, br_row[:, 0:BUCKET_ROWS].T)

        pos, meta = _rank_call(buckets, totals)
        h_sorted = _scatter_call(pos, meta, h2p)

        flat = lambda w: w.reshape((N_EXPERTS,) + w.shape[2:]).astype(BF16)
        y_sorted = _moe_call(
            meta[META_TILE_BUCKET], meta[META_N_TILES, 0:1], tables, h_sorted, wr, br_row,
            flat(w_expert_gate[l]), flat(w_expert_up[l]), flat(w_expert_down[l]))

        x2 = _gather_call(pos, y_sorted, x1, gate2)
    return x2.reshape(x.shape)
```

```python
import functools

import jax
import jax.numpy as jnp
import numpy as np
from jax import lax
from jax.experimental import pallas as pl
from jax.experimental.pallas import tpu as pltpu

D_MODEL = 1024
BATCH = 8
SEQ = 4096
TOKENS = BATCH * SEQ
N_HEADS = 16
N_KV_HEADS = 4
HEAD_DIM = 64
Q_PER_KV = N_HEADS // N_KV_HEADS
BLOCK = 128
HEAD_SLOT = 128
ALIBI_PARTS = 3
ATTN_WIDTH = N_HEADS * HEAD_DIM
KV_WIDTH = N_KV_HEADS * HEAD_DIM
GMLP_WIDTH = 1024
GMLP_GROUPS = 8
GMLP_CHUNK = 128
N_EXPERT_GROUPS = 4
EXPERTS_PER_GROUP = 4
N_EXPERTS = N_EXPERT_GROUPS * EXPERTS_PER_GROUP
D_EXPERT = 512
IN_WIDTH = ATTN_WIDTH + 2 * KV_WIDTH + 2 * GMLP_WIDTH + 2 * D_MODEL
EPS = 1e-6
NEG_INF = -1e30
LOG2E = float(np.float32(np.log2(np.e)))

Q0, Q1 = 0, ATTN_WIDTH
K0, K1 = Q1, Q1 + KV_WIDTH
V0, V1 = K1, K1 + KV_WIDTH
U0, U1 = V1, V1 + GMLP_WIDTH
G0, G1 = U1, U1 + GMLP_WIDTH
B0, B1 = G1, G1 + 2 * D_MODEL

PAIRS = ((0, 1), (0, 2), (0, 3), (1, 2), (1, 3), (2, 3))
N_BUCKETS = N_EXPERT_GROUPS * len(PAIRS)
BUCKET_ROWS = 32
EXPERT_ROW0 = 8

TM_PROJ = 512
PROJ_CHUNK = 512
TM_ATTN = 512
TM_MIX = 512
TM_MOE = 256
MOE_TILES_PER_STEP = 2
TM_ROW = 512
SCORES_AHEAD = 6
ROW_UNROLL = 16
NT_MOE = TOKENS // TM_MOE + N_BUCKETS
SORTED_ROWS = NT_MOE * TM_MOE
SLAB = D_MODEL // 128
META_LANES = 256
assert NT_MOE <= META_LANES

VMEM_LIMIT = 56 * 1024 * 1024

F32 = jnp.float32
BF16 = jnp.bfloat16
U32 = jnp.uint32
I32 = jnp.int32


def _alibi_slopes():
    return [float(np.float32(2.0 ** (-8.0 * (i + 1) / N_HEADS))) for i in range(N_HEADS)]


def _params(n_axes):
    return pltpu.CompilerParams(dimension_semantics=("arbitrary",) * n_axes, vmem_limit_bytes=VMEM_LIMIT)


def _sigmoid(x):
    return 1.0 / (1.0 + jnp.exp(-x))


def _gelu(x):
    return 0.5 * x * (1.0 + lax.erf(x * np.float32(1.0 / np.sqrt(2.0))))


def _dot(a, b):
    return jnp.dot(a, b, preferred_element_type=F32)


def _dot_nt(a, b):
    return lax.dot_general(a, b, (((1,), (1,)), ((), ())), preferred_element_type=F32)


def _mod_kernel(c_ref, w_ref, b_ref, o_ref):
    c = c_ref[...]
    ca = c * _sigmoid(c)
    o_ref[...] = _dot(ca.astype(BF16), w_ref[...].astype(BF16)) + b_ref[...]


def _mod_call(c, w_ada, b_ada):
    n = w_ada.shape[1]
    bn = 1536
    return pl.pallas_call(
        _mod_kernel,
        grid=(n // bn,),
        in_specs=[
            pl.BlockSpec((BATCH, D_MODEL), lambda j: (0, 0)),
            pl.BlockSpec((D_MODEL, bn), lambda j: (0, j)),
            pl.BlockSpec((1, bn), lambda j: (0, j)),
        ],
        out_specs=pl.BlockSpec((BATCH, bn), lambda j: (0, j)),
        out_shape=jax.ShapeDtypeStruct((BATCH, n), F32),
        compiler_params=_params(1),
        name="mod",
    )(c, w_ada, b_ada.reshape(1, n))


def _round_kernel(w_ref, o_ref):
    o_ref[...] = w_ref[...].astype(BF16)


def _round_call(w, layer):
    _, k, n = w.shape
    bn = 512
    return pl.pallas_call(
        _round_kernel,
        grid=(n // bn,),
        in_specs=[pl.BlockSpec((None, k, bn), lambda j: (layer, 0, j))],
        out_specs=pl.BlockSpec((k, bn), lambda j: (0, j)),
        out_shape=jax.ShapeDtypeStruct((k, n), BF16),
        compiler_params=_params(1),
        name="round_weight",
    )(w)


def _inproj_kernel(x_ref, sc_ref, sh_ref, ng_ref, w_ref, wvt_ref, qg_ref, kg_ref, lng_ref, lnb_ref, bg_ref, qaug_ref,
                   q_o, k_o, vt_o, u_o, vg_o, gt_o):
    x = x_ref[...]
    ms = jnp.mean(x * x, axis=-1, keepdims=True)
    h = (x * lax.rsqrt(ms + EPS)) * ng_ref[...]
    h = h * (1.0 + sc_ref[0]) + sh_ref[0]
    hb = h.astype(BF16)

    def proj(c0, c1):
        return _dot(hb, w_ref[:, c0:c1])

    low_half = lax.broadcasted_iota(I32, (1, HEAD_SLOT), 1) < HEAD_DIM

    def store_normed_heads(raw, gain_ref, spare_ref, o_ref):
        for p in range(raw.shape[1] // HEAD_SLOT):
            pair = raw[:, p * HEAD_SLOT:(p + 1) * HEAD_SLOT]
            for hd, head in ((2 * p, pair), (2 * p + 1, pltpu.roll(pair, HEAD_DIM, 1))):
                cols = slice(hd * HEAD_SLOT, (hd + 1) * HEAD_SLOT)
                xh = jnp.where(low_half, head, 0.0)
                r = lax.rsqrt(jnp.sum(xh * xh, axis=-1, keepdims=True) * (1.0 / HEAD_DIM) + EPS)
                y = xh * r * gain_ref[:, cols]
                if spare_ref is not None:
                    y = y + spare_ref[:, cols]
                o_ref[:, cols] = y.astype(BF16)

    store_normed_heads(proj(Q0, Q1), qg_ref, qaug_ref, q_o)
    store_normed_heads(proj(K0, K1), kg_ref, None, k_o)
    vt_o[...] = _dot_nt(wvt_ref[...], hb).astype(BF16)

    cw = PROJ_CHUNK

    def gate_chunk(c):
        cols = slice(c * cw, (c + 1) * cw)
        gt_o[:, cols] = _sigmoid(proj(B0 + c * cw, B0 + (c + 1) * cw) + bg_ref[:, cols]).astype(BF16)

    vg_parts = []
    for c in range(GMLP_WIDTH // cw):
        u_o[:, c * cw:(c + 1) * cw] = _gelu(proj(U0 + c * cw, U0 + (c + 1) * cw)).astype(BF16)
        gate_chunk(2 * c)
        vg_parts.append(_gelu(proj(G0 + c * cw, G0 + (c + 1) * cw)))
        gate_chunk(2 * c + 1)
    vg = jnp.concatenate(vg_parts, axis=1)
    mu = jnp.mean(vg, axis=-1, keepdims=True)
    vc = vg - mu
    var = jnp.mean(vc * vc, axis=-1, keepdims=True)
    vg_o[...] = (vc * lax.rsqrt(var + EPS) * lng_ref[...] + lnb_ref[...]).astype(BF16)


def _inproj_call(x2, scale1, shift1, norm_gain, w_in, wv_t, q_gain_row, k_gain_row, ln_gain, ln_bias, b_gate, q_aug):
    tm = TM_PROJ
    per_batch = SEQ // tm
    row = lambda n: pl.BlockSpec((1, n), lambda i: (0, 0))
    modrow = pl.BlockSpec((1, 1, D_MODEL), lambda i: (i // per_batch, 0, 0))
    tile = lambda n: pl.BlockSpec((tm, n), lambda i: (i, 0))
    out = lambda n: jax.ShapeDtypeStruct((TOKENS, n), BF16)
    return pl.pallas_call(
        _inproj_kernel,
        grid=(TOKENS // tm,),
        in_specs=[
            tile(D_MODEL), modrow, modrow, row(D_MODEL),
            pl.BlockSpec((D_MODEL, IN_WIDTH), lambda i: (0, 0)),
            pl.BlockSpec((KV_WIDTH, D_MODEL), lambda i: (0, 0)),
            row(N_HEADS * HEAD_SLOT), row(N_KV_HEADS * HEAD_SLOT), row(GMLP_WIDTH), row(GMLP_WIDTH),
            row(2 * D_MODEL), row(N_HEADS * HEAD_SLOT),
        ],
        out_specs=[tile(N_HEADS * HEAD_SLOT), tile(N_KV_HEADS * HEAD_SLOT),
                   pl.BlockSpec((KV_WIDTH, tm), lambda i: (0, i)),
                   tile(GMLP_WIDTH), tile(GMLP_WIDTH), tile(2 * D_MODEL)],
        out_shape=[out(N_HEADS * HEAD_SLOT), out(N_KV_HEADS * HEAD_SLOT),
                   jax.ShapeDtypeStruct((KV_WIDTH, TOKENS), BF16),
                   out(GMLP_WIDTH), out(GMLP_WIDTH), out(2 * D_MODEL)],
        compiler_params=_params(1),
        name="inproj",
    )(x2, scale1, shift1, norm_gain, w_in, wv_t, q_gain_row, k_gain_row, ln_gain, ln_bias, b_gate, q_aug)


def _attn_kernel(sink_ref, q_ref, kc_ref, kp_ref, vc_ref, vp_ref, o_ref):
    i = pl.program_id(1)
    slopes = _alibi_slopes()
    kj = lax.broadcasted_iota(I32, (BLOCK, 2 * BLOCK), 0)
    qi = lax.broadcasted_iota(I32, (BLOCK, 2 * BLOCK), 1) & (BLOCK - 1)
    own = kj <= qi
    from_own = jnp.where(own, 1.0, 0.0).astype(BF16)
    from_prev = jnp.where(own, 0.0, 1.0).astype(BF16)
    has_prev = i > 0
    low_half = lax.broadcasted_iota(I32, (1, HEAD_SLOT), 1) < HEAD_DIM
    k_lane = lax.broadcasted_iota(I32, (2 * BLOCK, HEAD_SLOT), 1)
    k_aug = jnp.where((k_lane >= HEAD_DIM) & (k_lane < HEAD_DIM + ALIBI_PARTS),
                      lax.broadcasted_iota(I32, (2 * BLOCK, HEAD_SLOT), 0), 0).astype(F32).astype(BF16)
    q_pos = (lax.broadcasted_iota(I32, (1, BLOCK), 1) + BLOCK).astype(F32)

    keys_of = {}

    def scores(item):
        sb, kv, pr = item
        r0, r1 = sb * BLOCK, (sb + 1) * BLOCK
        if (sb, kv) not in keys_of:
            cols = slice(kv * HEAD_SLOT, (kv + 1) * HEAD_SLOT)
            k_prev = kp_ref[:, cols] if sb == 0 else kc_ref[r0 - BLOCK:r0, cols]
            keys_of[sb, kv] = jnp.where(low_half, jnp.concatenate([k_prev, kc_ref[r0:r1, cols]], axis=0), k_aug)
        ha = kv * Q_PER_KV + 2 * pr
        queries = jnp.concatenate([q_ref[r0:r1, ha * HEAD_SLOT:(ha + 1) * HEAD_SLOT],
                                   q_ref[r0:r1, (ha + 1) * HEAD_SLOT:(ha + 2) * HEAD_SLOT]], axis=0)
        s = _dot_nt(keys_of[sb, kv], queries)
        s_prev = s[0:BLOCK, :]
        if sb == 0:
            s_prev = jnp.where(has_prev, s_prev, NEG_INF)
        return jnp.where(own, s[BLOCK:2 * BLOCK, :], s_prev)

    def attend(item, s):
        sb, kv, pr = item
        r0, r1 = sb * BLOCK, (sb + 1) * BLOCK
        rows = slice(kv * HEAD_DIM, (kv + 1) * HEAD_DIM)
        v_prev = vp_ref[rows, :] if sb == 0 else vc_ref[rows, r0 - BLOCK:r0]
        vt = jnp.concatenate([v_prev, vc_ref[rows, r0:r1]], axis=1)
        ha = kv * Q_PER_KV + 2 * pr
        hb = ha + 1
        sink = LOG2E * jnp.concatenate(
            [sink_ref[ha] + slopes[ha] * q_pos, sink_ref[hb] + slopes[hb] * q_pos], axis=1)
        m = jnp.maximum(jnp.max(s, axis=0, keepdims=True), sink)
        p = jnp.exp2(s - m)
        den = jnp.sum(p, axis=0, keepdims=True) + jnp.exp2(sink - m)
        pb = p.astype(BF16)
        p_keys = jnp.concatenate([pb * from_prev, pb * from_own], axis=0)
        o = _dot(vt, p_keys) / den
        pair = jnp.concatenate([o[:, 0:BLOCK], o[:, BLOCK:2 * BLOCK]], axis=0).T
        o_ref[r0:r1, ha * HEAD_DIM:(hb + 1) * HEAD_DIM] = pair.astype(BF16)

    items = [(sb, kv, pr) for sb in range(TM_ATTN // BLOCK) for kv in range(N_KV_HEADS)
             for pr in range(Q_PER_KV // 2)]
    pending = []
    for n in range(len(items) + SCORES_AHEAD):
        if n < len(items):
            pending.append(scores(items[n]))
        if n >= SCORES_AHEAD:
            attend(items[n - SCORES_AHEAD], pending.pop(0))


def _attn_call(sinks, q, k, vt):
    tq = TM_ATTN
    per_batch = SEQ // tq
    blocks_per_tile = tq // BLOCK
    q_width = N_HEADS * HEAD_SLOT
    k_width = N_KV_HEADS * HEAD_SLOT
    cur = lambda n: pl.BlockSpec((tq, n), lambda b, i: (b * per_batch + i, 0))
    prev_block = lambda b, i: b * (SEQ // BLOCK) + jnp.maximum(i * blocks_per_tile - 1, 0)
    return pl.pallas_call(
        _attn_kernel,
        grid=(BATCH, per_batch),
        in_specs=[
            pl.BlockSpec(memory_space=pltpu.SMEM),
            cur(q_width), cur(k_width),
            pl.BlockSpec((BLOCK, k_width), lambda b, i: (prev_block(b, i), 0)),
            pl.BlockSpec((KV_WIDTH, tq), lambda b, i: (0, b * per_batch + i)),
            pl.BlockSpec((KV_WIDTH, BLOCK), lambda b, i: (0, prev_block(b, i))),
        ],
        out_specs=cur(ATTN_WIDTH),
        out_shape=jax.ShapeDtypeStruct((TOKENS, ATTN_WIDTH), BF16),
        compiler_params=_params(2),
        name="attn",
    )(sinks, q, k, k, vt, vt)


def _first_argmax(vals):
    m = vals[0]
    for v in vals[1:]:
        m = jnp.maximum(m, v)
    idx = jnp.full(m.shape, len(vals) - 1, I32)
    for k in range(len(vals) - 2, -1, -1):
        idx = jnp.where(vals[k] == m, k, idx)
    return m, idx


def _mixer_kernel(x_ref, ya_ref, u_ref, vg_ref, gt_ref, ws_ref, bmap_ref, woa_ref, wog_ref, wout_ref, gate1_ref,
                  ng_ref, sc_ref, sh_ref, wrt_ref, brt_ref, x1_o, h2_o, bk_o, cnt_o, cnt):
    tm = TM_MIX
    t_idx = lax.broadcasted_iota(I32, (GMLP_CHUNK, GMLP_CHUNK), 0)
    s_idx = lax.broadcasted_iota(I32, (GMLP_CHUNK, GMLP_CHUNK), 1)
    ws = [jnp.where(t_idx >= s_idx, ws_ref[g], 0.0).astype(BF16) for g in range(GMLP_GROUPS)]
    gc = GMLP_WIDTH // GMLP_GROUPS
    chunks = []
    for c in range(tm // GMLP_CHUNK):
        r0, r1 = c * GMLP_CHUNK, (c + 1) * GMLP_CHUNK
        mixed = jnp.concatenate(
            [_dot(ws[g], vg_ref[r0:r1, g * gc:(g + 1) * gc]) for g in range(GMLP_GROUPS)], axis=1)
        mixed = mixed + bmap_ref[...]
        chunks.append((u_ref[r0:r1, :].astype(F32) * mixed).astype(BF16))
    y_gmlp = jnp.concatenate(chunks, axis=0)
    pa = _dot(ya_ref[...], woa_ref[...])
    pg = _dot(y_gmlp, wog_ref[...])
    merged = gt_ref[:, 0:D_MODEL].astype(F32) * pa + gt_ref[:, D_MODEL:2 * D_MODEL].astype(F32) * pg
    x1 = x_ref[...] + gate1_ref[0] * _dot(merged.astype(BF16), wout_ref[...])
    x1_o[...] = x1

    ms = jnp.mean(x1 * x1, axis=-1, keepdims=True)
    h2 = (x1 * lax.rsqrt(ms + EPS)) * ng_ref[...]
    h2 = h2 * (1.0 + sc_ref[0]) + sh_ref[0]
    _store_slabs(h2_o, h2)

    lg = _dot_nt(wrt_ref[...], h2.astype(BF16)) + brt_ref[...]
    _, gi = _first_argmax([lg[r:r + 1, :] for r in range(N_EXPERT_GROUPS)])
    el = [lg[EXPERT_ROW0 + r:EXPERT_ROW0 + r + 1, :] for r in range(N_EXPERTS)]
    eg = []
    for k in range(EXPERTS_PER_GROUP):
        v = el[(N_EXPERT_GROUPS - 1) * EXPERTS_PER_GROUP + k]
        for g in range(N_EXPERT_GROUPS - 2, -1, -1):
            v = jnp.where(gi == g, el[g * EXPERTS_PER_GROUP + k], v)
        eg.append(v)
    _, i1 = _first_argmax(eg)
    _, i2 = _first_argmax([jnp.where(i1 == k, -3e38, eg[k]) for k in range(EXPERTS_PER_GROUP)])
    lo = jnp.minimum(i1, i2)
    hi = jnp.maximum(i1, i2)
    pair = jnp.where(lo == 0, hi - 1, jnp.where(lo == 1, hi + 1, 5))
    bucket = gi * len(PAIRS) + pair
    bk_o[0] = bucket

    @pl.when(pl.program_id(0) == 0)
    def _():
        cnt[...] = jnp.zeros_like(cnt)
    in_bucket = lax.broadcasted_iota(I32, (BUCKET_ROWS, tm), 0) == bucket
    cnt[...] += jnp.sum(jnp.where(in_bucket, 1.0, 0.0), axis=1, keepdims=True)
    cnt_o[...] = cnt[...]


def _mixer_call(x2, y_attn, u, vg, gates, w_spatial, bias_map, wo_attn, wo_gmlp, w_out, gate1, norm_gain, scale2, shift2,
                wr_t, br_t):
    tm = TM_MIX
    per_batch = SEQ // tm
    nt = TOKENS // tm
    tile = lambda n: pl.BlockSpec((tm, n), lambda i: (i, 0))
    full = lambda shape: pl.BlockSpec(shape, lambda i: (0,) * len(shape))
    modrow = pl.BlockSpec((1, 1, D_MODEL), lambda i: (i // per_batch, 0, 0))
    return pl.pallas_call(
        _mixer_kernel,
        grid=(nt,),
        in_specs=[
            tile(D_MODEL), tile(ATTN_WIDTH), tile(GMLP_WIDTH), tile(GMLP_WIDTH), tile(2 * D_MODEL),
            full((GMLP_GROUPS, GMLP_CHUNK, GMLP_CHUNK)), full((GMLP_CHUNK, GMLP_WIDTH)),
            full((ATTN_WIDTH, D_MODEL)), full((GMLP_WIDTH, D_MODEL)), full((D_MODEL, D_MODEL)),
            modrow, full((1, D_MODEL)), modrow, modrow,
            full((BUCKET_ROWS, D_MODEL)), full((BUCKET_ROWS, 1)),
        ],
        out_specs=[tile(D_MODEL), pl.BlockSpec((tm * SLAB, 128), lambda i: (i, 0)),
                   pl.BlockSpec((1, 1, tm), lambda i: (i, 0, 0)), full((BUCKET_ROWS, 128))],
        out_shape=[
            jax.ShapeDtypeStruct((TOKENS, D_MODEL), F32),
            jax.ShapeDtypeStruct((TOKENS * SLAB, 128), F32),
            jax.ShapeDtypeStruct((nt, 1, tm), I32),
            jax.ShapeDtypeStruct((BUCKET_ROWS, 128), F32),
        ],
        scratch_shapes=[pltpu.VMEM((BUCKET_ROWS, 128), F32)],
        compiler_params=_params(1),
        name="mixer",
    )(x2, y_attn, u, vg, gates, w_spatial, bias_map, wo_attn, wo_gmlp, w_out, gate1, norm_gain, scale2, shift2,
      wr_t, br_t)


META_TILE_BUCKET, META_N_TILES, META_PAD_START, META_PAD_END = 0, 1, 2, 3


def _rank_kernel(bk_ref, total_ref, pos_o, meta_o, seen, off):
    tm = TM_MIX
    i = pl.program_id(0)
    bk = bk_ref[0]
    onehot = lax.broadcasted_iota(I32, (BUCKET_ROWS, tm), 0) == bk

    @pl.when(i == 0)
    def _():
        total = total_ref[...]
        tiles = jnp.floor((total + (TM_MOE - 1)) * (1.0 / TM_MOE))
        r = lax.broadcasted_iota(I32, (BUCKET_ROWS, BUCKET_ROWS), 0)
        c = lax.broadcasted_iota(I32, (BUCKET_ROWS, BUCKET_ROWS), 1)
        before = jnp.where(c < r, 1.0, 0.0).astype(BF16)
        first_tile = _dot(before, tiles.astype(BF16))
        first_row = first_tile * TM_MOE
        off[...] = first_row
        seen[...] = jnp.zeros_like(seen)
        end_tile = (first_tile + tiles)[:, 0:1]
        lane = lax.broadcasted_iota(I32, (BUCKET_ROWS, META_LANES), 1)
        bsub = lax.broadcasted_iota(I32, (BUCKET_ROWS, META_LANES), 0)
        is_bucket = bsub < N_BUCKETS
        tile_bucket = jnp.sum(jnp.where((lane.astype(F32) >= end_tile) & is_bucket, 1.0, 0.0), axis=0, keepdims=True)
        as_row = lambda col: jnp.sum(jnp.where((bsub == lane) & is_bucket, col, 0.0), axis=0, keepdims=True)
        n_tiles = jnp.sum(jnp.where(bsub == N_BUCKETS - 1, end_tile, 0.0), axis=0, keepdims=True)
        pad_start = as_row(end_tile - jnp.minimum(tiles[:, 0:1], 1.0))
        pad_end = as_row(end_tile)
        row = lax.broadcasted_iota(I32, (8, META_LANES), 0)
        meta = jnp.zeros((8, META_LANES), F32)
        for k, v in ((META_TILE_BUCKET, tile_bucket), (META_N_TILES, n_tiles), (META_PAD_START, pad_start),
                     (META_PAD_END, pad_end)):
            meta = jnp.where(row == k, v, meta)
        meta_o[...] = meta.astype(I32)

    s_idx = lax.broadcasted_iota(I32, (tm, tm), 0)
    t_idx = lax.broadcasted_iota(I32, (tm, tm), 1)
    upto = jnp.where(s_idx <= t_idx, 1.0, 0.0).astype(BF16)
    incl = _dot(jnp.where(onehot, 1.0, 0.0).astype(BF16), upto)
    base = off[:, 0:1] + seen[:, 0:1]
    posf = jnp.sum(jnp.where(onehot, base + incl - 1.0, 0.0), axis=0, keepdims=True)
    pos_o[0] = posf.astype(I32)
    seen[...] += incl[:, tm - 1:tm]


def _rank_call(buckets, totals):
    nt, _, tm = buckets.shape
    return pl.pallas_call(
        _rank_kernel,
        grid=(nt,),
        in_specs=[pl.BlockSpec((1, 1, tm), lambda i: (i, 0, 0)),
                  pl.BlockSpec((BUCKET_ROWS, 128), lambda i: (0, 0))],
        out_specs=[
            pl.BlockSpec((1, 1, tm), lambda i: (i, 0, 0)),
            pl.BlockSpec((8, META_LANES), lambda i: (0, 0)),
        ],
        out_shape=[
            jax.ShapeDtypeStruct((nt, 1, tm), I32),
            jax.ShapeDtypeStruct((8, META_LANES), I32),
        ],
        scratch_shapes=[pltpu.VMEM((BUCKET_ROWS, 128), F32), pltpu.VMEM((BUCKET_ROWS, 128), F32)],
        compiler_params=_params(1),
        name="rank",
    )(buckets, totals)


def _store_slabs(ref, x):
    n = x.shape[0]
    for s in range(SLAB):
        ref[pl.ds(s, n, stride=SLAB), :] = x[:, s * 128:(s + 1) * 128]


def _load_slabs(ref, n):
    return jnp.concatenate([ref[pl.ds(s, n, stride=SLAB), :] for s in range(SLAB)], axis=1)


def _row_copy(src, src_row, dst, dst_row, sem):
    return pltpu.make_async_copy(src.at[pl.ds(pl.multiple_of(src_row * SLAB, SLAB), SLAB)],
                                 dst.at[pl.ds(pl.multiple_of(dst_row * SLAB, SLAB), SLAB)], sem)


def _start_tile_rows(copy_of_row):
    def trip(c, carry):
        for k in range(ROW_UNROLL):
            copy_of_row(c * ROW_UNROLL + k).start(priority=k % 2)
        return carry
    lax.fori_loop(0, TM_ROW // ROW_UNROLL, trip, 0)


def _tile_wait(src, dst, sem):
    pltpu.make_async_copy(src.at[pl.ds(0, TM_ROW * SLAB)], dst, sem).wait()


def _scatter_kernel(pos_ref, meta_ref, h_ref, o_ref, zero_tile, sem, pad_sem):
    @pl.when(pl.program_id(0) == 0)
    def _():
        zero_tile[...] = jnp.zeros_like(zero_tile)

        def tile_copy(j):
            rows = pl.ds(pl.multiple_of(j * (TM_MOE * SLAB), TM_MOE * SLAB), TM_MOE * SLAB)
            return pltpu.make_async_copy(zero_tile, o_ref.at[rows], pad_sem)

        def for_all_zero_copies(act):
            def per_bucket(b, carry):
                lo, hi = meta_ref[META_PAD_START, b], meta_ref[META_PAD_END, b]
                return lax.fori_loop(lo, hi, lambda j, c: (act(tile_copy(j)), c)[1], carry)
            lax.fori_loop(0, N_BUCKETS, per_bucket, 0)
            lax.fori_loop(meta_ref[META_N_TILES, 0], NT_MOE, lambda j, c: (act(tile_copy(j)), c)[1], 0)

        for_all_zero_copies(lambda cp: cp.start())
        for_all_zero_copies(lambda cp: cp.wait())

    _start_tile_rows(lambda r: _row_copy(h_ref, r, o_ref, pos_ref[0, 0, r], sem))
    _tile_wait(h_ref, o_ref.at[pl.ds(0, TM_ROW * SLAB)], sem)


def _scatter_call(pos, meta, h2):
    tm = TM_ROW
    return pl.pallas_call(
        _scatter_kernel,
        grid=(TOKENS // tm,),
        in_specs=[
            pl.BlockSpec((1, 1, tm), lambda i: (i, 0, 0), memory_space=pltpu.SMEM),
            pl.BlockSpec(memory_space=pltpu.SMEM),
            pl.BlockSpec((tm * SLAB, 128), lambda i: (i, 0)),
        ],
        out_specs=pl.BlockSpec(memory_space=pl.ANY),
        out_shape=jax.ShapeDtypeStruct((SORTED_ROWS * SLAB, 128), F32),
        scratch_shapes=[pltpu.VMEM((TM_MOE * SLAB, 128), F32), pltpu.SemaphoreType.DMA(()),
                        pltpu.SemaphoreType.DMA(())],
        compiler_params=_params(1),
        name="scatter",
    )(pos, meta, h2)


def _gather_kernel(pos_ref, pos_next_ref, ys_ref, x1_ref, gate2_ref, o_ref, buf, sem):
    i = pl.program_id(0)
    n = pl.num_programs(0)
    slot = lax.rem(i, 2)

    def fetch(p_ref, s):
        _start_tile_rows(lambda r: _row_copy(ys_ref, p_ref[0, 0, r], buf.at[s], r, sem.at[s]))

    @pl.when(i == 0)
    def _():
        fetch(pos_ref, 0)

    @pl.when(i + 1 < n)
    def _():
        fetch(pos_next_ref, 1 - slot)

    _tile_wait(ys_ref, buf.at[slot], sem.at[slot])
    o_ref[...] = x1_ref[...] + gate2_ref[0] * _load_slabs(buf.at[slot], TM_ROW)


def _gather_call(pos, y_sorted, x1, gate2):
    tm = TM_ROW
    per_batch = SEQ // tm
    n = TOKENS // tm
    return pl.pallas_call(
        _gather_kernel,
        grid=(n,),
        in_specs=[
            pl.BlockSpec((1, 1, tm), lambda i: (i, 0, 0), memory_space=pltpu.SMEM),
            pl.BlockSpec((1, 1, tm), lambda i: (jnp.minimum(i + 1, n - 1), 0, 0), memory_space=pltpu.SMEM),
            pl.BlockSpec(memory_space=pl.ANY),
            pl.BlockSpec((tm, D_MODEL), lambda i: (i, 0)),
            pl.BlockSpec((1, 1, D_MODEL), lambda i: (i // per_batch, 0, 0)),
        ],
        out_specs=pl.BlockSpec((tm, D_MODEL), lambda i: (i, 0)),
        out_shape=jax.ShapeDtypeStruct((TOKENS, D_MODEL), F32),
        scratch_shapes=[pltpu.VMEM((2, tm * SLAB, 128), F32), pltpu.SemaphoreType.DMA((2,))],
        compiler_params=_params(1),
        name="gather",
    )(pos, pos, y_sorted, x1, gate2)


PLAN_ROW, PLAN_E_LO, PLAN_E_HI, PLAN_GROUP = 0, 1, 2, 3


def _tile_plan(tile_bucket, n_tiles, tables):
    row = jnp.minimum(jnp.arange(NT_MOE, dtype=I32), n_tiles[0] - 1)
    bucket = tile_bucket[row]
    return jnp.concatenate([row, tables[bucket], tables[N_BUCKETS + bucket], tables[2 * N_BUCKETS + bucket]])


def _moe_kernel(plan_ref, nt_ref, *refs):
    n = MOE_TILES_PER_STEP
    h_refs = refs[0:n]
    wr_ref, brow_ref = refs[n:n + 2]
    w_refs = [refs[n + 2 + 6 * t:n + 2 + 6 * (t + 1)] for t in range(n)]
    o_ref = refs[n + 2 + 6 * n]
    first = pl.program_id(0) * n
    n_used = nt_ref[0]

    @pl.when(first < n_used)
    def _():
        tiles = []
        for t in range(n):
            e_lo = plan_ref[PLAN_E_LO * NT_MOE + first + t]
            e_hi = plan_ref[PLAN_E_HI * NT_MOE + first + t]
            g = plan_ref[PLAN_GROUP * NT_MOE + first + t]
            h = _load_slabs(h_refs[t], TM_MOE).astype(BF16)
            lg = _dot(h, wr_ref[...]) + brow_ref[...]
            wg_lo, wu_lo, _, wg_hi, wu_hi, _ = w_refs[t]
            ups = (_dot(h, wg_lo[...]), _dot(h, wu_lo[...]), _dot(h, wg_hi[...]), _dot(h, wu_hi[...]))
            tiles.append((lg, e_lo, e_hi, g, ups))

        for t, (lg, e_lo, e_hi, g, (a_lo, b_lo, a_hi, b_hi)) in enumerate(tiles):
            lane = lax.broadcasted_iota(I32, lg.shape, 1)
            is_group = lane < N_EXPERT_GROUPS
            gmax = jnp.max(jnp.where(is_group, lg, NEG_INF), axis=1, keepdims=True)
            ge = jnp.exp(lg - gmax)
            g_w = (jnp.sum(jnp.where(lane == g, ge, 0.0), axis=1, keepdims=True)
                   / jnp.sum(jnp.where(is_group, ge, 0.0), axis=1, keepdims=True))
            v_lo = jnp.sum(jnp.where(lane == EXPERT_ROW0 + e_lo, lg, 0.0), axis=1, keepdims=True)
            v_hi = jnp.sum(jnp.where(lane == EXPERT_ROW0 + e_hi, lg, 0.0), axis=1, keepdims=True)
            m = jnp.maximum(v_lo, v_hi)
            x_lo = jnp.exp(v_lo - m)
            x_hi = jnp.exp(v_hi - m)
            w_lo = x_lo / (x_lo + x_hi) * g_w
            w_hi = x_hi / (x_lo + x_hi) * g_w
            hid_lo = (a_lo * _sigmoid(a_lo) * b_lo * w_lo).astype(BF16)
            hid_hi = (a_hi * _sigmoid(a_hi) * b_hi * w_hi).astype(BF16)
            _, _, wd_lo, _, _, wd_hi = w_refs[t]
            y = _dot(hid_lo, wd_lo[...]) + _dot(hid_hi, wd_hi[...])
            if t > 0:
                y = jnp.where(first + t < n_used, y, 0.0)
            for s in range(SLAB):
                o_ref[pl.ds(t * TM_MOE * SLAB + s, TM_MOE, stride=SLAB), :] = y[:, s * 128:(s + 1) * 128]

    @pl.when(first >= n_used)
    def _():
        o_ref[...] = jnp.zeros_like(o_ref)


def _moe_call(plan, n_tiles, h_sorted, wr, br_row, w_gate, w_up, w_down):
    tm = TM_MOE
    n = MOE_TILES_PER_STEP

    def h_spec(t):
        return pl.BlockSpec((tm * SLAB, 128), lambda s, plan, nt: (plan[PLAN_ROW * NT_MOE + s * n + t], 0))

    def w_spec(shape, section, t):
        return pl.BlockSpec((None,) + shape, lambda s, plan, nt: (plan[section * NT_MOE + s * n + t], 0, 0))

    up = (D_MODEL, D_EXPERT)
    down = (D_EXPERT, D_MODEL)
    per_tile_weights = [[w_spec(up, PLAN_E_LO, t), w_spec(up, PLAN_E_LO, t), w_spec(down, PLAN_E_LO, t),
                         w_spec(up, PLAN_E_HI, t), w_spec(up, PLAN_E_HI, t), w_spec(down, PLAN_E_HI, t)]
                        for t in range(n)]
    grid_spec = pltpu.PrefetchScalarGridSpec(
        num_scalar_prefetch=2,
        grid=(NT_MOE // n,),
        in_specs=[h_spec(t) for t in range(n)] + [
            pl.BlockSpec((D_MODEL, 128), lambda s, plan, nt: (0, 0)),
            pl.BlockSpec((1, 128), lambda s, plan, nt: (0, 0)),
        ] + [spec for specs in per_tile_weights for spec in specs],
        out_specs=pl.BlockSpec((n * tm * SLAB, 128), lambda s, plan, nt: (s, 0)),
    )
    weights = [w_gate, w_up, w_down, w_gate, w_up, w_down] * n
    return pl.pallas_call(
        _moe_kernel,
        grid_spec=grid_spec,
        out_shape=jax.ShapeDtypeStruct((SORTED_ROWS * SLAB, 128), F32),
        compiler_params=_params(1),
        name="moe",
    )(plan, n_tiles, *([h_sorted] * n), wr, br_row, *weights)


def _alibi_query_lanes():
    lanes = np.zeros((1, N_HEADS * HEAD_SLOT), np.float32)
    for hd, slope in enumerate(_alibi_slopes()):
        rest = np.float32(slope * LOG2E)
        for part in range(ALIBI_PARTS):
            piece = np.float32(np.asarray(rest).astype(jnp.bfloat16))
            lanes[0, hd * HEAD_SLOT + HEAD_DIM + part] = piece
            rest = np.float32(rest - piece)
    return jnp.asarray(lanes)


def _bucket_tables():
    e_lo, e_hi, grp = [], [], []
    for g in range(N_EXPERT_GROUPS):
        for lo, hi in PAIRS:
            e_lo.append(g * EXPERTS_PER_GROUP + lo)
            e_hi.append(g * EXPERTS_PER_GROUP + hi)
            grp.append(g)
    return jnp.asarray(e_lo + e_hi + grp, I32)


def kernel(x, c, w_ada, b_ada, norm1_gain, w_in, b_branch_gate, q_norm_gain, k_norm_gain, attn_sinks, gmlp_norm_gain, gmlp_norm_bias, gmlp_w_spatial, gmlp_b_spatial, w_o_attn, w_o_gmlp, w_out, norm2_gain, w_group_router, b_group_router, w_expert_router, b_expert_router, w_expert_gate, w_expert_up, w_expert_down):
    depth = w_ada.shape[0]
    x2 = x.reshape(TOKENS, D_MODEL)
    tables = _bucket_tables()
    q_aug = _alibi_query_lanes()
    row = lambda v: v.reshape(1, -1)
    for l in range(depth):
        mod = _mod_call(c, w_ada[l], b_ada[l])
        shift1, scale1, gate1, shift2, scale2, gate2 = [
            m.reshape(BATCH, 1, D_MODEL) for m in jnp.split(mod, 6, axis=-1)]

        slot_pad = jnp.zeros((HEAD_SLOT - HEAD_DIM,), F32)
        q_gain_row = row(jnp.tile(jnp.concatenate([q_norm_gain[l] * (HEAD_DIM ** -0.5 * LOG2E), slot_pad]), N_HEADS))
        k_gain_row = row(jnp.tile(jnp.concatenate([k_norm_gain[l], slot_pad]), N_KV_HEADS))
        w_in_b = _round_call(w_in, l)
        q, k, vt, u, vg, gates = _inproj_call(
            x2, scale1, shift1, row(norm1_gain[l]), w_in_b, w_in_b[:, V0:V1].T,
            q_gain_row, k_gain_row, row(gmlp_norm_gain[l]), row(gmlp_norm_bias[l]), row(b_branch_gate[l]), q_aug)

        y_attn = _attn_call(attn_sinks[l], q, k, vt)

        bias_map = jnp.repeat(gmlp_b_spatial[l].T, GMLP_WIDTH // GMLP_GROUPS, axis=1)
        e0, e1 = EXPERT_ROW0, EXPERT_ROW0 + N_EXPERTS
        wr = jnp.zeros((D_MODEL, 128), F32)
        wr = wr.at[:, 0:N_EXPERT_GROUPS].set(w_group_router[l]).at[:, e0:e1].set(w_expert_router[l]).astype(BF16)
        br_row = jnp.zeros((1, 128), F32)
        br_row = br_row.at[0, 0:N_EXPERT_GROUPS].set(b_group_router[l]).at[0, e0:e1].set(b_expert_router[l])
        x1, h2p, buckets, totals = _mixer_call(
            x2, y_attn, u, vg, gates, gmlp_w_spatial[l], bias_map, _round_call(w_o_attn, l),
            _round_call(w_o_gmlp, l), _round_call(w_out, l), gate1, row(norm2_gain[l]), scale2, shift2,
            wr[:, 0:BUCKET_ROWS].T, br_row[:, 0:BUCKET_ROWS].T)

        pos, meta = _rank_call(buckets, totals)
        h_sorted = _scatter_call(pos, meta, h2p)

        flat = lambda w: w.reshape((N_EXPERTS,) + w.shape[2:]).astype(BF16)
        n_tiles = meta[META_N_TILES, 0:1]
        y_sorted = _moe_call(
            _tile_plan(meta[META_TILE_BUCKET], n_tiles, tables), n_tiles, h_sorted, wr, br_row,
            flat(w_expert_gate[l]), flat(w_expert_up[l]), flat(w_expert_down[l]))

        x2 = _gather_call(pos, y_sorted, x1, gate2)
    return x2.reshape(x.shape)
```

```python
import functools

import jax
import jax.numpy as jnp
import numpy as np
from jax import lax
from jax.experimental import pallas as pl
from jax.experimental.pallas import tpu as pltpu

D_MODEL = 1024
BATCH = 8
SEQ = 4096
TOKENS = BATCH * SEQ
N_HEADS = 16
N_KV_HEADS = 4
HEAD_DIM = 64
Q_PER_KV = N_HEADS // N_KV_HEADS
BLOCK = 128
HEAD_SLOT = 128
ALIBI_PARTS = 3
ATTN_WIDTH = N_HEADS * HEAD_DIM
KV_WIDTH = N_KV_HEADS * HEAD_DIM
GMLP_WIDTH = 1024
GMLP_GROUPS = 8
GMLP_CHUNK = 128
N_EXPERT_GROUPS = 4
EXPERTS_PER_GROUP = 4
N_EXPERTS = N_EXPERT_GROUPS * EXPERTS_PER_GROUP
D_EXPERT = 512
IN_WIDTH = ATTN_WIDTH + 2 * KV_WIDTH + 2 * GMLP_WIDTH + 2 * D_MODEL
EPS = 1e-6
NEG_INF = -1e30
LOG2E = float(np.float32(np.log2(np.e)))

Q0, Q1 = 0, ATTN_WIDTH
K0, K1 = Q1, Q1 + KV_WIDTH
V0, V1 = K1, K1 + KV_WIDTH
U0, U1 = V1, V1 + GMLP_WIDTH
G0, G1 = U1, U1 + GMLP_WIDTH
B0, B1 = G1, G1 + 2 * D_MODEL

PAIRS = ((0, 1), (0, 2), (0, 3), (1, 2), (1, 3), (2, 3))
N_BUCKETS = N_EXPERT_GROUPS * len(PAIRS)
BUCKET_ROWS = 32
EXPERT_ROW0 = 8

TM_PROJ = 1024
PROJ_ROWS = 512
PROJ_CHUNK = 512
TM_ATTN = 1024
TM_MIX = 1024
MIX_ROWS = 512
TM_MOE = 256
MOE_TILES_PER_STEP = 2
TM_ROW = 512
SCORES_AHEAD = 6
ROW_UNROLL = 16
NT_MOE = TOKENS // TM_MOE + N_BUCKETS
SORTED_ROWS = NT_MOE * TM_MOE
SLAB = D_MODEL // 128
META_LANES = 256
assert NT_MOE <= META_LANES

VMEM_LIMIT = 56 * 1024 * 1024

F32 = jnp.float32
BF16 = jnp.bfloat16
U32 = jnp.uint32
I32 = jnp.int32


def _alibi_slopes():
    return [float(np.float32(2.0 ** (-8.0 * (i + 1) / N_HEADS))) for i in range(N_HEADS)]


def _params(n_axes):
    return pltpu.CompilerParams(dimension_semantics=("arbitrary",) * n_axes, vmem_limit_bytes=VMEM_LIMIT)


def _sigmoid(x):
    return 1.0 / (1.0 + jnp.exp(-x))


def _gelu(x):
    return 0.5 * x * (1.0 + lax.erf(x * np.float32(1.0 / np.sqrt(2.0))))


def _dot(a, b):
    return jnp.dot(a, b, preferred_element_type=F32)


def _dot_nt(a, b):
    return lax.dot_general(a, b, (((1,), (1,)), ((), ())), preferred_element_type=F32)


def _mod_kernel(c_ref, w_ref, b_ref, o_ref):
    c = c_ref[...]
    ca = c * _sigmoid(c)
    o_ref[...] = _dot(ca.astype(BF16), w_ref[...].astype(BF16)) + b_ref[...]


def _mod_call(c, w_ada, b_ada):
    n = w_ada.shape[1]
    bn = 1536
    return pl.pallas_call(
        _mod_kernel,
        grid=(n // bn,),
        in_specs=[
            pl.BlockSpec((BATCH, D_MODEL), lambda j: (0, 0)),
            pl.BlockSpec((D_MODEL, bn), lambda j: (0, j)),
            pl.BlockSpec((1, bn), lambda j: (0, j)),
        ],
        out_specs=pl.BlockSpec((BATCH, bn), lambda j: (0, j)),
        out_shape=jax.ShapeDtypeStruct((BATCH, n), F32),
        compiler_params=_params(1),
        name="mod",
    )(c, w_ada, b_ada.reshape(1, n))


def _round_kernel(w_ref, o_ref):
    o_ref[...] = w_ref[...].astype(BF16)


def _round_call(w, layer):
    _, k, n = w.shape
    bn = 512
    return pl.pallas_call(
        _round_kernel,
        grid=(n // bn,),
        in_specs=[pl.BlockSpec((None, k, bn), lambda j: (layer, 0, j))],
        out_specs=pl.BlockSpec((k, bn), lambda j: (0, j)),
        out_shape=jax.ShapeDtypeStruct((k, n), BF16),
        compiler_params=_params(1),
        name="round_weight",
    )(w)


def _inproj_kernel(x_ref, sc_ref, sh_ref, ng_ref, w_ref, wvt_ref, qg_ref, kg_ref, lng_ref, lnb_ref, bg_ref, qaug_ref,
                   q_o, k_o, vt_o, u_o, vg_o, gt_o):
    low_half = lax.broadcasted_iota(I32, (1, HEAD_SLOT), 1) < HEAD_DIM
    for r0 in range(0, TM_PROJ, PROJ_ROWS):
        _project_rows(slice(r0, r0 + PROJ_ROWS), low_half, x_ref, sc_ref, sh_ref, ng_ref, w_ref, wvt_ref, qg_ref,
                      kg_ref, lng_ref, lnb_ref, bg_ref, qaug_ref, q_o, k_o, vt_o, u_o, vg_o, gt_o)


def _project_rows(rows, low_half, x_ref, sc_ref, sh_ref, ng_ref, w_ref, wvt_ref, qg_ref, kg_ref, lng_ref, lnb_ref,
                  bg_ref, qaug_ref, q_o, k_o, vt_o, u_o, vg_o, gt_o):
    x = x_ref[rows, :]
    ms = jnp.mean(x * x, axis=-1, keepdims=True)
    h = (x * lax.rsqrt(ms + EPS)) * ng_ref[...]
    h = h * (1.0 + sc_ref[0]) + sh_ref[0]
    hb = h.astype(BF16)

    def proj(c0, c1):
        return _dot(hb, w_ref[:, c0:c1])

    def store_normed_heads(raw, gain_ref, spare_ref, o_ref):
        for p in range(raw.shape[1] // HEAD_SLOT):
            pair = raw[:, p * HEAD_SLOT:(p + 1) * HEAD_SLOT]
            for hd, head in ((2 * p, pair), (2 * p + 1, pltpu.roll(pair, HEAD_DIM, 1))):
                cols = slice(hd * HEAD_SLOT, (hd + 1) * HEAD_SLOT)
                xh = jnp.where(low_half, head, 0.0)
                r = lax.rsqrt(jnp.sum(xh * xh, axis=-1, keepdims=True) * (1.0 / HEAD_DIM) + EPS)
                y = xh * r * gain_ref[:, cols]
                if spare_ref is not None:
                    y = y + spare_ref[:, cols]
                o_ref[rows, cols] = y.astype(BF16)

    store_normed_heads(proj(Q0, Q1), qg_ref, qaug_ref, q_o)
    store_normed_heads(proj(K0, K1), kg_ref, None, k_o)
    vt_o[:, rows] = _dot_nt(wvt_ref[...], hb).astype(BF16)

    cw = PROJ_CHUNK

    def gate_chunk(c):
        cols = slice(c * cw, (c + 1) * cw)
        gt_o[rows, cols] = _sigmoid(proj(B0 + c * cw, B0 + (c + 1) * cw) + bg_ref[:, cols]).astype(BF16)

    vg_parts = []
    for c in range(GMLP_WIDTH // cw):
        u_o[rows, c * cw:(c + 1) * cw] = _gelu(proj(U0 + c * cw, U0 + (c + 1) * cw)).astype(BF16)
        gate_chunk(2 * c)
        vg_parts.append(_gelu(proj(G0 + c * cw, G0 + (c + 1) * cw)))
        gate_chunk(2 * c + 1)
    vg = jnp.concatenate(vg_parts, axis=1)
    mu = jnp.mean(vg, axis=-1, keepdims=True)
    vc = vg - mu
    var = jnp.mean(vc * vc, axis=-1, keepdims=True)
    vg_o[rows, :] = (vc * lax.rsqrt(var + EPS) * lng_ref[...] + lnb_ref[...]).astype(BF16)


def _inproj_call(x2, scale1, shift1, norm_gain, w_in, wv_t, q_gain_row, k_gain_row, ln_gain, ln_bias, b_gate, q_aug):
    tm = TM_PROJ
    per_batch = SEQ // tm
    row = lambda n: pl.BlockSpec((1, n), lambda i: (0, 0))
    modrow = pl.BlockSpec((1, 1, D_MODEL), lambda i: (i // per_batch, 0, 0))
    tile = lambda n: pl.BlockSpec((tm, n), lambda i: (i, 0))
    out = lambda n: jax.ShapeDtypeStruct((TOKENS, n), BF16)
    return pl.pallas_call(
        _inproj_kernel,
        grid=(TOKENS // tm,),
        in_specs=[
            tile(D_MODEL), modrow, modrow, row(D_MODEL),
            pl.BlockSpec((D_MODEL, IN_WIDTH), lambda i: (0, 0)),
            pl.BlockSpec((KV_WIDTH, D_MODEL), lambda i: (0, 0)),
            row(N_HEADS * HEAD_SLOT), row(N_KV_HEADS * HEAD_SLOT), row(GMLP_WIDTH), row(GMLP_WIDTH),
            row(2 * D_MODEL), row(N_HEADS * HEAD_SLOT),
        ],
        out_specs=[tile(N_HEADS * HEAD_SLOT), tile(N_KV_HEADS * HEAD_SLOT),
                   pl.BlockSpec((KV_WIDTH, tm), lambda i: (0, i)),
                   tile(GMLP_WIDTH), tile(GMLP_WIDTH), tile(2 * D_MODEL)],
        out_shape=[out(N_HEADS * HEAD_SLOT), out(N_KV_HEADS * HEAD_SLOT),
                   jax.ShapeDtypeStruct((KV_WIDTH, TOKENS), BF16),
                   out(GMLP_WIDTH), out(GMLP_WIDTH), out(2 * D_MODEL)],
        compiler_params=_params(1),
        name="inproj",
    )(x2, scale1, shift1, norm_gain, w_in, wv_t, q_gain_row, k_gain_row, ln_gain, ln_bias, b_gate, q_aug)


def _attn_kernel(sink_ref, q_ref, kc_ref, kp_ref, vc_ref, vp_ref, o_ref):
    i = pl.program_id(1)
    slopes = _alibi_slopes()
    kj = lax.broadcasted_iota(I32, (BLOCK, 2 * BLOCK), 0)
    qi = lax.broadcasted_iota(I32, (BLOCK, 2 * BLOCK), 1) & (BLOCK - 1)
    own = kj <= qi
    from_own = jnp.where(own, 1.0, 0.0).astype(BF16)
    from_prev = jnp.where(own, 0.0, 1.0).astype(BF16)
    has_prev = i > 0
    low_half = lax.broadcasted_iota(I32, (1, HEAD_SLOT), 1) < HEAD_DIM
    k_lane = lax.broadcasted_iota(I32, (2 * BLOCK, HEAD_SLOT), 1)
    k_aug = jnp.where((k_lane >= HEAD_DIM) & (k_lane < HEAD_DIM + ALIBI_PARTS),
                      lax.broadcasted_iota(I32, (2 * BLOCK, HEAD_SLOT), 0), 0).astype(F32).astype(BF16)
    q_pos = (lax.broadcasted_iota(I32, (1, BLOCK), 1) + BLOCK).astype(F32)

    keys_of = {}

    def scores(item):
        sb, kv, pr = item
        r0, r1 = sb * BLOCK, (sb + 1) * BLOCK
        if (sb, kv) not in keys_of:
            cols = slice(kv * HEAD_SLOT, (kv + 1) * HEAD_SLOT)
            k_prev = kp_ref[:, cols] if sb == 0 else kc_ref[r0 - BLOCK:r0, cols]
            keys_of[sb, kv] = jnp.where(low_half, jnp.concatenate([k_prev, kc_ref[r0:r1, cols]], axis=0), k_aug)
        ha = kv * Q_PER_KV + 2 * pr
        queries = jnp.concatenate([q_ref[r0:r1, ha * HEAD_SLOT:(ha + 1) * HEAD_SLOT],
                                   q_ref[r0:r1, (ha + 1) * HEAD_SLOT:(ha + 2) * HEAD_SLOT]], axis=0)
        s = _dot_nt(keys_of[sb, kv], queries)
        s_prev = s[0:BLOCK, :]
        if sb == 0:
            s_prev = jnp.where(has_prev, s_prev, NEG_INF)
        return jnp.where(own, s[BLOCK:2 * BLOCK, :], s_prev)

    def attend(item, s):
        sb, kv, pr = item
        r0, r1 = sb * BLOCK, (sb + 1) * BLOCK
        rows = slice(kv * HEAD_DIM, (kv + 1) * HEAD_DIM)
        v_prev = vp_ref[rows, :] if sb == 0 else vc_ref[rows, r0 - BLOCK:r0]
        vt = jnp.concatenate([v_prev, vc_ref[rows, r0:r1]], axis=1)
        ha = kv * Q_PER_KV + 2 * pr
        hb = ha + 1
        sink = LOG2E * jnp.concatenate(
            [sink_ref[ha] + slopes[ha] * q_pos, sink_ref[hb] + slopes[hb] * q_pos], axis=1)
        m = jnp.maximum(jnp.max(s, axis=0, keepdims=True), sink)
        p = jnp.exp2(s - m)
        den = jnp.sum(p, axis=0, keepdims=True) + jnp.exp2(sink - m)
        pb = p.astype(BF16)
        p_keys = jnp.concatenate([pb * from_prev, pb * from_own], axis=0)
        o = _dot(vt, p_keys) / den
        pair = jnp.concatenate([o[:, 0:BLOCK], o[:, BLOCK:2 * BLOCK]], axis=0).T
        o_ref[r0:r1, ha * HEAD_DIM:(hb + 1) * HEAD_DIM] = pair.astype(BF16)

    items = [(sb, kv, pr) for sb in range(TM_ATTN // BLOCK) for kv in range(N_KV_HEADS)
             for pr in range(Q_PER_KV // 2)]
    pending = []
    for n in range(len(items) + SCORES_AHEAD):
        if n < len(items):
            pending.append(scores(items[n]))
        if n >= SCORES_AHEAD:
            attend(items[n - SCORES_AHEAD], pending.pop(0))


def _attn_call(sinks, q, k, vt):
    tq = TM_ATTN
    per_batch = SEQ // tq
    blocks_per_tile = tq // BLOCK
    q_width = N_HEADS * HEAD_SLOT
    k_width = N_KV_HEADS * HEAD_SLOT
    cur = lambda n: pl.BlockSpec((tq, n), lambda b, i: (b * per_batch + i, 0))
    prev_block = lambda b, i: b * (SEQ // BLOCK) + jnp.maximum(i * blocks_per_tile - 1, 0)
    return pl.pallas_call(
        _attn_kernel,
        grid=(BATCH, per_batch),
        in_specs=[
            pl.BlockSpec(memory_space=pltpu.SMEM),
            cur(q_width), cur(k_width),
            pl.BlockSpec((BLOCK, k_width), lambda b, i: (prev_block(b, i), 0)),
            pl.BlockSpec((KV_WIDTH, tq), lambda b, i: (0, b * per_batch + i)),
            pl.BlockSpec((KV_WIDTH, BLOCK), lambda b, i: (0, prev_block(b, i))),
        ],
        out_specs=cur(ATTN_WIDTH),
        out_shape=jax.ShapeDtypeStruct((TOKENS, ATTN_WIDTH), BF16),
        compiler_params=_params(2),
        name="attn",
    )(sinks, q, k, k, vt, vt)


def _first_argmax(vals):
    m = vals[0]
    for v in vals[1:]:
        m = jnp.maximum(m, v)
    idx = jnp.full(m.shape, len(vals) - 1, I32)
    for k in range(len(vals) - 2, -1, -1):
        idx = jnp.where(vals[k] == m, k, idx)
    return m, idx


def _mixer_kernel(x_ref, ya_ref, u_ref, vg_ref, gt_ref, ws_ref, bmap_ref, woa_ref, wog_ref, wout_ref, gate1_ref,
                  ng_ref, sc_ref, sh_ref, wrt_ref, brt_ref, x1_o, h2_o, bk_o, cnt_o, cnt):
    tm = TM_MIX
    t_idx = lax.broadcasted_iota(I32, (GMLP_CHUNK, GMLP_CHUNK), 0)
    s_idx = lax.broadcasted_iota(I32, (GMLP_CHUNK, GMLP_CHUNK), 1)
    ws = [jnp.where(t_idx >= s_idx, ws_ref[g], 0.0).astype(BF16) for g in range(GMLP_GROUPS)]
    gc = GMLP_WIDTH // GMLP_GROUPS

    def mix_rows(r0):
        rows = slice(r0, r0 + MIX_ROWS)
        chunks = []
        for c0 in range(r0, r0 + MIX_ROWS, GMLP_CHUNK):
            c1 = c0 + GMLP_CHUNK
            mixed = jnp.concatenate(
                [_dot(ws[g], vg_ref[c0:c1, g * gc:(g + 1) * gc]) for g in range(GMLP_GROUPS)], axis=1)
            mixed = mixed + bmap_ref[...]
            chunks.append((u_ref[c0:c1, :].astype(F32) * mixed).astype(BF16))
        y_gmlp = jnp.concatenate(chunks, axis=0)
        pa = _dot(ya_ref[rows, :], woa_ref[...])
        pg = _dot(y_gmlp, wog_ref[...])
        merged = gt_ref[rows, 0:D_MODEL].astype(F32) * pa + gt_ref[rows, D_MODEL:2 * D_MODEL].astype(F32) * pg
        x1 = x_ref[rows, :] + gate1_ref[0] * _dot(merged.astype(BF16), wout_ref[...])
        x1_o[rows, :] = x1

        ms = jnp.mean(x1 * x1, axis=-1, keepdims=True)
        h2 = (x1 * lax.rsqrt(ms + EPS)) * ng_ref[...]
        h2 = h2 * (1.0 + sc_ref[0]) + sh_ref[0]
        for s in range(SLAB):
            h2_o[pl.ds(r0 * SLAB + s, MIX_ROWS, stride=SLAB), :] = h2[:, s * 128:(s + 1) * 128]

        lg = _dot_nt(wrt_ref[...], h2.astype(BF16)) + brt_ref[...]
        _, gi = _first_argmax([lg[r:r + 1, :] for r in range(N_EXPERT_GROUPS)])
        el = [lg[EXPERT_ROW0 + r:EXPERT_ROW0 + r + 1, :] for r in range(N_EXPERTS)]
        eg = []
        for k in range(EXPERTS_PER_GROUP):
            v = el[(N_EXPERT_GROUPS - 1) * EXPERTS_PER_GROUP + k]
            for g in range(N_EXPERT_GROUPS - 2, -1, -1):
                v = jnp.where(gi == g, el[g * EXPERTS_PER_GROUP + k], v)
            eg.append(v)
        _, i1 = _first_argmax(eg)
        _, i2 = _first_argmax([jnp.where(i1 == k, -3e38, eg[k]) for k in range(EXPERTS_PER_GROUP)])
        lo = jnp.minimum(i1, i2)
        hi = jnp.maximum(i1, i2)
        pair = jnp.where(lo == 0, hi - 1, jnp.where(lo == 1, hi + 1, 5))
        bucket = gi * len(PAIRS) + pair
        bk_o[0, :, rows] = bucket
        in_bucket = lax.broadcasted_iota(I32, (BUCKET_ROWS, MIX_ROWS), 0) == bucket
        return jnp.sum(jnp.where(in_bucket, 1.0, 0.0), axis=1, keepdims=True)

    counts = [mix_rows(r0) for r0 in range(0, tm, MIX_ROWS)]

    @pl.when(pl.program_id(0) == 0)
    def _():
        cnt[...] = jnp.zeros_like(cnt)
    cnt[...] += sum(counts[1:], counts[0])
    cnt_o[...] = cnt[...]


def _mixer_call(x2, y_attn, u, vg, gates, w_spatial, bias_map, wo_attn, wo_gmlp, w_out, gate1, norm_gain, scale2, shift2,
                wr_t, br_t):
    tm = TM_MIX
    per_batch = SEQ // tm
    nt = TOKENS // tm
    tile = lambda n: pl.BlockSpec((tm, n), lambda i: (i, 0))
    full = lambda shape: pl.BlockSpec(shape, lambda i: (0,) * len(shape))
    once = lambda shape: pl.BlockSpec(shape, lambda i: (0,) * len(shape), pipeline_mode=pl.Buffered(1))
    modrow = pl.BlockSpec((1, 1, D_MODEL), lambda i: (i // per_batch, 0, 0))
    return pl.pallas_call(
        _mixer_kernel,
        grid=(nt,),
        in_specs=[
            tile(D_MODEL), tile(ATTN_WIDTH), tile(GMLP_WIDTH), tile(GMLP_WIDTH), tile(2 * D_MODEL),
            full((GMLP_GROUPS, GMLP_CHUNK, GMLP_CHUNK)), full((GMLP_CHUNK, GMLP_WIDTH)),
            once((ATTN_WIDTH, D_MODEL)), once((GMLP_WIDTH, D_MODEL)), once((D_MODEL, D_MODEL)),
            modrow, full((1, D_MODEL)), modrow, modrow,
            full((BUCKET_ROWS, D_MODEL)), full((BUCKET_ROWS, 1)),
        ],
        out_specs=[tile(D_MODEL), pl.BlockSpec((tm * SLAB, 128), lambda i: (i, 0)),
                   pl.BlockSpec((1, 1, tm), lambda i: (i, 0, 0)), full((BUCKET_ROWS, 128))],
        out_shape=[
            jax.ShapeDtypeStruct((TOKENS, D_MODEL), F32),
            jax.ShapeDtypeStruct((TOKENS * SLAB, 128), F32),
            jax.ShapeDtypeStruct((nt, 1, tm), I32),
            jax.ShapeDtypeStruct((BUCKET_ROWS, 128), F32),
        ],
        scratch_shapes=[pltpu.VMEM((BUCKET_ROWS, 128), F32)],
        compiler_params=_params(1),
        name="mixer",
    )(x2, y_attn, u, vg, gates, w_spatial, bias_map, wo_attn, wo_gmlp, w_out, gate1, norm_gain, scale2, shift2,
      wr_t, br_t)


META_TILE_BUCKET, META_N_TILES, META_PAD_START, META_PAD_END = 0, 1, 2, 3


def _rank_kernel(bk_ref, total_ref, pos_o, meta_o, seen, off):
    tm = TM_MIX
    i = pl.program_id(0)
    bk = bk_ref[0]
    onehot = lax.broadcasted_iota(I32, (BUCKET_ROWS, tm), 0) == bk

    @pl.when(i == 0)
    def _():
        total = total_ref[...]
        tiles = jnp.floor((total + (TM_MOE - 1)) * (1.0 / TM_MOE))
        r = lax.broadcasted_iota(I32, (BUCKET_ROWS, BUCKET_ROWS), 0)
        c = lax.broadcasted_iota(I32, (BUCKET_ROWS, BUCKET_ROWS), 1)
        before = jnp.where(c < r, 1.0, 0.0).astype(BF16)
        first_tile = _dot(before, tiles.astype(BF16))
        first_row = first_tile * TM_MOE
        off[...] = first_row
        seen[...] = jnp.zeros_like(seen)
        end_tile = (first_tile + tiles)[:, 0:1]
        lane = lax.broadcasted_iota(I32, (BUCKET_ROWS, META_LANES), 1)
        bsub = lax.broadcasted_iota(I32, (BUCKET_ROWS, META_LANES), 0)
        is_bucket = bsub < N_BUCKETS
        tile_bucket = jnp.sum(jnp.where((lane.astype(F32) >= end_tile) & is_bucket, 1.0, 0.0), axis=0, keepdims=True)
        as_row = lambda col: jnp.sum(jnp.where((bsub == lane) & is_bucket, col, 0.0), axis=0, keepdims=True)
        n_tiles = jnp.sum(jnp.where(bsub == N_BUCKETS - 1, end_tile, 0.0), axis=0, keepdims=True)
        pad_start = as_row(end_tile - jnp.minimum(tiles[:, 0:1], 1.0))
        pad_end = as_row(end_tile)
        row = lax.broadcasted_iota(I32, (8, META_LANES), 0)
        meta = jnp.zeros((8, META_LANES), F32)
        for k, v in ((META_TILE_BUCKET, tile_bucket), (META_N_TILES, n_tiles), (META_PAD_START, pad_start),
                     (META_PAD_END, pad_end)):
            meta = jnp.where(row == k, v, meta)
        meta_o[...] = meta.astype(I32)

    s_idx = lax.broadcasted_iota(I32, (tm, tm), 0)
    t_idx = lax.broadcasted_iota(I32, (tm, tm), 1)
    upto = jnp.where(s_idx <= t_idx, 1.0, 0.0).astype(BF16)
    incl = _dot(jnp.where(onehot, 1.0, 0.0).astype(BF16), upto)
    base = off[:, 0:1] + seen[:, 0:1]
    posf = jnp.sum(jnp.where(onehot, base + incl - 1.0, 0.0), axis=0, keepdims=True)
    pos_o[0] = posf.astype(I32)
    seen[...] += incl[:, tm - 1:tm]


def _rank_call(buckets, totals):
    nt, _, tm = buckets.shape
    return pl.pallas_call(
        _rank_kernel,
        grid=(nt,),
        in_specs=[pl.BlockSpec((1, 1, tm), lambda i: (i, 0, 0)),
                  pl.BlockSpec((BUCKET_ROWS, 128), lambda i: (0, 0))],
        out_specs=[
            pl.BlockSpec((1, 1, tm), lambda i: (i, 0, 0)),
            pl.BlockSpec((8, META_LANES), lambda i: (0, 0)),
        ],
        out_shape=[
            jax.ShapeDtypeStruct((nt, 1, tm), I32),
            jax.ShapeDtypeStruct((8, META_LANES), I32),
        ],
        scratch_shapes=[pltpu.VMEM((BUCKET_ROWS, 128), F32), pltpu.VMEM((BUCKET_ROWS, 128), F32)],
        compiler_params=_params(1),
        name="rank",
    )(buckets, totals)


def _store_slabs(ref, x):
    n = x.shape[0]
    for s in range(SLAB):
        ref[pl.ds(s, n, stride=SLAB), :] = x[:, s * 128:(s + 1) * 128]


def _load_slabs(ref, n):
    return jnp.concatenate([ref[pl.ds(s, n, stride=SLAB), :] for s in range(SLAB)], axis=1)


def _row_copy(src, src_row, dst, dst_row, sem):
    return pltpu.make_async_copy(src.at[pl.ds(pl.multiple_of(src_row * SLAB, SLAB), SLAB)],
                                 dst.at[pl.ds(pl.multiple_of(dst_row * SLAB, SLAB), SLAB)], sem)


def _start_tile_rows(copy_of_row):
    def trip(c, carry):
        for k in range(ROW_UNROLL):
            copy_of_row(c * ROW_UNROLL + k).start(priority=k % 2)
        return carry
    lax.fori_loop(0, TM_ROW // ROW_UNROLL, trip, 0)


def _tile_wait(src, dst, sem):
    pltpu.make_async_copy(src.at[pl.ds(0, TM_ROW * SLAB)], dst, sem).wait()


def _scatter_kernel(pos_ref, meta_ref, h_ref, o_ref, zero_tile, sem, pad_sem):
    @pl.when(pl.program_id(0) == 0)
    def _():
        zero_tile[...] = jnp.zeros_like(zero_tile)

        def tile_copy(j):
            rows = pl.ds(pl.multiple_of(j * (TM_MOE * SLAB), TM_MOE * SLAB), TM_MOE * SLAB)
            return pltpu.make_async_copy(zero_tile, o_ref.at[rows], pad_sem)

        def for_all_zero_copies(act):
            def per_bucket(b, carry):
                lo, hi = meta_ref[META_PAD_START, b], meta_ref[META_PAD_END, b]
                return lax.fori_loop(lo, hi, lambda j, c: (act(tile_copy(j)), c)[1], carry)
            lax.fori_loop(0, N_BUCKETS, per_bucket, 0)
            lax.fori_loop(meta_ref[META_N_TILES, 0], NT_MOE, lambda j, c: (act(tile_copy(j)), c)[1], 0)

        for_all_zero_copies(lambda cp: cp.start())
        for_all_zero_copies(lambda cp: cp.wait())

    _start_tile_rows(lambda r: _row_copy(h_ref, r, o_ref, pos_ref[0, 0, r], sem))
    _tile_wait(h_ref, o_ref.at[pl.ds(0, TM_ROW * SLAB)], sem)


def _scatter_call(pos, meta, h2):
    tm = TM_ROW
    return pl.pallas_call(
        _scatter_kernel,
        grid=(TOKENS // tm,),
        in_specs=[
            pl.BlockSpec((1, 1, tm), lambda i: (i, 0, 0), memory_space=pltpu.SMEM),
            pl.BlockSpec(memory_space=pltpu.SMEM),
            pl.BlockSpec((tm * SLAB, 128), lambda i: (i, 0)),
        ],
        out_specs=pl.BlockSpec(memory_space=pl.ANY),
        out_shape=jax.ShapeDtypeStruct((SORTED_ROWS * SLAB, 128), F32),
        scratch_shapes=[pltpu.VMEM((TM_MOE * SLAB, 128), F32), pltpu.SemaphoreType.DMA(()),
                        pltpu.SemaphoreType.DMA(())],
        compiler_params=_params(1),
        name="scatter",
    )(pos, meta, h2)


def _gather_kernel(pos_ref, pos_next_ref, ys_ref, x1_ref, gate2_ref, o_ref, buf, sem):
    i = pl.program_id(0)
    n = pl.num_programs(0)
    slot = lax.rem(i, 2)

    def fetch(p_ref, s):
        _start_tile_rows(lambda r: _row_copy(ys_ref, p_ref[0, 0, r], buf.at[s], r, sem.at[s]))

    @pl.when(i == 0)
    def _():
        fetch(pos_ref, 0)

    @pl.when(i + 1 < n)
    def _():
        fetch(pos_next_ref, 1 - slot)

    _tile_wait(ys_ref, buf.at[slot], sem.at[slot])
    o_ref[...] = x1_ref[...] + gate2_ref[0] * _load_slabs(buf.at[slot], TM_ROW)


def _gather_call(pos, y_sorted, x1, gate2):
    tm = TM_ROW
    per_batch = SEQ // tm
    n = TOKENS // tm
    return pl.pallas_call(
        _gather_kernel,
        grid=(n,),
        in_specs=[
            pl.BlockSpec((1, 1, tm), lambda i: (i, 0, 0), memory_space=pltpu.SMEM),
            pl.BlockSpec((1, 1, tm), lambda i: (jnp.minimum(i + 1, n - 1), 0, 0), memory_space=pltpu.SMEM),
            pl.BlockSpec(memory_space=pl.ANY),
            pl.BlockSpec((tm, D_MODEL), lambda i: (i, 0)),
            pl.BlockSpec((1, 1, D_MODEL), lambda i: (i // per_batch, 0, 0)),
        ],
        out_specs=pl.BlockSpec((tm, D_MODEL), lambda i: (i, 0)),
        out_shape=jax.ShapeDtypeStruct((TOKENS, D_MODEL), F32),
        scratch_shapes=[pltpu.VMEM((2, tm * SLAB, 128), F32), pltpu.SemaphoreType.DMA((2,))],
        compiler_params=_params(1),
        name="gather",
    )(pos, pos, y_sorted, x1, gate2)


PLAN_ROW, PLAN_E_LO, PLAN_E_HI, PLAN_GROUP = 0, 1, 2, 3


def _tile_plan(tile_bucket, n_tiles, tables):
    row = jnp.minimum(jnp.arange(NT_MOE, dtype=I32), n_tiles[0] - 1)
    bucket = tile_bucket[row]
    return jnp.concatenate([row, tables[bucket], tables[N_BUCKETS + bucket], tables[2 * N_BUCKETS + bucket]])


def _moe_kernel(plan_ref, nt_ref, *refs):
    n = MOE_TILES_PER_STEP
    h_refs = refs[0:n]
    wr_ref, brow_ref = refs[n:n + 2]
    w_refs = [refs[n + 2 + 6 * t:n + 2 + 6 * (t + 1)] for t in range(n)]
    o_ref = refs[n + 2 + 6 * n]
    first = pl.program_id(0) * n
    n_used = nt_ref[0]

    @pl.when(first < n_used)
    def _():
        tiles = []
        for t in range(n):
            e_lo = plan_ref[PLAN_E_LO * NT_MOE + first + t]
            e_hi = plan_ref[PLAN_E_HI * NT_MOE + first + t]
            g = plan_ref[PLAN_GROUP * NT_MOE + first + t]
            h = _load_slabs(h_refs[t], TM_MOE).astype(BF16)
            lg = _dot(h, wr_ref[...]) + brow_ref[...]
            wg_lo, wu_lo, _, wg_hi, wu_hi, _ = w_refs[t]
            ups = (_dot(h, wg_lo[...]), _dot(h, wu_lo[...]), _dot(h, wg_hi[...]), _dot(h, wu_hi[...]))
            tiles.append((lg, e_lo, e_hi, g, ups))

        for t, (lg, e_lo, e_hi, g, (a_lo, b_lo, a_hi, b_hi)) in enumerate(tiles):
            lane = lax.broadcasted_iota(I32, lg.shape, 1)
            is_group = lane < N_EXPERT_GROUPS
            gmax = jnp.max(jnp.where(is_group, lg, NEG_INF), axis=1, keepdims=True)
            ge = jnp.exp(lg - gmax)
            g_w = (jnp.sum(jnp.where(lane == g, ge, 0.0), axis=1, keepdims=True)
                   / jnp.sum(jnp.where(is_group, ge, 0.0), axis=1, keepdims=True))
            v_lo = jnp.sum(jnp.where(lane == EXPERT_ROW0 + e_lo, lg, 0.0), axis=1, keepdims=True)
            v_hi = jnp.sum(jnp.where(lane == EXPERT_ROW0 + e_hi, lg, 0.0), axis=1, keepdims=True)
            m = jnp.maximum(v_lo, v_hi)
            x_lo = jnp.exp(v_lo - m)
            x_hi = jnp.exp(v_hi - m)
            w_lo = x_lo / (x_lo + x_hi) * g_w
            w_hi = x_hi / (x_lo + x_hi) * g_w
            hid_lo = (a_lo * _sigmoid(a_lo) * b_lo * w_lo).astype(BF16)
            hid_hi = (a_hi * _sigmoid(a_hi) * b_hi * w_hi).astype(BF16)
            _, _, wd_lo, _, _, wd_hi = w_refs[t]
            y = _dot(hid_lo, wd_lo[...]) + _dot(hid_hi, wd_hi[...])
            if t > 0:
                y = jnp.where(first + t < n_used, y, 0.0)
            for s in range(SLAB):
                o_ref[pl.ds(t * TM_MOE * SLAB + s, TM_MOE, stride=SLAB), :] = y[:, s * 128:(s + 1) * 128]

    @pl.when(first >= n_used)
    def _():
        o_ref[...] = jnp.zeros_like(o_ref)


def _moe_call(plan, n_tiles, h_sorted, wr, br_row, w_gate, w_up, w_down):
    tm = TM_MOE
    n = MOE_TILES_PER_STEP

    def h_spec(t):
        return pl.BlockSpec((tm * SLAB, 128), lambda s, plan, nt: (plan[PLAN_ROW * NT_MOE + s * n + t], 0))

    def w_spec(shape, section, t):
        return pl.BlockSpec((None,) + shape, lambda s, plan, nt: (plan[section * NT_MOE + s * n + t], 0, 0))

    up = (D_MODEL, D_EXPERT)
    down = (D_EXPERT, D_MODEL)
    per_tile_weights = [[w_spec(up, PLAN_E_LO, t), w_spec(up, PLAN_E_LO, t), w_spec(down, PLAN_E_LO, t),
                         w_spec(up, PLAN_E_HI, t), w_spec(up, PLAN_E_HI, t), w_spec(down, PLAN_E_HI, t)]
                        for t in range(n)]
    grid_spec = pltpu.PrefetchScalarGridSpec(
        num_scalar_prefetch=2,
        grid=(NT_MOE // n,),
        in_specs=[h_spec(t) for t in range(n)] + [
            pl.BlockSpec((D_MODEL, 128), lambda s, plan, nt: (0, 0)),
            pl.BlockSpec((1, 128), lambda s, plan, nt: (0, 0)),
        ] + [spec for specs in per_tile_weights for spec in specs],
        out_specs=pl.BlockSpec((n * tm * SLAB, 128), lambda s, plan, nt: (s, 0)),
    )
    weights = [w_gate, w_up, w_down, w_gate, w_up, w_down] * n
    return pl.pallas_call(
        _moe_kernel,
        grid_spec=grid_spec,
        out_shape=jax.ShapeDtypeStruct((SORTED_ROWS * SLAB, 128), F32),
        compiler_params=_params(1),
        name="moe",
    )(plan, n_tiles, *([h_sorted] * n), wr, br_row, *weights)


def _alibi_query_lanes():
    lanes = np.zeros((1, N_HEADS * HEAD_SLOT), np.float32)
    for hd, slope in enumerate(_alibi_slopes()):
        rest = np.float32(slope * LOG2E)
        for part in range(ALIBI_PARTS):
            piece = np.float32(np.asarray(rest).astype(jnp.bfloat16))
            lanes[0, hd * HEAD_SLOT + HEAD_DIM + part] = piece
            rest = np.float32(rest - piece)
    return jnp.asarray(lanes)


def _bucket_tables():
    e_lo, e_hi, grp = [], [], []
    for g in range(N_EXPERT_GROUPS):
        for lo, hi in PAIRS:
            e_lo.append(g * EXPERTS_PER_GROUP + lo)
            e_hi.append(g * EXPERTS_PER_GROUP + hi)
            grp.append(g)
    return jnp.asarray(e_lo + e_hi + grp, I32)


def kernel(x, c, w_ada, b_ada, norm1_gain, w_in, b_branch_gate, q_norm_gain, k_norm_gain, attn_sinks, gmlp_norm_gain, gmlp_norm_bias, gmlp_w_spatial, gmlp_b_spatial, w_o_attn, w_o_gmlp, w_out, norm2_gain, w_group_router, b_group_router, w_expert_router, b_expert_router, w_expert_gate, w_expert_up, w_expert_down):
    depth = w_ada.shape[0]
    x2 = x.reshape(TOKENS, D_MODEL)
    tables = _bucket_tables()
    q_aug = _alibi_query_lanes()
    row = lambda v: v.reshape(1, -1)
    for l in range(depth):
        mod = _mod_call(c, w_ada[l], b_ada[l])
        shift1, scale1, gate1, shift2, scale2, gate2 = [
            m.reshape(BATCH, 1, D_MODEL) for m in jnp.split(mod, 6, axis=-1)]

        slot_pad = jnp.zeros((HEAD_SLOT - HEAD_DIM,), F32)
        q_gain_row = row(jnp.tile(jnp.concatenate([q_norm_gain[l] * (HEAD_DIM ** -0.5 * LOG2E), slot_pad]), N_HEADS))
        k_gain_row = row(jnp.tile(jnp.concatenate([k_norm_gain[l], slot_pad]), N_KV_HEADS))
        w_in_b = _round_call(w_in, l)
        q, k, vt, u, vg, gates = _inproj_call(
            x2, scale1, shift1, row(norm1_gain[l]), w_in_b, w_in_b[:, V0:V1].T,
            q_gain_row, k_gain_row, row(gmlp_norm_gain[l]), row(gmlp_norm_bias[l]), row(b_branch_gate[l]), q_aug)

        y_attn = _attn_call(attn_sinks[l], q, k, vt)

        bias_map = jnp.repeat(gmlp_b_spatial[l].T, GMLP_WIDTH // GMLP_GROUPS, axis=1)
        e0, e1 = EXPERT_ROW0, EXPERT_ROW0 + N_EXPERTS
        wr = jnp.zeros((D_MODEL, 128), F32)
        wr = wr.at[:, 0:N_EXPERT_GROUPS].set(w_group_router[l]).at[:, e0:e1].set(w_expert_router[l]).astype(BF16)
        br_row = jnp.zeros((1, 128), F32)
        br_row = br_row.at[0, 0:N_EXPERT_GROUPS].set(b_group_router[l]).at[0, e0:e1].set(b_expert_router[l])
        x1, h2p, buckets, totals = _mixer_call(
            x2, y_attn, u, vg, gates, gmlp_w_spatial[l], bias_map, _round_call(w_o_attn, l),
            _round_call(w_o_gmlp, l), _round_call(w_out, l), gate1, row(norm2_gain[l]), scale2, shift2,
            wr[:, 0:BUCKET_ROWS].T, br_row[:, 0:BUCKET_ROWS].T)

        pos, meta = _rank_call(buckets, totals)
        pos = pos.reshape(TOKENS // TM_ROW, 1, TM_ROW)
        h_sorted = _scatter_call(pos, meta, h2p)

        flat = lambda w: w.reshape((N_EXPERTS,) + w.shape[2:]).astype(BF16)
        n_tiles = meta[META_N_TILES, 0:1]
        y_sorted = _moe_call(
            _tile_plan(meta[META_TILE_BUCKET], n_tiles, tables), n_tiles, h_sorted, wr, br_row,
            flat(w_expert_gate[l]), flat(w_expert_up[l]), flat(w_expert_down[l]))

        x2 = _gather_call(pos, y_sorted, x1, gate2)
    return x2.reshape(x.shape)
```

```python
import functools

import jax
import jax.numpy as jnp
import numpy as np
from jax import lax
from jax.experimental import pallas as pl
from jax.experimental.pallas import tpu as pltpu

D_MODEL = 1024
BATCH = 8
SEQ = 4096
TOKENS = BATCH * SEQ
N_HEADS = 16
N_KV_HEADS = 4
HEAD_DIM = 64
Q_PER_KV = N_HEADS // N_KV_HEADS
BLOCK = 128
HEAD_SLOT = 128
ALIBI_PARTS = 3
ATTN_WIDTH = N_HEADS * HEAD_DIM
KV_WIDTH = N_KV_HEADS * HEAD_DIM
GMLP_WIDTH = 1024
GMLP_GROUPS = 8
GMLP_CHUNK = 128
N_EXPERT_GROUPS = 4
EXPERTS_PER_GROUP = 4
N_EXPERTS = N_EXPERT_GROUPS * EXPERTS_PER_GROUP
D_EXPERT = 512
IN_WIDTH = ATTN_WIDTH + 2 * KV_WIDTH + 2 * GMLP_WIDTH + 2 * D_MODEL
EPS = 1e-6
NEG_INF = -1e30
LOG2E = float(np.float32(np.log2(np.e)))

Q0, Q1 = 0, ATTN_WIDTH
K0, K1 = Q1, Q1 + KV_WIDTH
V0, V1 = K1, K1 + KV_WIDTH
U0, U1 = V1, V1 + GMLP_WIDTH
G0, G1 = U1, U1 + GMLP_WIDTH
B0, B1 = G1, G1 + 2 * D_MODEL

PAIRS = ((0, 1), (0, 2), (0, 3), (1, 2), (1, 3), (2, 3))
N_BUCKETS = N_EXPERT_GROUPS * len(PAIRS)
BUCKET_ROWS = 32
EXPERT_ROW0 = 8

TM_PROJ = 1024
PROJ_ROWS = 512
PROJ_CHUNK = 512
TM_ATTN = 1024
TM_MIX = 1024
MIX_ROWS = 512
TM_MOE = 256
MOE_TILES_PER_STEP = 2
TM_ROW = 512
SCORES_AHEAD = 6
ROW_UNROLL = 16
NT_MOE = TOKENS // TM_MOE + N_BUCKETS
SORTED_ROWS = NT_MOE * TM_MOE
SLAB = D_MODEL // 128
META_LANES = 256
assert NT_MOE <= META_LANES

VMEM_LIMIT = 56 * 1024 * 1024

F32 = jnp.float32
BF16 = jnp.bfloat16
U32 = jnp.uint32
I32 = jnp.int32


def _alibi_slopes():
    return [float(np.float32(2.0 ** (-8.0 * (i + 1) / N_HEADS))) for i in range(N_HEADS)]


def _params(n_axes):
    return pltpu.CompilerParams(dimension_semantics=("arbitrary",) * n_axes, vmem_limit_bytes=VMEM_LIMIT)


def _sigmoid(x):
    return 1.0 / (1.0 + jnp.exp(-x))


def _gelu(x):
    return 0.5 * x * (1.0 + lax.erf(x * np.float32(1.0 / np.sqrt(2.0))))


def _dot(a, b):
    return jnp.dot(a, b, preferred_element_type=F32)


def _dot_nt(a, b):
    return lax.dot_general(a, b, (((1,), (1,)), ((), ())), preferred_element_type=F32)


def _mod_kernel(c_ref, w_ref, b_ref, o_ref):
    c = c_ref[...]
    ca = c * _sigmoid(c)
    o_ref[...] = _dot(ca.astype(BF16), w_ref[...].astype(BF16)) + b_ref[...]


def _mod_call(c, w_ada, b_ada):
    n = w_ada.shape[1]
    bn = 1536
    return pl.pallas_call(
        _mod_kernel,
        grid=(n // bn,),
        in_specs=[
            pl.BlockSpec((BATCH, D_MODEL), lambda j: (0, 0)),
            pl.BlockSpec((D_MODEL, bn), lambda j: (0, j)),
            pl.BlockSpec((1, bn), lambda j: (0, j)),
        ],
        out_specs=pl.BlockSpec((BATCH, bn), lambda j: (0, j)),
        out_shape=jax.ShapeDtypeStruct((BATCH, n), F32),
        compiler_params=_params(1),
        name="mod",
    )(c, w_ada, b_ada.reshape(1, n))


def _round_kernel(w_ref, o_ref):
    o_ref[...] = w_ref[...].astype(BF16)


def _round_call(w, layer):
    _, k, n = w.shape
    bn = 512
    return pl.pallas_call(
        _round_kernel,
        grid=(n // bn,),
        in_specs=[pl.BlockSpec((None, k, bn), lambda j: (layer, 0, j))],
        out_specs=pl.BlockSpec((k, bn), lambda j: (0, j)),
        out_shape=jax.ShapeDtypeStruct((k, n), BF16),
        compiler_params=_params(1),
        name="round_weight",
    )(w)


def _inproj_kernel(x_ref, sc_ref, sh_ref, ng_ref, w_ref, wvt_ref, qg_ref, kg_ref, lng_ref, lnb_ref, bg_ref, qaug_ref,
                   q_o, k_o, vt_o, u_o, vg_o, gt_o):
    low_half = lax.broadcasted_iota(I32, (1, HEAD_SLOT), 1) < HEAD_DIM
    for r0 in range(0, TM_PROJ, PROJ_ROWS):
        _project_rows(slice(r0, r0 + PROJ_ROWS), low_half, x_ref, sc_ref, sh_ref, ng_ref, w_ref, wvt_ref, qg_ref,
                      kg_ref, lng_ref, lnb_ref, bg_ref, qaug_ref, q_o, k_o, vt_o, u_o, vg_o, gt_o)


def _project_rows(rows, low_half, x_ref, sc_ref, sh_ref, ng_ref, w_ref, wvt_ref, qg_ref, kg_ref, lng_ref, lnb_ref,
                  bg_ref, qaug_ref, q_o, k_o, vt_o, u_o, vg_o, gt_o):
    x = x_ref[rows, :]
    ms = jnp.mean(x * x, axis=-1, keepdims=True)
    h = (x * lax.rsqrt(ms + EPS)) * ng_ref[...]
    h = h * (1.0 + sc_ref[0]) + sh_ref[0]
    hb = h.astype(BF16)

    def proj(c0, c1):
        return _dot(hb, w_ref[:, c0:c1])

    def store_normed_heads(raw, gain_ref, spare_ref, o_ref):
        for p in range(raw.shape[1] // HEAD_SLOT):
            pair = raw[:, p * HEAD_SLOT:(p + 1) * HEAD_SLOT]
            for hd, head in ((2 * p, pair), (2 * p + 1, pltpu.roll(pair, HEAD_DIM, 1))):
                cols = slice(hd * HEAD_SLOT, (hd + 1) * HEAD_SLOT)
                xh = jnp.where(low_half, head, 0.0)
                r = lax.rsqrt(jnp.sum(xh * xh, axis=-1, keepdims=True) * (1.0 / HEAD_DIM) + EPS)
                y = xh * r * gain_ref[:, cols]
                if spare_ref is not None:
                    y = y + spare_ref[:, cols]
                o_ref[rows, cols] = y.astype(BF16)

    store_normed_heads(proj(Q0, Q1), qg_ref, qaug_ref, q_o)
    store_normed_heads(proj(K0, K1), kg_ref, None, k_o)
    vt_o[:, rows] = _dot_nt(wvt_ref[...], hb).astype(BF16)

    cw = PROJ_CHUNK

    def gate_chunk(c):
        cols = slice(c * cw, (c + 1) * cw)
        gt_o[rows, cols] = _sigmoid(proj(B0 + c * cw, B0 + (c + 1) * cw) + bg_ref[:, cols]).astype(BF16)

    vg_parts = []
    for c in range(GMLP_WIDTH // cw):
        u_o[rows, c * cw:(c + 1) * cw] = _gelu(proj(U0 + c * cw, U0 + (c + 1) * cw)).astype(BF16)
        gate_chunk(2 * c)
        vg_parts.append(_gelu(proj(G0 + c * cw, G0 + (c + 1) * cw)))
        gate_chunk(2 * c + 1)
    vg = jnp.concatenate(vg_parts, axis=1)
    mu = jnp.mean(vg, axis=-1, keepdims=True)
    vc = vg - mu
    var = jnp.mean(vc * vc, axis=-1, keepdims=True)
    vg_o[rows, :] = (vc * lax.rsqrt(var + EPS) * lng_ref[...] + lnb_ref[...]).astype(BF16)


def _inproj_call(x2, scale1, shift1, norm_gain, w_in, wv_t, q_gain_row, k_gain_row, ln_gain, ln_bias, b_gate, q_aug):
    tm = TM_PROJ
    per_batch = SEQ // tm
    row = lambda n: pl.BlockSpec((1, n), lambda i: (0, 0))
    modrow = pl.BlockSpec((1, 1, D_MODEL), lambda i: (i // per_batch, 0, 0))
    tile = lambda n: pl.BlockSpec((tm, n), lambda i: (i, 0))
    out = lambda n: jax.ShapeDtypeStruct((TOKENS, n), BF16)
    return pl.pallas_call(
        _inproj_kernel,
        grid=(TOKENS // tm,),
        in_specs=[
            tile(D_MODEL), modrow, modrow, row(D_MODEL),
            pl.BlockSpec((D_MODEL, IN_WIDTH), lambda i: (0, 0)),
            pl.BlockSpec((KV_WIDTH, D_MODEL), lambda i: (0, 0)),
            row(N_HEADS * HEAD_SLOT), row(N_KV_HEADS * HEAD_SLOT), row(GMLP_WIDTH), row(GMLP_WIDTH),
            row(2 * D_MODEL), row(N_HEADS * HEAD_SLOT),
        ],
        out_specs=[tile(N_HEADS * HEAD_SLOT), tile(N_KV_HEADS * HEAD_SLOT),
                   pl.BlockSpec((KV_WIDTH, tm), lambda i: (0, i)),
                   tile(GMLP_WIDTH), tile(GMLP_WIDTH), tile(2 * D_MODEL)],
        out_shape=[out(N_HEADS * HEAD_SLOT), out(N_KV_HEADS * HEAD_SLOT),
                   jax.ShapeDtypeStruct((KV_WIDTH, TOKENS), BF16),
                   out(GMLP_WIDTH), out(GMLP_WIDTH), out(2 * D_MODEL)],
        compiler_params=_params(1),
        name="inproj",
    )(x2, scale1, shift1, norm_gain, w_in, wv_t, q_gain_row, k_gain_row, ln_gain, ln_bias, b_gate, q_aug)


def _attn_kernel(sink_ref, q_ref, kc_ref, kp_ref, vc_ref, vp_ref, o_ref):
    i = pl.program_id(1)
    slopes = _alibi_slopes()
    kj = lax.broadcasted_iota(I32, (BLOCK, 2 * BLOCK), 0)
    qi = lax.broadcasted_iota(I32, (BLOCK, 2 * BLOCK), 1) & (BLOCK - 1)
    own = kj <= qi
    from_own = jnp.where(own, 1.0, 0.0).astype(BF16)
    from_prev = jnp.where(own, 0.0, 1.0).astype(BF16)
    has_prev = i > 0
    low_half = lax.broadcasted_iota(I32, (1, HEAD_SLOT), 1) < HEAD_DIM
    k_lane = lax.broadcasted_iota(I32, (2 * BLOCK, HEAD_SLOT), 1)
    k_aug = jnp.where((k_lane >= HEAD_DIM) & (k_lane < HEAD_DIM + ALIBI_PARTS),
                      lax.broadcasted_iota(I32, (2 * BLOCK, HEAD_SLOT), 0), 0).astype(F32).astype(BF16)
    q_pos = (lax.broadcasted_iota(I32, (1, BLOCK), 1) + BLOCK).astype(F32)

    keys_of = {}

    def scores(item):
        sb, kv, pr = item
        r0, r1 = sb * BLOCK, (sb + 1) * BLOCK
        if (sb, kv) not in keys_of:
            cols = slice(kv * HEAD_SLOT, (kv + 1) * HEAD_SLOT)
            k_prev = kp_ref[:, cols] if sb == 0 else kc_ref[r0 - BLOCK:r0, cols]
            keys_of[sb, kv] = jnp.where(low_half, jnp.concatenate([k_prev, kc_ref[r0:r1, cols]], axis=0), k_aug)
        ha = kv * Q_PER_KV + 2 * pr
        queries = jnp.concatenate([q_ref[r0:r1, ha * HEAD_SLOT:(ha + 1) * HEAD_SLOT],
                                   q_ref[r0:r1, (ha + 1) * HEAD_SLOT:(ha + 2) * HEAD_SLOT]], axis=0)
        s = _dot_nt(keys_of[sb, kv], queries)
        s_prev = s[0:BLOCK, :]
        if sb == 0:
            s_prev = jnp.where(has_prev, s_prev, NEG_INF)
        return jnp.where(own, s[BLOCK:2 * BLOCK, :], s_prev)

    def attend(item, s):
        sb, kv, pr = item
        r0, r1 = sb * BLOCK, (sb + 1) * BLOCK
        rows = slice(kv * HEAD_DIM, (kv + 1) * HEAD_DIM)
        v_prev = vp_ref[rows, :] if sb == 0 else vc_ref[rows, r0 - BLOCK:r0]
        vt = jnp.concatenate([v_prev, vc_ref[rows, r0:r1]], axis=1)
        ha = kv * Q_PER_KV + 2 * pr
        hb = ha + 1
        sink = LOG2E * jnp.concatenate(
            [sink_ref[ha] + slopes[ha] * q_pos, sink_ref[hb] + slopes[hb] * q_pos], axis=1)
        m = jnp.maximum(jnp.max(s, axis=0, keepdims=True), sink)
        p = jnp.exp2(s - m)
        den = jnp.sum(p, axis=0, keepdims=True) + jnp.exp2(sink - m)
        pb = p.astype(BF16)
        p_keys = jnp.concatenate([pb * from_prev, pb * from_own], axis=0)
        o = _dot(vt, p_keys) / den
        pair = jnp.concatenate([o[:, 0:BLOCK], o[:, BLOCK:2 * BLOCK]], axis=0).T
        o_ref[r0:r1, ha * HEAD_DIM:(hb + 1) * HEAD_DIM] = pair.astype(BF16)

    items = [(sb, kv, pr) for sb in range(TM_ATTN // BLOCK) for kv in range(N_KV_HEADS)
             for pr in range(Q_PER_KV // 2)]
    pending = []
    for n in range(len(items) + SCORES_AHEAD):
        if n < len(items):
            pending.append(scores(items[n]))
        if n >= SCORES_AHEAD:
            attend(items[n - SCORES_AHEAD], pending.pop(0))


def _attn_call(sinks, q, k, vt):
    tq = TM_ATTN
    per_batch = SEQ // tq
    blocks_per_tile = tq // BLOCK
    q_width = N_HEADS * HEAD_SLOT
    k_width = N_KV_HEADS * HEAD_SLOT
    cur = lambda n: pl.BlockSpec((tq, n), lambda b, i: (b * per_batch + i, 0))
    prev_block = lambda b, i: b * (SEQ // BLOCK) + jnp.maximum(i * blocks_per_tile - 1, 0)
    return pl.pallas_call(
        _attn_kernel,
        grid=(BATCH, per_batch),
        in_specs=[
            pl.BlockSpec(memory_space=pltpu.SMEM),
            cur(q_width), cur(k_width),
            pl.BlockSpec((BLOCK, k_width), lambda b, i: (prev_block(b, i), 0)),
            pl.BlockSpec((KV_WIDTH, tq), lambda b, i: (0, b * per_batch + i)),
            pl.BlockSpec((KV_WIDTH, BLOCK), lambda b, i: (0, prev_block(b, i))),
        ],
        out_specs=cur(ATTN_WIDTH),
        out_shape=jax.ShapeDtypeStruct((TOKENS, ATTN_WIDTH), BF16),
        compiler_params=_params(2),
        name="attn",
    )(sinks, q, k, k, vt, vt)


def _first_argmax(vals):
    m = vals[0]
    for v in vals[1:]:
        m = jnp.maximum(m, v)
    idx = jnp.full(m.shape, len(vals) - 1, I32)
    for k in range(len(vals) - 2, -1, -1):
        idx = jnp.where(vals[k] == m, k, idx)
    return m, idx


def _mixer_kernel(x_ref, ya_ref, u_ref, vg_ref, gt_ref, ws_ref, bmap_ref, woa_ref, wog_ref, wout_ref, gate1_ref,
                  ng_ref, sc_ref, sh_ref, wrt_ref, brt_ref, x1_o, h2_o, bk_o, cnt_o, cnt):
    tm = TM_MIX
    t_idx = lax.broadcasted_iota(I32, (GMLP_CHUNK, GMLP_CHUNK), 0)
    s_idx = lax.broadcasted_iota(I32, (GMLP_CHUNK, GMLP_CHUNK), 1)
    ws = [jnp.where(t_idx >= s_idx, ws_ref[g], 0.0).astype(BF16) for g in range(GMLP_GROUPS)]
    gc = GMLP_WIDTH // GMLP_GROUPS

    def mix_rows(r0):
        rows = slice(r0, r0 + MIX_ROWS)
        chunks = []
        for c0 in range(r0, r0 + MIX_ROWS, GMLP_CHUNK):
            c1 = c0 + GMLP_CHUNK
            mixed = jnp.concatenate(
                [_dot(ws[g], vg_ref[c0:c1, g * gc:(g + 1) * gc]) for g in range(GMLP_GROUPS)], axis=1)
            mixed = mixed + bmap_ref[...]
            chunks.append((u_ref[c0:c1, :].astype(F32) * mixed).astype(BF16))
        y_gmlp = jnp.concatenate(chunks, axis=0)
        pa = _dot(ya_ref[rows, :], woa_ref[...])
        pg = _dot(y_gmlp, wog_ref[...])
        merged = gt_ref[rows, 0:D_MODEL].astype(F32) * pa + gt_ref[rows, D_MODEL:2 * D_MODEL].astype(F32) * pg
        x1 = x_ref[rows, :] + gate1_ref[0] * _dot(merged.astype(BF16), wout_ref[...])
        x1_o[rows, :] = x1

        ms = jnp.mean(x1 * x1, axis=-1, keepdims=True)
        h2 = (x1 * lax.rsqrt(ms + EPS)) * ng_ref[...]
        h2 = h2 * (1.0 + sc_ref[0]) + sh_ref[0]
        for s in range(SLAB):
            h2_o[pl.ds(r0 * SLAB + s, MIX_ROWS, stride=SLAB), :] = h2[:, s * 128:(s + 1) * 128]

        lg = _dot_nt(wrt_ref[...], h2.astype(BF16)) + brt_ref[...]
        _, gi = _first_argmax([lg[r:r + 1, :] for r in range(N_EXPERT_GROUPS)])
        el = [lg[EXPERT_ROW0 + r:EXPERT_ROW0 + r + 1, :] for r in range(N_EXPERTS)]
        eg = []
        for k in range(EXPERTS_PER_GROUP):
            v = el[(N_EXPERT_GROUPS - 1) * EXPERTS_PER_GROUP + k]
            for g in range(N_EXPERT_GROUPS - 2, -1, -1):
                v = jnp.where(gi == g, el[g * EXPERTS_PER_GROUP + k], v)
            eg.append(v)
        _, i1 = _first_argmax(eg)
        _, i2 = _first_argmax([jnp.where(i1 == k, -3e38, eg[k]) for k in range(EXPERTS_PER_GROUP)])
        lo = jnp.minimum(i1, i2)
        hi = jnp.maximum(i1, i2)
        pair = jnp.where(lo == 0, hi - 1, jnp.where(lo == 1, hi + 1, 5))
        bucket = gi * len(PAIRS) + pair
        bk_o[0, :, rows] = bucket
        in_bucket = lax.broadcasted_iota(I32, (BUCKET_ROWS, MIX_ROWS), 0) == bucket
        return jnp.sum(jnp.where(in_bucket, 1.0, 0.0), axis=1, keepdims=True)

    counts = [mix_rows(r0) for r0 in range(0, tm, MIX_ROWS)]

    @pl.when(pl.program_id(0) == 0)
    def _():
        cnt[...] = jnp.zeros_like(cnt)
    cnt[...] += sum(counts[1:], counts[0])
    cnt_o[...] = cnt[...]


def _mixer_call(x2, y_attn, u, vg, gates, w_spatial, bias_map, wo_attn, wo_gmlp, w_out, gate1, norm_gain, scale2, shift2,
                wr_t, br_t):
    tm = TM_MIX
    per_batch = SEQ // tm
    nt = TOKENS // tm
    tile = lambda n: pl.BlockSpec((tm, n), lambda i: (i, 0))
    full = lambda shape: pl.BlockSpec(shape, lambda i: (0,) * len(shape))
    once = lambda shape: pl.BlockSpec(shape, lambda i: (0,) * len(shape), pipeline_mode=pl.Buffered(1))
    modrow = pl.BlockSpec((1, 1, D_MODEL), lambda i: (i // per_batch, 0, 0))
    return pl.pallas_call(
        _mixer_kernel,
        grid=(nt,),
        in_specs=[
            tile(D_MODEL), tile(ATTN_WIDTH), tile(GMLP_WIDTH), tile(GMLP_WIDTH), tile(2 * D_MODEL),
            full((GMLP_GROUPS, GMLP_CHUNK, GMLP_CHUNK)), full((GMLP_CHUNK, GMLP_WIDTH)),
            once((ATTN_WIDTH, D_MODEL)), once((GMLP_WIDTH, D_MODEL)), once((D_MODEL, D_MODEL)),
            modrow, full((1, D_MODEL)), modrow, modrow,
            full((BUCKET_ROWS, D_MODEL)), full((BUCKET_ROWS, 1)),
        ],
        out_specs=[tile(D_MODEL), pl.BlockSpec((tm * SLAB, 128), lambda i: (i, 0)),
                   pl.BlockSpec((1, 1, tm), lambda i: (i, 0, 0)), full((BUCKET_ROWS, 128))],
        out_shape=[
            jax.ShapeDtypeStruct((TOKENS, D_MODEL), F32),
            jax.ShapeDtypeStruct((TOKENS * SLAB, 128), F32),
            jax.ShapeDtypeStruct((nt, 1, tm), I32),
            jax.ShapeDtypeStruct((BUCKET_ROWS, 128), F32),
        ],
        scratch_shapes=[pltpu.VMEM((BUCKET_ROWS, 128), F32)],
        compiler_params=_params(1),
        name="mixer",
    )(x2, y_attn, u, vg, gates, w_spatial, bias_map, wo_attn, wo_gmlp, w_out, gate1, norm_gain, scale2, shift2,
      wr_t, br_t)


META_TILE_BUCKET, META_N_TILES, META_PAD_START, META_PAD_END = 0, 1, 2, 3


def _rank_kernel(bk_ref, total_ref, pos_o, meta_o, seen, off):
    tm = TM_MIX
    i = pl.program_id(0)
    bk = bk_ref[0]
    onehot = lax.broadcasted_iota(I32, (BUCKET_ROWS, tm), 0) == bk

    @pl.when(i == 0)
    def _():
        total = total_ref[...]
        tiles = jnp.floor((total + (TM_MOE - 1)) * (1.0 / TM_MOE))
        r = lax.broadcasted_iota(I32, (BUCKET_ROWS, BUCKET_ROWS), 0)
        c = lax.broadcasted_iota(I32, (BUCKET_ROWS, BUCKET_ROWS), 1)
        before = jnp.where(c < r, 1.0, 0.0).astype(BF16)
        first_tile = _dot(before, tiles.astype(BF16))
        first_row = first_tile * TM_MOE
        off[...] = first_row
        seen[...] = jnp.zeros_like(seen)
        end_tile = (first_tile + tiles)[:, 0:1]
        lane = lax.broadcasted_iota(I32, (BUCKET_ROWS, META_LANES), 1)
        bsub = lax.broadcasted_iota(I32, (BUCKET_ROWS, META_LANES), 0)
        is_bucket = bsub < N_BUCKETS
        tile_bucket = jnp.sum(jnp.where((lane.astype(F32) >= end_tile) & is_bucket, 1.0, 0.0), axis=0, keepdims=True)
        as_row = lambda col: jnp.sum(jnp.where((bsub == lane) & is_bucket, col, 0.0), axis=0, keepdims=True)
        n_tiles = jnp.sum(jnp.where(bsub == N_BUCKETS - 1, end_tile, 0.0), axis=0, keepdims=True)
        pad_start = as_row(end_tile - jnp.minimum(tiles[:, 0:1], 1.0))
        pad_end = as_row(end_tile)
        row = lax.broadcasted_iota(I32, (8, META_LANES), 0)
        meta = jnp.zeros((8, META_LANES), F32)
        for k, v in ((META_TILE_BUCKET, tile_bucket), (META_N_TILES, n_tiles), (META_PAD_START, pad_start),
                     (META_PAD_END, pad_end)):
            meta = jnp.where(row == k, v, meta)
        meta_o[...] = meta.astype(I32)

    s_idx = lax.broadcasted_iota(I32, (tm, tm), 0)
    t_idx = lax.broadcasted_iota(I32, (tm, tm), 1)
    upto = jnp.where(s_idx <= t_idx, 1.0, 0.0).astype(BF16)
    incl = _dot(jnp.where(onehot, 1.0, 0.0).astype(BF16), upto)
    base = off[:, 0:1] + seen[:, 0:1]
    posf = jnp.sum(jnp.where(onehot, base + incl - 1.0, 0.0), axis=0, keepdims=True)
    pos_o[0] = posf.astype(I32)
    seen[...] += incl[:, tm - 1:tm]


def _rank_call(buckets, totals):
    nt, _, tm = buckets.shape
    return pl.pallas_call(
        _rank_kernel,
        grid=(nt,),
        in_specs=[pl.BlockSpec((1, 1, tm), lambda i: (i, 0, 0)),
                  pl.BlockSpec((BUCKET_ROWS, 128), lambda i: (0, 0))],
        out_specs=[
            pl.BlockSpec((1, 1, tm), lambda i: (i, 0, 0)),
            pl.BlockSpec((8, META_LANES), lambda i: (0, 0)),
        ],
        out_shape=[
            jax.ShapeDtypeStruct((nt, 1, tm), I32),
            jax.ShapeDtypeStruct((8, META_LANES), I32),
        ],
        scratch_shapes=[pltpu.VMEM((BUCKET_ROWS, 128), F32), pltpu.VMEM((BUCKET_ROWS, 128), F32)],
        compiler_params=_params(1),
        name="rank",
    )(buckets, totals)


def _store_slabs(ref, x):
    n = x.shape[0]
    for s in range(SLAB):
        ref[pl.ds(s, n, stride=SLAB), :] = x[:, s * 128:(s + 1) * 128]


def _load_slabs(ref, n):
    return jnp.concatenate([ref[pl.ds(s, n, stride=SLAB), :] for s in range(SLAB)], axis=1)


def _row_copy(src, src_row, dst, dst_row, sem):
    return pltpu.make_async_copy(src.at[pl.ds(pl.multiple_of(src_row * SLAB, SLAB), SLAB)],
                                 dst.at[pl.ds(pl.multiple_of(dst_row * SLAB, SLAB), SLAB)], sem)


def _start_tile_rows(copy_of_row):
    def trip(c, carry):
        for k in range(ROW_UNROLL):
            copy_of_row(c * ROW_UNROLL + k).start(priority=k % 2)
        return carry
    lax.fori_loop(0, TM_ROW // ROW_UNROLL, trip, 0)


def _tile_wait(src, dst, sem):
    pltpu.make_async_copy(src.at[pl.ds(0, TM_ROW * SLAB)], dst, sem).wait()


def _scatter_kernel(pos_ref, meta_ref, h_ref, wg_ref, wu_ref, wd_ref, o_ref, wg_o, wu_o, wd_o, zero_tile, sem,
                    pad_sem):
    for src, dst in ((wg_ref, wg_o), (wu_ref, wu_o), (wd_ref, wd_o)):
        dst[...] = src[...].astype(BF16)

    @pl.when(pl.program_id(0) == 0)
    def _():
        zero_tile[...] = jnp.zeros_like(zero_tile)

        def tile_copy(j):
            rows = pl.ds(pl.multiple_of(j * (TM_MOE * SLAB), TM_MOE * SLAB), TM_MOE * SLAB)
            return pltpu.make_async_copy(zero_tile, o_ref.at[rows], pad_sem)

        def for_all_zero_copies(act):
            def per_bucket(b, carry):
                lo, hi = meta_ref[META_PAD_START, b], meta_ref[META_PAD_END, b]
                return lax.fori_loop(lo, hi, lambda j, c: (act(tile_copy(j)), c)[1], carry)
            lax.fori_loop(0, N_BUCKETS, per_bucket, 0)
            lax.fori_loop(meta_ref[META_N_TILES, 0], NT_MOE, lambda j, c: (act(tile_copy(j)), c)[1], 0)

        for_all_zero_copies(lambda cp: cp.start())
        for_all_zero_copies(lambda cp: cp.wait())

    _start_tile_rows(lambda r: _row_copy(h_ref, r, o_ref, pos_ref[0, 0, r], sem))
    _tile_wait(h_ref, o_ref.at[pl.ds(0, TM_ROW * SLAB)], sem)


def _scatter_call(pos, meta, h2, w_gate, w_up, w_down):
    tm = TM_ROW
    steps = TOKENS // tm
    w_in = [w_gate, w_up, w_down]
    w_blocks = [pl.BlockSpec((w.shape[0] // steps, w.shape[1]), lambda i: (i, 0)) for w in w_in]
    return pl.pallas_call(
        _scatter_kernel,
        grid=(steps,),
        in_specs=[
            pl.BlockSpec((1, 1, tm), lambda i: (i, 0, 0), memory_space=pltpu.SMEM),
            pl.BlockSpec(memory_space=pltpu.SMEM),
            pl.BlockSpec((tm * SLAB, 128), lambda i: (i, 0)),
        ] + w_blocks,
        out_specs=[pl.BlockSpec(memory_space=pl.ANY)] + w_blocks,
        out_shape=[jax.ShapeDtypeStruct((SORTED_ROWS * SLAB, 128), F32)]
        + [jax.ShapeDtypeStruct(w.shape, BF16) for w in w_in],
        scratch_shapes=[pltpu.VMEM((TM_MOE * SLAB, 128), F32), pltpu.SemaphoreType.DMA(()),
                        pltpu.SemaphoreType.DMA(())],
        compiler_params=_params(1),
        name="scatter",
    )(pos, meta, h2, *w_in)


def _gather_kernel(pos_ref, pos_next_ref, ys_ref, x1_ref, gate2_ref, o_ref, buf, sem):
    i = pl.program_id(0)
    n = pl.num_programs(0)
    slot = lax.rem(i, 2)

    def fetch(p_ref, s):
        _start_tile_rows(lambda r: _row_copy(ys_ref, p_ref[0, 0, r], buf.at[s], r, sem.at[s]))

    @pl.when(i == 0)
    def _():
        fetch(pos_ref, 0)

    @pl.when(i + 1 < n)
    def _():
        fetch(pos_next_ref, 1 - slot)

    _tile_wait(ys_ref, buf.at[slot], sem.at[slot])
    o_ref[...] = x1_ref[...] + gate2_ref[0] * _load_slabs(buf.at[slot], TM_ROW)


def _gather_call(pos, y_sorted, x1, gate2):
    tm = TM_ROW
    per_batch = SEQ // tm
    n = TOKENS // tm
    return pl.pallas_call(
        _gather_kernel,
        grid=(n,),
        in_specs=[
            pl.BlockSpec((1, 1, tm), lambda i: (i, 0, 0), memory_space=pltpu.SMEM),
            pl.BlockSpec((1, 1, tm), lambda i: (jnp.minimum(i + 1, n - 1), 0, 0), memory_space=pltpu.SMEM),
            pl.BlockSpec(memory_space=pl.ANY),
            pl.BlockSpec((tm, D_MODEL), lambda i: (i, 0)),
            pl.BlockSpec((1, 1, D_MODEL), lambda i: (i // per_batch, 0, 0)),
        ],
        out_specs=pl.BlockSpec((tm, D_MODEL), lambda i: (i, 0)),
        out_shape=jax.ShapeDtypeStruct((TOKENS, D_MODEL), F32),
        scratch_shapes=[pltpu.VMEM((2, tm * SLAB, 128), F32), pltpu.SemaphoreType.DMA((2,))],
        compiler_params=_params(1),
        name="gather",
    )(pos, pos, y_sorted, x1, gate2)


PLAN_ROW, PLAN_E_LO, PLAN_E_HI, PLAN_GROUP = 0, 1, 2, 3


def _tile_plan(tile_bucket, n_tiles, tables):
    row = jnp.minimum(jnp.arange(NT_MOE, dtype=I32), n_tiles[0] - 1)
    bucket = tile_bucket[row]
    return jnp.concatenate([row, tables[bucket], tables[N_BUCKETS + bucket], tables[2 * N_BUCKETS + bucket]])


def _moe_kernel(plan_ref, nt_ref, *refs):
    n = MOE_TILES_PER_STEP
    h_refs = refs[0:n]
    wr_ref, brow_ref = refs[n:n + 2]
    w_refs = [refs[n + 2 + 6 * t:n + 2 + 6 * (t + 1)] for t in range(n)]
    o_ref = refs[n + 2 + 6 * n]
    first = pl.program_id(0) * n
    n_used = nt_ref[0]

    @pl.when(first < n_used)
    def _():
        tiles = []
        for t in range(n):
            e_lo = plan_ref[PLAN_E_LO * NT_MOE + first + t]
            e_hi = plan_ref[PLAN_E_HI * NT_MOE + first + t]
            g = plan_ref[PLAN_GROUP * NT_MOE + first + t]
            h = _load_slabs(h_refs[t], TM_MOE).astype(BF16)
            lg = _dot(h, wr_ref[...]) + brow_ref[...]
            wg_lo, wu_lo, _, wg_hi, wu_hi, _ = w_refs[t]
            ups = (_dot(h, wg_lo[...]), _dot(h, wu_lo[...]), _dot(h, wg_hi[...]), _dot(h, wu_hi[...]))
            tiles.append((lg, e_lo, e_hi, g, ups))

        for t, (lg, e_lo, e_hi, g, (a_lo, b_lo, a_hi, b_hi)) in enumerate(tiles):
            lane = lax.broadcasted_iota(I32, lg.shape, 1)
            is_group = lane < N_EXPERT_GROUPS
            gmax = jnp.max(jnp.where(is_group, lg, NEG_INF), axis=1, keepdims=True)
            ge = jnp.exp(lg - gmax)
            g_w = (jnp.sum(jnp.where(lane == g, ge, 0.0), axis=1, keepdims=True)
                   / jnp.sum(jnp.where(is_group, ge, 0.0), axis=1, keepdims=True))
            v_lo = jnp.sum(jnp.where(lane == EXPERT_ROW0 + e_lo, lg, 0.0), axis=1, keepdims=True)
            v_hi = jnp.sum(jnp.where(lane == EXPERT_ROW0 + e_hi, lg, 0.0), axis=1, keepdims=True)
            m = jnp.maximum(v_lo, v_hi)
            x_lo = jnp.exp(v_lo - m)
            x_hi = jnp.exp(v_hi - m)
            w_lo = x_lo / (x_lo + x_hi) * g_w
            w_hi = x_hi / (x_lo + x_hi) * g_w
            hid_lo = (a_lo * _sigmoid(a_lo) * b_lo * w_lo).astype(BF16)
            hid_hi = (a_hi * _sigmoid(a_hi) * b_hi * w_hi).astype(BF16)
            _, _, wd_lo, _, _, wd_hi = w_refs[t]
            y = _dot(hid_lo, wd_lo[...]) + _dot(hid_hi, wd_hi[...])
            if t > 0:
                y = jnp.where(first + t < n_used, y, 0.0)
            for s in range(SLAB):
                o_ref[pl.ds(t * TM_MOE * SLAB + s, TM_MOE, stride=SLAB), :] = y[:, s * 128:(s + 1) * 128]

    @pl.when(first >= n_used)
    def _():
        o_ref[...] = jnp.zeros_like(o_ref)


def _moe_call(plan, n_tiles, h_sorted, wr, br_row, w_gate, w_up, w_down):
    tm = TM_MOE
    n = MOE_TILES_PER_STEP

    def h_spec(t):
        return pl.BlockSpec((tm * SLAB, 128), lambda s, plan, nt: (plan[PLAN_ROW * NT_MOE + s * n + t], 0))

    def w_spec(shape, section, t):
        return pl.BlockSpec((None,) + shape, lambda s, plan, nt: (plan[section * NT_MOE + s * n + t], 0, 0))

    up = (D_MODEL, D_EXPERT)
    down = (D_EXPERT, D_MODEL)
    per_tile_weights = [[w_spec(up, PLAN_E_LO, t), w_spec(up, PLAN_E_LO, t), w_spec(down, PLAN_E_LO, t),
                         w_spec(up, PLAN_E_HI, t), w_spec(up, PLAN_E_HI, t), w_spec(down, PLAN_E_HI, t)]
                        for t in range(n)]
    grid_spec = pltpu.PrefetchScalarGridSpec(
        num_scalar_prefetch=2,
        grid=(NT_MOE // n,),
        in_specs=[h_spec(t) for t in range(n)] + [
            pl.BlockSpec((D_MODEL, 128), lambda s, plan, nt: (0, 0)),
            pl.BlockSpec((1, 128), lambda s, plan, nt: (0, 0)),
        ] + [spec for specs in per_tile_weights for spec in specs],
        out_specs=pl.BlockSpec((n * tm * SLAB, 128), lambda s, plan, nt: (s, 0)),
    )
    weights = [w_gate, w_up, w_down, w_gate, w_up, w_down] * n
    return pl.pallas_call(
        _moe_kernel,
        grid_spec=grid_spec,
        out_shape=jax.ShapeDtypeStruct((SORTED_ROWS * SLAB, 128), F32),
        compiler_params=_params(1),
        name="moe",
    )(plan, n_tiles, *([h_sorted] * n), wr, br_row, *weights)


def _alibi_query_lanes():
    lanes = np.zeros((1, N_HEADS * HEAD_SLOT), np.float32)
    for hd, slope in enumerate(_alibi_slopes()):
        rest = np.float32(slope * LOG2E)
        for part in range(ALIBI_PARTS):
            piece = np.float32(np.asarray(rest).astype(jnp.bfloat16))
            lanes[0, hd * HEAD_SLOT + HEAD_DIM + part] = piece
            rest = np.float32(rest - piece)
    return jnp.asarray(lanes)


def _bucket_tables():
    e_lo, e_hi, grp = [], [], []
    for g in range(N_EXPERT_GROUPS):
        for lo, hi in PAIRS:
            e_lo.append(g * EXPERTS_PER_GROUP + lo)
            e_hi.append(g * EXPERTS_PER_GROUP + hi)
            grp.append(g)
    return jnp.asarray(e_lo + e_hi + grp, I32)


def kernel(x, c, w_ada, b_ada, norm1_gain, w_in, b_branch_gate, q_norm_gain, k_norm_gain, attn_sinks, gmlp_norm_gain, gmlp_norm_bias, gmlp_w_spatial, gmlp_b_spatial, w_o_attn, w_o_gmlp, w_out, norm2_gain, w_group_router, b_group_router, w_expert_router, b_expert_router, w_expert_gate, w_expert_up, w_expert_down):
    depth = w_ada.shape[0]
    x2 = x.reshape(TOKENS, D_MODEL)
    tables = _bucket_tables()
    q_aug = _alibi_query_lanes()
    row = lambda v: v.reshape(1, -1)
    for l in range(depth):
        mod = _mod_call(c, w_ada[l], b_ada[l])
        shift1, scale1, gate1, shift2, scale2, gate2 = [
            m.reshape(BATCH, 1, D_MODEL) for m in jnp.split(mod, 6, axis=-1)]

        slot_pad = jnp.zeros((HEAD_SLOT - HEAD_DIM,), F32)
        q_gain_row = row(jnp.tile(jnp.concatenate([q_norm_gain[l] * (HEAD_DIM ** -0.5 * LOG2E), slot_pad]), N_HEADS))
        k_gain_row = row(jnp.tile(jnp.concatenate([k_norm_gain[l], slot_pad]), N_KV_HEADS))
        w_in_b = _round_call(w_in, l)
        q, k, vt, u, vg, gates = _inproj_call(
            x2, scale1, shift1, row(norm1_gain[l]), w_in_b, w_in_b[:, V0:V1].T,
            q_gain_row, k_gain_row, row(gmlp_norm_gain[l]), row(gmlp_norm_bias[l]), row(b_branch_gate[l]), q_aug)

        y_attn = _attn_call(attn_sinks[l], q, k, vt)

        bias_map = jnp.repeat(gmlp_b_spatial[l].T, GMLP_WIDTH // GMLP_GROUPS, axis=1)
        e0, e1 = EXPERT_ROW0, EXPERT_ROW0 + N_EXPERTS
        wr = jnp.zeros((D_MODEL, 128), F32)
        wr = wr.at[:, 0:N_EXPERT_GROUPS].set(w_group_router[l]).at[:, e0:e1].set(w_expert_router[l]).astype(BF16)
        br_row = jnp.zeros((1, 128), F32)
        br_row = br_row.at[0, 0:N_EXPERT_GROUPS].set(b_group_router[l]).at[0, e0:e1].set(b_expert_router[l])
        x1, h2p, buckets, totals = _mixer_call(
            x2, y_attn, u, vg, gates, gmlp_w_spatial[l], bias_map, _round_call(w_o_attn, l),
            _round_call(w_o_gmlp, l), _round_call(w_out, l), gate1, row(norm2_gain[l]), scale2, shift2,
            wr[:, 0:BUCKET_ROWS].T, br_row[:, 0:BUCKET_ROWS].T)

        pos, meta = _rank_call(buckets, totals)
        pos = pos.reshape(TOKENS // TM_ROW, 1, TM_ROW)
        stack = lambda w: w[l].reshape(N_EXPERTS * w.shape[3], w.shape[4])
        h_sorted, wg_b, wu_b, wd_b = _scatter_call(
            pos, meta, h2p, stack(w_expert_gate), stack(w_expert_up), stack(w_expert_down))
        per_expert = lambda w: w.reshape(N_EXPERTS, w.shape[0] // N_EXPERTS, w.shape[1])

        n_tiles = meta[META_N_TILES, 0:1]
        y_sorted = _moe_call(
            _tile_plan(meta[META_TILE_BUCKET], n_tiles, tables), n_tiles, h_sorted, wr, br_row,
            per_expert(wg_b), per_expert(wu_b), per_expert(wd_b))

        x2 = _gather_call(pos, y_sorted, x1, gate2)
    return x2.reshape(x.shape)
```

```python
import functools

import jax
import jax.numpy as jnp
import numpy as np
from jax import lax
from jax.experimental import pallas as pl
from jax.experimental.pallas import tpu as pltpu

D_MODEL = 1024
BATCH = 8
SEQ = 4096
TOKENS = BATCH * SEQ
N_HEADS = 16
N_KV_HEADS = 4
HEAD_DIM = 64
Q_PER_KV = N_HEADS // N_KV_HEADS
BLOCK = 128
HEAD_SLOT = 128
ALIBI_PARTS = 3
ATTN_WIDTH = N_HEADS * HEAD_DIM
KV_WIDTH = N_KV_HEADS * HEAD_DIM
GMLP_WIDTH = 1024
GMLP_GROUPS = 8
GMLP_CHUNK = 128
N_EXPERT_GROUPS = 4
EXPERTS_PER_GROUP = 4
N_EXPERTS = N_EXPERT_GROUPS * EXPERTS_PER_GROUP
D_EXPERT = 512
IN_WIDTH = ATTN_WIDTH + 2 * KV_WIDTH + 2 * GMLP_WIDTH + 2 * D_MODEL
EPS = 1e-6
NEG_INF = -1e30
LOG2E = float(np.float32(np.log2(np.e)))

Q0, Q1 = 0, ATTN_WIDTH
K0, K1 = Q1, Q1 + KV_WIDTH
V0, V1 = K1, K1 + KV_WIDTH
U0, U1 = V1, V1 + GMLP_WIDTH
G0, G1 = U1, U1 + GMLP_WIDTH
B0, B1 = G1, G1 + 2 * D_MODEL

PAIRS = ((0, 1), (0, 2), (0, 3), (1, 2), (1, 3), (2, 3))
N_BUCKETS = N_EXPERT_GROUPS * len(PAIRS)
BUCKET_ROWS = 32
EXPERT_ROW0 = 8

TM_PROJ = 1024
PROJ_ROWS = 512
PROJ_CHUNK = 512
TM_ATTN = 1024
TM_MIX = 1024
MIX_ROWS = 512
TM_MOE = 256
MOE_TILES_PER_STEP = 2
TM_ROW = 512
ROUND_ROWS = 16
SCORES_AHEAD = 6
ROW_UNROLL = 16
NT_MOE = TOKENS // TM_MOE + N_BUCKETS
SORTED_ROWS = NT_MOE * TM_MOE
SLAB = D_MODEL // 128
META_LANES = 256
assert NT_MOE <= META_LANES

VMEM_LIMIT = 56 * 1024 * 1024

F32 = jnp.float32
BF16 = jnp.bfloat16
U32 = jnp.uint32
I32 = jnp.int32


def _alibi_slopes():
    return [float(np.float32(2.0 ** (-8.0 * (i + 1) / N_HEADS))) for i in range(N_HEADS)]


def _params(n_axes):
    return pltpu.CompilerParams(dimension_semantics=("arbitrary",) * n_axes, vmem_limit_bytes=VMEM_LIMIT)


def _sigmoid(x):
    return 1.0 / (1.0 + jnp.exp(-x))


def _gelu(x):
    return 0.5 * x * (1.0 + lax.erf(x * np.float32(1.0 / np.sqrt(2.0))))


def _dot(a, b):
    return jnp.dot(a, b, preferred_element_type=F32)


def _dot_nt(a, b):
    return lax.dot_general(a, b, (((1,), (1,)), ((), ())), preferred_element_type=F32)


def _mod_kernel(c_ref, w_ref, b_ref, o_ref):
    c = c_ref[...]
    ca = c * _sigmoid(c)
    o_ref[...] = _dot(ca.astype(BF16), w_ref[...].astype(BF16)) + b_ref[...]


def _mod_call(c, w_ada, b_ada):
    n = w_ada.shape[1]
    bn = 1536
    return pl.pallas_call(
        _mod_kernel,
        grid=(n // bn,),
        in_specs=[
            pl.BlockSpec((BATCH, D_MODEL), lambda j: (0, 0)),
            pl.BlockSpec((D_MODEL, bn), lambda j: (0, j)),
            pl.BlockSpec((1, bn), lambda j: (0, j)),
        ],
        out_specs=pl.BlockSpec((BATCH, bn), lambda j: (0, j)),
        out_shape=jax.ShapeDtypeStruct((BATCH, n), F32),
        compiler_params=_params(1),
        name="mod",
    )(c, w_ada, b_ada.reshape(1, n))


def _round_kernel(w_ref, o_ref):
    o_ref[...] = w_ref[...].astype(BF16)


def _round_call(w, layer):
    _, k, n = w.shape
    bn = 512
    return pl.pallas_call(
        _round_kernel,
        grid=(n // bn,),
        in_specs=[pl.BlockSpec((None, k, bn), lambda j: (layer, 0, j))],
        out_specs=pl.BlockSpec((k, bn), lambda j: (0, j)),
        out_shape=jax.ShapeDtypeStruct((k, n), BF16),
        compiler_params=_params(1),
        name="round_weight",
    )(w)


def _inproj_kernel(x_ref, sc_ref, sh_ref, ng_ref, w_ref, wvt_ref, qg_ref, kg_ref, lng_ref, lnb_ref, bg_ref, qaug_ref,
                   q_o, k_o, vt_o, u_o, vg_o, gt_o):
    low_half = lax.broadcasted_iota(I32, (1, HEAD_SLOT), 1) < HEAD_DIM
    for r0 in range(0, TM_PROJ, PROJ_ROWS):
        _project_rows(slice(r0, r0 + PROJ_ROWS), low_half, x_ref, sc_ref, sh_ref, ng_ref, w_ref, wvt_ref, qg_ref,
                      kg_ref, lng_ref, lnb_ref, bg_ref, qaug_ref, q_o, k_o, vt_o, u_o, vg_o, gt_o)


def _project_rows(rows, low_half, x_ref, sc_ref, sh_ref, ng_ref, w_ref, wvt_ref, qg_ref, kg_ref, lng_ref, lnb_ref,
                  bg_ref, qaug_ref, q_o, k_o, vt_o, u_o, vg_o, gt_o):
    x = x_ref[rows, :]
    ms = jnp.mean(x * x, axis=-1, keepdims=True)
    h = (x * lax.rsqrt(ms + EPS)) * ng_ref[...]
    h = h * (1.0 + sc_ref[0]) + sh_ref[0]
    hb = h.astype(BF16)

    def proj(c0, c1):
        return _dot(hb, w_ref[:, c0:c1])

    def store_normed_heads(raw, gain_ref, spare_ref, o_ref):
        for p in range(raw.shape[1] // HEAD_SLOT):
            pair = raw[:, p * HEAD_SLOT:(p + 1) * HEAD_SLOT]
            for hd, head in ((2 * p, pair), (2 * p + 1, pltpu.roll(pair, HEAD_DIM, 1))):
                cols = slice(hd * HEAD_SLOT, (hd + 1) * HEAD_SLOT)
                xh = jnp.where(low_half, head, 0.0)
                r = lax.rsqrt(jnp.sum(xh * xh, axis=-1, keepdims=True) * (1.0 / HEAD_DIM) + EPS)
                y = xh * r * gain_ref[:, cols]
                if spare_ref is not None:
                    y = y + spare_ref[:, cols]
                o_ref[rows, cols] = y.astype(BF16)

    store_normed_heads(proj(Q0, Q1), qg_ref, qaug_ref, q_o)
    store_normed_heads(proj(K0, K1), kg_ref, None, k_o)
    vt_o[:, rows] = _dot_nt(wvt_ref[...], hb).astype(BF16)

    cw = PROJ_CHUNK

    def gate_chunk(c):
        cols = slice(c * cw, (c + 1) * cw)
        gt_o[rows, cols] = _sigmoid(proj(B0 + c * cw, B0 + (c + 1) * cw) + bg_ref[:, cols]).astype(BF16)

    vg_parts = []
    for c in range(GMLP_WIDTH // cw):
        u_o[rows, c * cw:(c + 1) * cw] = _gelu(proj(U0 + c * cw, U0 + (c + 1) * cw)).astype(BF16)
        gate_chunk(2 * c)
        vg_parts.append(_gelu(proj(G0 + c * cw, G0 + (c + 1) * cw)))
        gate_chunk(2 * c + 1)
    vg = jnp.concatenate(vg_parts, axis=1)
    mu = jnp.mean(vg, axis=-1, keepdims=True)
    vc = vg - mu
    var = jnp.mean(vc * vc, axis=-1, keepdims=True)
    vg_o[rows, :] = (vc * lax.rsqrt(var + EPS) * lng_ref[...] + lnb_ref[...]).astype(BF16)


def _inproj_call(x2, scale1, shift1, norm_gain, w_in, wv_t, q_gain_row, k_gain_row, ln_gain, ln_bias, b_gate, q_aug):
    tm = TM_PROJ
    per_batch = SEQ // tm
    row = lambda n: pl.BlockSpec((1, n), lambda i: (0, 0))
    modrow = pl.BlockSpec((1, 1, D_MODEL), lambda i: (i // per_batch, 0, 0))
    tile = lambda n: pl.BlockSpec((tm, n), lambda i: (i, 0))
    out = lambda n: jax.ShapeDtypeStruct((TOKENS, n), BF16)
    return pl.pallas_call(
        _inproj_kernel,
        grid=(TOKENS // tm,),
        in_specs=[
            tile(D_MODEL), modrow, modrow, row(D_MODEL),
            pl.BlockSpec((D_MODEL, IN_WIDTH), lambda i: (0, 0)),
            pl.BlockSpec((KV_WIDTH, D_MODEL), lambda i: (0, 0)),
            row(N_HEADS * HEAD_SLOT), row(N_KV_HEADS * HEAD_SLOT), row(GMLP_WIDTH), row(GMLP_WIDTH),
            row(2 * D_MODEL), row(N_HEADS * HEAD_SLOT),
        ],
        out_specs=[tile(N_HEADS * HEAD_SLOT), tile(N_KV_HEADS * HEAD_SLOT),
                   pl.BlockSpec((KV_WIDTH, tm), lambda i: (0, i)),
                   tile(GMLP_WIDTH), tile(GMLP_WIDTH), tile(2 * D_MODEL)],
        out_shape=[out(N_HEADS * HEAD_SLOT), out(N_KV_HEADS * HEAD_SLOT),
                   jax.ShapeDtypeStruct((KV_WIDTH, TOKENS), BF16),
                   out(GMLP_WIDTH), out(GMLP_WIDTH), out(2 * D_MODEL)],
        compiler_params=_params(1),
        name="inproj",
    )(x2, scale1, shift1, norm_gain, w_in, wv_t, q_gain_row, k_gain_row, ln_gain, ln_bias, b_gate, q_aug)


def _attn_kernel(sink_ref, q_ref, kc_ref, kp_ref, vc_ref, vp_ref, wg_ref, wu_ref, wd_ref, o_ref, wg_o, wu_o, wd_o):
    i = pl.program_id(1)
    w_refs, w_outs = (wg_ref, wu_ref, wd_ref), (wg_o, wu_o, wd_o)
    slopes = _alibi_slopes()
    kj = lax.broadcasted_iota(I32, (BLOCK, 2 * BLOCK), 0)
    qi = lax.broadcasted_iota(I32, (BLOCK, 2 * BLOCK), 1) & (BLOCK - 1)
    own = kj <= qi
    from_own = jnp.where(own, 1.0, 0.0).astype(BF16)
    from_prev = jnp.where(own, 0.0, 1.0).astype(BF16)
    has_prev = i > 0
    low_half = lax.broadcasted_iota(I32, (1, HEAD_SLOT), 1) < HEAD_DIM
    k_lane = lax.broadcasted_iota(I32, (2 * BLOCK, HEAD_SLOT), 1)
    k_aug = jnp.where((k_lane >= HEAD_DIM) & (k_lane < HEAD_DIM + ALIBI_PARTS),
                      lax.broadcasted_iota(I32, (2 * BLOCK, HEAD_SLOT), 0), 0).astype(F32).astype(BF16)
    q_pos = (lax.broadcasted_iota(I32, (1, BLOCK), 1) + BLOCK).astype(F32)

    keys_of = {}

    def scores(item):
        sb, kv, pr = item
        r0, r1 = sb * BLOCK, (sb + 1) * BLOCK
        if (sb, kv) not in keys_of:
            cols = slice(kv * HEAD_SLOT, (kv + 1) * HEAD_SLOT)
            k_prev = kp_ref[:, cols] if sb == 0 else kc_ref[r0 - BLOCK:r0, cols]
            keys_of[sb, kv] = jnp.where(low_half, jnp.concatenate([k_prev, kc_ref[r0:r1, cols]], axis=0), k_aug)
        ha = kv * Q_PER_KV + 2 * pr
        queries = jnp.concatenate([q_ref[r0:r1, ha * HEAD_SLOT:(ha + 1) * HEAD_SLOT],
                                   q_ref[r0:r1, (ha + 1) * HEAD_SLOT:(ha + 2) * HEAD_SLOT]], axis=0)
        s = _dot_nt(keys_of[sb, kv], queries)
        s_prev = s[0:BLOCK, :]
        if sb == 0:
            s_prev = jnp.where(has_prev, s_prev, NEG_INF)
        return jnp.where(own, s[BLOCK:2 * BLOCK, :], s_prev)

    def attend(item, s):
        sb, kv, pr = item
        r0, r1 = sb * BLOCK, (sb + 1) * BLOCK
        rows = slice(kv * HEAD_DIM, (kv + 1) * HEAD_DIM)
        v_prev = vp_ref[rows, :] if sb == 0 else vc_ref[rows, r0 - BLOCK:r0]
        vt = jnp.concatenate([v_prev, vc_ref[rows, r0:r1]], axis=1)
        ha = kv * Q_PER_KV + 2 * pr
        hb = ha + 1
        sink = LOG2E * jnp.concatenate(
            [sink_ref[ha] + slopes[ha] * q_pos, sink_ref[hb] + slopes[hb] * q_pos], axis=1)
        m = jnp.maximum(jnp.max(s, axis=0, keepdims=True), sink)
        p = jnp.exp2(s - m)
        den = jnp.sum(p, axis=0, keepdims=True) + jnp.exp2(sink - m)
        pb = p.astype(BF16)
        p_keys = jnp.concatenate([pb * from_prev, pb * from_own], axis=0)
        o = _dot(vt, p_keys) / den
        pair = jnp.concatenate([o[:, 0:BLOCK], o[:, BLOCK:2 * BLOCK]], axis=0).T
        o_ref[r0:r1, ha * HEAD_DIM:(hb + 1) * HEAD_DIM] = pair.astype(BF16)

    items = [(sb, kv, pr) for sb in range(TM_ATTN // BLOCK) for kv in range(N_KV_HEADS)
             for pr in range(Q_PER_KV // 2)]
    rounding = [(src, dst, r0) for src, dst in zip(w_refs, w_outs) for r0 in range(0, src.shape[0], ROUND_ROWS)]
    pending = []
    for n in range(len(items) + SCORES_AHEAD):
        if n < len(items):
            pending.append(scores(items[n]))
            for src, dst, r0 in rounding[n * len(rounding) // len(items):(n + 1) * len(rounding) // len(items)]:
                dst[r0:r0 + ROUND_ROWS, :] = src[r0:r0 + ROUND_ROWS, :].astype(BF16)
        if n >= SCORES_AHEAD:
            attend(items[n - SCORES_AHEAD], pending.pop(0))


def _attn_call(sinks, q, k, vt, w_gate, w_up, w_down):
    tq = TM_ATTN
    per_batch = SEQ // tq
    steps = BATCH * per_batch
    blocks_per_tile = tq // BLOCK
    q_width = N_HEADS * HEAD_SLOT
    k_width = N_KV_HEADS * HEAD_SLOT
    cur = lambda n: pl.BlockSpec((tq, n), lambda b, i: (b * per_batch + i, 0))
    prev_block = lambda b, i: b * (SEQ // BLOCK) + jnp.maximum(i * blocks_per_tile - 1, 0)
    w_in = [w_gate, w_up, w_down]
    w_blocks = [pl.BlockSpec((w.shape[0] // steps, w.shape[1]), lambda b, i: (b * per_batch + i, 0)) for w in w_in]
    return pl.pallas_call(
        _attn_kernel,
        grid=(BATCH, per_batch),
        in_specs=[
            pl.BlockSpec(memory_space=pltpu.SMEM),
            cur(q_width), cur(k_width),
            pl.BlockSpec((BLOCK, k_width), lambda b, i: (prev_block(b, i), 0)),
            pl.BlockSpec((KV_WIDTH, tq), lambda b, i: (0, b * per_batch + i)),
            pl.BlockSpec((KV_WIDTH, BLOCK), lambda b, i: (0, prev_block(b, i))),
        ] + w_blocks,
        out_specs=[cur(ATTN_WIDTH)] + w_blocks,
        out_shape=[jax.ShapeDtypeStruct((TOKENS, ATTN_WIDTH), BF16)]
        + [jax.ShapeDtypeStruct(w.shape, BF16) for w in w_in],
        compiler_params=_params(2),
        name="attn",
    )(sinks, q, k, k, vt, vt, *w_in)


def _first_argmax(vals):
    m = vals[0]
    for v in vals[1:]:
        m = jnp.maximum(m, v)
    idx = jnp.full(m.shape, len(vals) - 1, I32)
    for k in range(len(vals) - 2, -1, -1):
        idx = jnp.where(vals[k] == m, k, idx)
    return m, idx


def _mixer_kernel(x_ref, ya_ref, u_ref, vg_ref, gt_ref, ws_ref, bmap_ref, woa_ref, wog_ref, wout_ref, gate1_ref,
                  ng_ref, sc_ref, sh_ref, wrt_ref, brt_ref, x1_o, h2_o, bk_o, cnt_o, cnt):
    tm = TM_MIX
    t_idx = lax.broadcasted_iota(I32, (GMLP_CHUNK, GMLP_CHUNK), 0)
    s_idx = lax.broadcasted_iota(I32, (GMLP_CHUNK, GMLP_CHUNK), 1)
    ws = [jnp.where(t_idx >= s_idx, ws_ref[g], 0.0).astype(BF16) for g in range(GMLP_GROUPS)]
    gc = GMLP_WIDTH // GMLP_GROUPS

    def mix_rows(r0):
        rows = slice(r0, r0 + MIX_ROWS)
        chunks = []
        for c0 in range(r0, r0 + MIX_ROWS, GMLP_CHUNK):
            c1 = c0 + GMLP_CHUNK
            mixed = jnp.concatenate(
                [_dot(ws[g], vg_ref[c0:c1, g * gc:(g + 1) * gc]) for g in range(GMLP_GROUPS)], axis=1)
            mixed = mixed + bmap_ref[...]
            chunks.append((u_ref[c0:c1, :].astype(F32) * mixed).astype(BF16))
        y_gmlp = jnp.concatenate(chunks, axis=0)
        pa = _dot(ya_ref[rows, :], woa_ref[...])
        pg = _dot(y_gmlp, wog_ref[...])
        merged = gt_ref[rows, 0:D_MODEL].astype(F32) * pa + gt_ref[rows, D_MODEL:2 * D_MODEL].astype(F32) * pg
        x1 = x_ref[rows, :] + gate1_ref[0] * _dot(merged.astype(BF16), wout_ref[...])
        x1_o[rows, :] = x1

        ms = jnp.mean(x1 * x1, axis=-1, keepdims=True)
        h2 = (x1 * lax.rsqrt(ms + EPS)) * ng_ref[...]
        h2 = h2 * (1.0 + sc_ref[0]) + sh_ref[0]
        for s in range(SLAB):
            h2_o[pl.ds(r0 * SLAB + s, MIX_ROWS, stride=SLAB), :] = h2[:, s * 128:(s + 1) * 128]

        lg = _dot_nt(wrt_ref[...], h2.astype(BF16)) + brt_ref[...]
        _, gi = _first_argmax([lg[r:r + 1, :] for r in range(N_EXPERT_GROUPS)])
        el = [lg[EXPERT_ROW0 + r:EXPERT_ROW0 + r + 1, :] for r in range(N_EXPERTS)]
        eg = []
        for k in range(EXPERTS_PER_GROUP):
            v = el[(N_EXPERT_GROUPS - 1) * EXPERTS_PER_GROUP + k]
            for g in range(N_EXPERT_GROUPS - 2, -1, -1):
                v = jnp.where(gi == g, el[g * EXPERTS_PER_GROUP + k], v)
            eg.append(v)
        _, i1 = _first_argmax(eg)
        _, i2 = _first_argmax([jnp.where(i1 == k, -3e38, eg[k]) for k in range(EXPERTS_PER_GROUP)])
        lo = jnp.minimum(i1, i2)
        hi = jnp.maximum(i1, i2)
        pair = jnp.where(lo == 0, hi - 1, jnp.where(lo == 1, hi + 1, 5))
        bucket = gi * len(PAIRS) + pair
        bk_o[0, :, rows] = bucket
        in_bucket = lax.broadcasted_iota(I32, (BUCKET_ROWS, MIX_ROWS), 0) == bucket
        return jnp.sum(jnp.where(in_bucket, 1.0, 0.0), axis=1, keepdims=True)

    counts = [mix_rows(r0) for r0 in range(0, tm, MIX_ROWS)]

    @pl.when(pl.program_id(0) == 0)
    def _():
        cnt[...] = jnp.zeros_like(cnt)
    cnt[...] += sum(counts[1:], counts[0])
    cnt_o[...] = cnt[...]


def _mixer_call(x2, y_attn, u, vg, gates, w_spatial, bias_map, wo_attn, wo_gmlp, w_out, gate1, norm_gain, scale2, shift2,
                wr_t, br_t):
    tm = TM_MIX
    per_batch = SEQ // tm
    nt = TOKENS // tm
    tile = lambda n: pl.BlockSpec((tm, n), lambda i: (i, 0))
    full = lambda shape: pl.BlockSpec(shape, lambda i: (0,) * len(shape))
    once = lambda shape: pl.BlockSpec(shape, lambda i: (0,) * len(shape), pipeline_mode=pl.Buffered(1))
    modrow = pl.BlockSpec((1, 1, D_MODEL), lambda i: (i // per_batch, 0, 0))
    return pl.pallas_call(
        _mixer_kernel,
        grid=(nt,),
        in_specs=[
            tile(D_MODEL), tile(ATTN_WIDTH), tile(GMLP_WIDTH), tile(GMLP_WIDTH), tile(2 * D_MODEL),
            full((GMLP_GROUPS, GMLP_CHUNK, GMLP_CHUNK)), full((GMLP_CHUNK, GMLP_WIDTH)),
            once((ATTN_WIDTH, D_MODEL)), once((GMLP_WIDTH, D_MODEL)), once((D_MODEL, D_MODEL)),
            modrow, full((1, D_MODEL)), modrow, modrow,
            full((BUCKET_ROWS, D_MODEL)), full((BUCKET_ROWS, 1)),
        ],
        out_specs=[tile(D_MODEL), pl.BlockSpec((tm * SLAB, 128), lambda i: (i, 0)),
                   pl.BlockSpec((1, 1, tm), lambda i: (i, 0, 0)), full((BUCKET_ROWS, 128))],
        out_shape=[
            jax.ShapeDtypeStruct((TOKENS, D_MODEL), F32),
            jax.ShapeDtypeStruct((TOKENS * SLAB, 128), F32),
            jax.ShapeDtypeStruct((nt, 1, tm), I32),
            jax.ShapeDtypeStruct((BUCKET_ROWS, 128), F32),
        ],
        scratch_shapes=[pltpu.VMEM((BUCKET_ROWS, 128), F32)],
        compiler_params=_params(1),
        name="mixer",
    )(x2, y_attn, u, vg, gates, w_spatial, bias_map, wo_attn, wo_gmlp, w_out, gate1, norm_gain, scale2, shift2,
      wr_t, br_t)


META_TILE_BUCKET, META_N_TILES, META_PAD_START, META_PAD_END = 0, 1, 2, 3


def _rank_kernel(bk_ref, total_ref, pos_o, meta_o, seen, off):
    tm = TM_MIX
    i = pl.program_id(0)
    bk = bk_ref[0]
    onehot = lax.broadcasted_iota(I32, (BUCKET_ROWS, tm), 0) == bk

    @pl.when(i == 0)
    def _():
        total = total_ref[...]
        tiles = jnp.floor((total + (TM_MOE - 1)) * (1.0 / TM_MOE))
        r = lax.broadcasted_iota(I32, (BUCKET_ROWS, BUCKET_ROWS), 0)
        c = lax.broadcasted_iota(I32, (BUCKET_ROWS, BUCKET_ROWS), 1)
        before = jnp.where(c < r, 1.0, 0.0).astype(BF16)
        first_tile = _dot(before, tiles.astype(BF16))
        first_row = first_tile * TM_MOE
        off[...] = first_row
        seen[...] = jnp.zeros_like(seen)
        end_tile = (first_tile + tiles)[:, 0:1]
        lane = lax.broadcasted_iota(I32, (BUCKET_ROWS, META_LANES), 1)
        bsub = lax.broadcasted_iota(I32, (BUCKET_ROWS, META_LANES), 0)
        is_bucket = bsub < N_BUCKETS
        tile_bucket = jnp.sum(jnp.where((lane.astype(F32) >= end_tile) & is_bucket, 1.0, 0.0), axis=0, keepdims=True)
        as_row = lambda col: jnp.sum(jnp.where((bsub == lane) & is_bucket, col, 0.0), axis=0, keepdims=True)
        n_tiles = jnp.sum(jnp.where(bsub == N_BUCKETS - 1, end_tile, 0.0), axis=0, keepdims=True)
        pad_start = as_row(end_tile - jnp.minimum(tiles[:, 0:1], 1.0))
        pad_end = as_row(end_tile)
        row = lax.broadcasted_iota(I32, (8, META_LANES), 0)
        meta = jnp.zeros((8, META_LANES), F32)
        for k, v in ((META_TILE_BUCKET, tile_bucket), (META_N_TILES, n_tiles), (META_PAD_START, pad_start),
                     (META_PAD_END, pad_end)):
            meta = jnp.where(row == k, v, meta)
        meta_o[...] = meta.astype(I32)

    s_idx = lax.broadcasted_iota(I32, (tm, tm), 0)
    t_idx = lax.broadcasted_iota(I32, (tm, tm), 1)
    upto = jnp.where(s_idx <= t_idx, 1.0, 0.0).astype(BF16)
    incl = _dot(jnp.where(onehot, 1.0, 0.0).astype(BF16), upto)
    base = off[:, 0:1] + seen[:, 0:1]
    posf = jnp.sum(jnp.where(onehot, base + incl - 1.0, 0.0), axis=0, keepdims=True)
    pos_o[0] = posf.astype(I32)
    seen[...] += incl[:, tm - 1:tm]


def _rank_call(buckets, totals):
    nt, _, tm = buckets.shape
    return pl.pallas_call(
        _rank_kernel,
        grid=(nt,),
        in_specs=[pl.BlockSpec((1, 1, tm), lambda i: (i, 0, 0)),
                  pl.BlockSpec((BUCKET_ROWS, 128), lambda i: (0, 0))],
        out_specs=[
            pl.BlockSpec((1, 1, tm), lambda i: (i, 0, 0)),
            pl.BlockSpec((8, META_LANES), lambda i: (0, 0)),
        ],
        out_shape=[
            jax.ShapeDtypeStruct((nt, 1, tm), I32),
            jax.ShapeDtypeStruct((8, META_LANES), I32),
        ],
        scratch_shapes=[pltpu.VMEM((BUCKET_ROWS, 128), F32), pltpu.VMEM((BUCKET_ROWS, 128), F32)],
        compiler_params=_params(1),
        name="rank",
    )(buckets, totals)


def _store_slabs(ref, x):
    n = x.shape[0]
    for s in range(SLAB):
        ref[pl.ds(s, n, stride=SLAB), :] = x[:, s * 128:(s + 1) * 128]


def _load_slabs(ref, n):
    return jnp.concatenate([ref[pl.ds(s, n, stride=SLAB), :] for s in range(SLAB)], axis=1)


def _row_copy(src, src_row, dst, dst_row, sem):
    return pltpu.make_async_copy(src.at[pl.ds(pl.multiple_of(src_row * SLAB, SLAB), SLAB)],
                                 dst.at[pl.ds(pl.multiple_of(dst_row * SLAB, SLAB), SLAB)], sem)


def _start_tile_rows(copy_of_row):
    def trip(c, carry):
        for k in range(ROW_UNROLL):
            copy_of_row(c * ROW_UNROLL + k).start(priority=k % 2)
        return carry
    lax.fori_loop(0, TM_ROW // ROW_UNROLL, trip, 0)


def _tile_wait(src, dst, sem):
    pltpu.make_async_copy(src.at[pl.ds(0, TM_ROW * SLAB)], dst, sem).wait()


def _scatter_kernel(pos_ref, meta_ref, h_ref, o_ref, zero_tile, sem, pad_sem):
    @pl.when(pl.program_id(0) == 0)
    def _():
        zero_tile[...] = jnp.zeros_like(zero_tile)

        def tile_copy(j):
            rows = pl.ds(pl.multiple_of(j * (TM_MOE * SLAB), TM_MOE * SLAB), TM_MOE * SLAB)
            return pltpu.make_async_copy(zero_tile, o_ref.at[rows], pad_sem)

        def for_all_zero_copies(act):
            def per_bucket(b, carry):
                lo, hi = meta_ref[META_PAD_START, b], meta_ref[META_PAD_END, b]
                return lax.fori_loop(lo, hi, lambda j, c: (act(tile_copy(j)), c)[1], carry)
            lax.fori_loop(0, N_BUCKETS, per_bucket, 0)
            lax.fori_loop(meta_ref[META_N_TILES, 0], NT_MOE, lambda j, c: (act(tile_copy(j)), c)[1], 0)

        for_all_zero_copies(lambda cp: cp.start())
        for_all_zero_copies(lambda cp: cp.wait())

    _start_tile_rows(lambda r: _row_copy(h_ref, r, o_ref, pos_ref[0, 0, r], sem))
    _tile_wait(h_ref, o_ref.at[pl.ds(0, TM_ROW * SLAB)], sem)


def _scatter_call(pos, meta, h2):
    tm = TM_ROW
    return pl.pallas_call(
        _scatter_kernel,
        grid=(TOKENS // tm,),
        in_specs=[
            pl.BlockSpec((1, 1, tm), lambda i: (i, 0, 0), memory_space=pltpu.SMEM),
            pl.BlockSpec(memory_space=pltpu.SMEM),
            pl.BlockSpec((tm * SLAB, 128), lambda i: (i, 0)),
        ],
        out_specs=pl.BlockSpec(memory_space=pl.ANY),
        out_shape=jax.ShapeDtypeStruct((SORTED_ROWS * SLAB, 128), F32),
        scratch_shapes=[pltpu.VMEM((TM_MOE * SLAB, 128), F32), pltpu.SemaphoreType.DMA(()),
                        pltpu.SemaphoreType.DMA(())],
        compiler_params=_params(1),
        name="scatter",
    )(pos, meta, h2)


def _gather_kernel(pos_ref, pos_next_ref, ys_ref, x1_ref, gate2_ref, o_ref, buf, sem):
    i = pl.program_id(0)
    n = pl.num_programs(0)
    slot = lax.rem(i, 2)

    def fetch(p_ref, s):
        _start_tile_rows(lambda r: _row_copy(ys_ref, p_ref[0, 0, r], buf.at[s], r, sem.at[s]))

    @pl.when(i == 0)
    def _():
        fetch(pos_ref, 0)

    @pl.when(i + 1 < n)
    def _():
        fetch(pos_next_ref, 1 - slot)

    _tile_wait(ys_ref, buf.at[slot], sem.at[slot])
    o_ref[...] = x1_ref[...] + gate2_ref[0] * _load_slabs(buf.at[slot], TM_ROW)


def _gather_call(pos, y_sorted, x1, gate2):
    tm = TM_ROW
    per_batch = SEQ // tm
    n = TOKENS // tm
    return pl.pallas_call(
        _gather_kernel,
        grid=(n,),
        in_specs=[
            pl.BlockSpec((1, 1, tm), lambda i: (i, 0, 0), memory_space=pltpu.SMEM),
            pl.BlockSpec((1, 1, tm), lambda i: (jnp.minimum(i + 1, n - 1), 0, 0), memory_space=pltpu.SMEM),
            pl.BlockSpec(memory_space=pl.ANY),
            pl.BlockSpec((tm, D_MODEL), lambda i: (i, 0)),
            pl.BlockSpec((1, 1, D_MODEL), lambda i: (i // per_batch, 0, 0)),
        ],
        out_specs=pl.BlockSpec((tm, D_MODEL), lambda i: (i, 0)),
        out_shape=jax.ShapeDtypeStruct((TOKENS, D_MODEL), F32),
        scratch_shapes=[pltpu.VMEM((2, tm * SLAB, 128), F32), pltpu.SemaphoreType.DMA((2,))],
        compiler_params=_params(1),
        name="gather",
    )(pos, pos, y_sorted, x1, gate2)


PLAN_ROW, PLAN_E_LO, PLAN_E_HI, PLAN_GROUP = 0, 1, 2, 3


def _tile_plan(tile_bucket, n_tiles, tables):
    row = jnp.minimum(jnp.arange(NT_MOE, dtype=I32), n_tiles[0] - 1)
    bucket = tile_bucket[row]
    return jnp.concatenate([row, tables[bucket], tables[N_BUCKETS + bucket], tables[2 * N_BUCKETS + bucket]])


def _moe_kernel(plan_ref, nt_ref, *refs):
    n = MOE_TILES_PER_STEP
    h_refs = refs[0:n]
    wr_ref, brow_ref = refs[n:n + 2]
    w_refs = [refs[n + 2 + 6 * t:n + 2 + 6 * (t + 1)] for t in range(n)]
    o_ref = refs[n + 2 + 6 * n]
    first = pl.program_id(0) * n
    n_used = nt_ref[0]

    @pl.when(first < n_used)
    def _():
        tiles = []
        for t in range(n):
            e_lo = plan_ref[PLAN_E_LO * NT_MOE + first + t]
            e_hi = plan_ref[PLAN_E_HI * NT_MOE + first + t]
            g = plan_ref[PLAN_GROUP * NT_MOE + first + t]
            h = _load_slabs(h_refs[t], TM_MOE).astype(BF16)
            lg = _dot(h, wr_ref[...]) + brow_ref[...]
            wg_lo, wu_lo, _, wg_hi, wu_hi, _ = w_refs[t]
            ups = (_dot(h, wg_lo[...]), _dot(h, wu_lo[...]), _dot(h, wg_hi[...]), _dot(h, wu_hi[...]))
            tiles.append((lg, e_lo, e_hi, g, ups))

        for t, (lg, e_lo, e_hi, g, (a_lo, b_lo, a_hi, b_hi)) in enumerate(tiles):
            lane = lax.broadcasted_iota(I32, lg.shape, 1)
            is_group = lane < N_EXPERT_GROUPS
            gmax = jnp.max(jnp.where(is_group, lg, NEG_INF), axis=1, keepdims=True)
            ge = jnp.exp(lg - gmax)
            g_w = (jnp.sum(jnp.where(lane == g, ge, 0.0), axis=1, keepdims=True)
                   / jnp.sum(jnp.where(is_group, ge, 0.0), axis=1, keepdims=True))
            v_lo = jnp.sum(jnp.where(lane == EXPERT_ROW0 + e_lo, lg, 0.0), axis=1, keepdims=True)
            v_hi = jnp.sum(jnp.where(lane == EXPERT_ROW0 + e_hi, lg, 0.0), axis=1, keepdims=True)
            m = jnp.maximum(v_lo, v_hi)
            x_lo = jnp.exp(v_lo - m)
            x_hi = jnp.exp(v_hi - m)
            w_lo = x_lo / (x_lo + x_hi) * g_w
            w_hi = x_hi / (x_lo + x_hi) * g_w
            hid_lo = (a_lo * _sigmoid(a_lo) * b_lo * w_lo).astype(BF16)
            hid_hi = (a_hi * _sigmoid(a_hi) * b_hi * w_hi).astype(BF16)
            _, _, wd_lo, _, _, wd_hi = w_refs[t]
            y = _dot(hid_lo, wd_lo[...]) + _dot(hid_hi, wd_hi[...])
            if t > 0:
                y = jnp.where(first + t < n_used, y, 0.0)
            for s in range(SLAB):
                o_ref[pl.ds(t * TM_MOE * SLAB + s, TM_MOE, stride=SLAB), :] = y[:, s * 128:(s + 1) * 128]

    @pl.when(first >= n_used)
    def _():
        o_ref[...] = jnp.zeros_like(o_ref)


def _moe_call(plan, n_tiles, h_sorted, wr, br_row, w_gate, w_up, w_down):
    tm = TM_MOE
    n = MOE_TILES_PER_STEP

    def h_spec(t):
        return pl.BlockSpec((tm * SLAB, 128), lambda s, plan, nt: (plan[PLAN_ROW * NT_MOE + s * n + t], 0))

    def w_spec(shape, section, t):
        return pl.BlockSpec((None,) + shape, lambda s, plan, nt: (plan[section * NT_MOE + s * n + t], 0, 0))

    up = (D_MODEL, D_EXPERT)
    down = (D_EXPERT, D_MODEL)
    per_tile_weights = [[w_spec(up, PLAN_E_LO, t), w_spec(up, PLAN_E_LO, t), w_spec(down, PLAN_E_LO, t),
                         w_spec(up, PLAN_E_HI, t), w_spec(up, PLAN_E_HI, t), w_spec(down, PLAN_E_HI, t)]
                        for t in range(n)]
    grid_spec = pltpu.PrefetchScalarGridSpec(
        num_scalar_prefetch=2,
        grid=(NT_MOE // n,),
        in_specs=[h_spec(t) for t in range(n)] + [
            pl.BlockSpec((D_MODEL, 128), lambda s, plan, nt: (0, 0)),
            pl.BlockSpec((1, 128), lambda s, plan, nt: (0, 0)),
        ] + [spec for specs in per_tile_weights for spec in specs],
        out_specs=pl.BlockSpec((n * tm * SLAB, 128), lambda s, plan, nt: (s, 0)),
    )
    weights = [w_gate, w_up, w_down, w_gate, w_up, w_down] * n
    return pl.pallas_call(
        _moe_kernel,
        grid_spec=grid_spec,
        out_shape=jax.ShapeDtypeStruct((SORTED_ROWS * SLAB, 128), F32),
        compiler_params=_params(1),
        name="moe",
    )(plan, n_tiles, *([h_sorted] * n), wr, br_row, *weights)


def _alibi_query_lanes():
    lanes = np.zeros((1, N_HEADS * HEAD_SLOT), np.float32)
    for hd, slope in enumerate(_alibi_slopes()):
        rest = np.float32(slope * LOG2E)
        for part in range(ALIBI_PARTS):
            piece = np.float32(np.asarray(rest).astype(jnp.bfloat16))
            lanes[0, hd * HEAD_SLOT + HEAD_DIM + part] = piece
            rest = np.float32(rest - piece)
    return jnp.asarray(lanes)


def _bucket_tables():
    e_lo, e_hi, grp = [], [], []
    for g in range(N_EXPERT_GROUPS):
        for lo, hi in PAIRS:
            e_lo.append(g * EXPERTS_PER_GROUP + lo)
            e_hi.append(g * EXPERTS_PER_GROUP + hi)
            grp.append(g)
    return jnp.asarray(e_lo + e_hi + grp, I32)


def kernel(x, c, w_ada, b_ada, norm1_gain, w_in, b_branch_gate, q_norm_gain, k_norm_gain, attn_sinks, gmlp_norm_gain, gmlp_norm_bias, gmlp_w_spatial, gmlp_b_spatial, w_o_attn, w_o_gmlp, w_out, norm2_gain, w_group_router, b_group_router, w_expert_router, b_expert_router, w_expert_gate, w_expert_up, w_expert_down):
    depth = w_ada.shape[0]
    x2 = x.reshape(TOKENS, D_MODEL)
    tables = _bucket_tables()
    q_aug = _alibi_query_lanes()
    row = lambda v: v.reshape(1, -1)
    for l in range(depth):
        mod = _mod_call(c, w_ada[l], b_ada[l])
        shift1, scale1, gate1, shift2, scale2, gate2 = [
            m.reshape(BATCH, 1, D_MODEL) for m in jnp.split(mod, 6, axis=-1)]

        slot_pad = jnp.zeros((HEAD_SLOT - HEAD_DIM,), F32)
        q_gain_row = row(jnp.tile(jnp.concatenate([q_norm_gain[l] * (HEAD_DIM ** -0.5 * LOG2E), slot_pad]), N_HEADS))
        k_gain_row = row(jnp.tile(jnp.concatenate([k_norm_gain[l], slot_pad]), N_KV_HEADS))
        w_in_b = _round_call(w_in, l)
        q, k, vt, u, vg, gates = _inproj_call(
            x2, scale1, shift1, row(norm1_gain[l]), w_in_b, w_in_b[:, V0:V1].T,
            q_gain_row, k_gain_row, row(gmlp_norm_gain[l]), row(gmlp_norm_bias[l]), row(b_branch_gate[l]), q_aug)

        stack = lambda w: w[l].reshape(N_EXPERTS * w.shape[3], w.shape[4])
        y_attn, wg_b, wu_b, wd_b = _attn_call(
            attn_sinks[l], q, k, vt, stack(w_expert_gate), stack(w_expert_up), stack(w_expert_down))

        bias_map = jnp.repeat(gmlp_b_spatial[l].T, GMLP_WIDTH // GMLP_GROUPS, axis=1)
        e0, e1 = EXPERT_ROW0, EXPERT_ROW0 + N_EXPERTS
        wr = jnp.zeros((D_MODEL, 128), F32)
        wr = wr.at[:, 0:N_EXPERT_GROUPS].set(w_group_router[l]).at[:, e0:e1].set(w_expert_router[l]).astype(BF16)
        br_row = jnp.zeros((1, 128), F32)
        br_row = br_row.at[0, 0:N_EXPERT_GROUPS].set(b_group_router[l]).at[0, e0:e1].set(b_expert_router[l])
        x1, h2p, buckets, totals = _mixer_call(
            x2, y_attn, u, vg, gates, gmlp_w_spatial[l], bias_map, _round_call(w_o_attn, l),
            _round_call(w_o_gmlp, l), _round_call(w_out, l), gate1, row(norm2_gain[l]), scale2, shift2,
            wr[:, 0:BUCKET_ROWS].T, br_row[:, 0:BUCKET_ROWS].T)

        pos, meta = _rank_call(buckets, totals)
        pos = pos.reshape(TOKENS // TM_ROW, 1, TM_ROW)
        h_sorted = _scatter_call(pos, meta, h2p)
        per_expert = lambda w: w.reshape(N_EXPERTS, w.shape[0] // N_EXPERTS, w.shape[1])

        n_tiles = meta[META_N_TILES, 0:1]
        y_sorted = _moe_call(
            _tile_plan(meta[META_TILE_BUCKET], n_tiles, tables), n_tiles, h_sorted, wr, br_row,
            per_expert(wg_b), per_expert(wu_b), per_expert(wd_b))

        x2 = _gather_call(pos, y_sorted, x1, gate2)
    return x2.reshape(x.shape)
```

```python
import functools

import jax
import jax.numpy as jnp
import numpy as np
from jax import lax
from jax.experimental import pallas as pl
from jax.experimental.pallas import tpu as pltpu

D_MODEL = 1024
BATCH = 8
SEQ = 4096
TOKENS = BATCH * SEQ
N_HEADS = 16
N_KV_HEADS = 4
HEAD_DIM = 64
Q_PER_KV = N_HEADS // N_KV_HEADS
BLOCK = 128
HEAD_SLOT = 128
ALIBI_PARTS = 3
ATTN_WIDTH = N_HEADS * HEAD_DIM
KV_WIDTH = N_KV_HEADS * HEAD_DIM
GMLP_WIDTH = 1024
GMLP_GROUPS = 8
GMLP_CHUNK = 128
N_EXPERT_GROUPS = 4
EXPERTS_PER_GROUP = 4
N_EXPERTS = N_EXPERT_GROUPS * EXPERTS_PER_GROUP
D_EXPERT = 512
IN_WIDTH = ATTN_WIDTH + 2 * KV_WIDTH + 2 * GMLP_WIDTH + 2 * D_MODEL
EPS = 1e-6
NEG_INF = -1e30
LOG2E = float(np.float32(np.log2(np.e)))

Q0, Q1 = 0, ATTN_WIDTH
K0, K1 = Q1, Q1 + KV_WIDTH
V0, V1 = K1, K1 + KV_WIDTH
U0, U1 = V1, V1 + GMLP_WIDTH
G0, G1 = U1, U1 + GMLP_WIDTH
B0, B1 = G1, G1 + 2 * D_MODEL

PAIRS = ((0, 1), (0, 2), (0, 3), (1, 2), (1, 3), (2, 3))
N_BUCKETS = N_EXPERT_GROUPS * len(PAIRS)
BUCKET_ROWS = 32
EXPERT_ROW0 = 8

TM_PROJ = 1024
PROJ_ROWS = 512
PROJ_CHUNK = 512
TM_ATTN = 1024
TM_MIX = 1024
MIX_ROWS = 512
TM_MOE = 256
MOE_TILES_PER_STEP = 2
TM_ROW = 512
ROUND_ROWS = 16
SCORES_AHEAD = 6
ROW_UNROLL = 16
NT_MOE = TOKENS // TM_MOE + N_BUCKETS
SORTED_ROWS = NT_MOE * TM_MOE
SLAB = D_MODEL // 128
META_LANES = 256
assert NT_MOE <= META_LANES

VMEM_LIMIT = 56 * 1024 * 1024

F32 = jnp.float32
BF16 = jnp.bfloat16
U32 = jnp.uint32
I32 = jnp.int32


def _alibi_slopes():
    return [float(np.float32(2.0 ** (-8.0 * (i + 1) / N_HEADS))) for i in range(N_HEADS)]


def _params(n_axes):
    return pltpu.CompilerParams(dimension_semantics=("arbitrary",) * n_axes, vmem_limit_bytes=VMEM_LIMIT)


def _sigmoid(x):
    return 1.0 / (1.0 + jnp.exp(-x))


def _gelu(x):
    return 0.5 * x * (1.0 + lax.erf(x * np.float32(1.0 / np.sqrt(2.0))))


def _dot(a, b):
    return jnp.dot(a, b, preferred_element_type=F32)


def _dot_nt(a, b):
    return lax.dot_general(a, b, (((1,), (1,)), ((), ())), preferred_element_type=F32)


def _mod_kernel(c_ref, w_ref, b_ref, o_ref):
    c = c_ref[...]
    ca = c * _sigmoid(c)
    o_ref[...] = _dot(ca.astype(BF16), w_ref[...].astype(BF16)) + b_ref[...]


def _mod_call(c, w_ada, b_ada):
    n = w_ada.shape[1]
    bn = 1536
    return pl.pallas_call(
        _mod_kernel,
        grid=(n // bn,),
        in_specs=[
            pl.BlockSpec((BATCH, D_MODEL), lambda j: (0, 0)),
            pl.BlockSpec((D_MODEL, bn), lambda j: (0, j)),
            pl.BlockSpec((1, bn), lambda j: (0, j)),
        ],
        out_specs=pl.BlockSpec((BATCH, bn), lambda j: (0, j)),
        out_shape=jax.ShapeDtypeStruct((BATCH, n), F32),
        compiler_params=_params(1),
        name="mod",
    )(c, w_ada, b_ada.reshape(1, n))


def _round_kernel(w_ref, o_ref):
    o_ref[...] = w_ref[...].astype(BF16)


def _round_call(w, layer):
    _, k, n = w.shape
    bn = 512
    return pl.pallas_call(
        _round_kernel,
        grid=(n // bn,),
        in_specs=[pl.BlockSpec((None, k, bn), lambda j: (layer, 0, j))],
        out_specs=pl.BlockSpec((k, bn), lambda j: (0, j)),
        out_shape=jax.ShapeDtypeStruct((k, n), BF16),
        compiler_params=_params(1),
        name="round_weight",
    )(w)


def _inproj_kernel(x_ref, sc_ref, sh_ref, ng_ref, w_ref, wvt_ref, qg_ref, kg_ref, lng_ref, lnb_ref, bg_ref, qaug_ref,
                   q_o, k_o, vt_o, u_o, vg_o, gt_o):
    low_half = lax.broadcasted_iota(I32, (1, HEAD_SLOT), 1) < HEAD_DIM
    for r0 in range(0, TM_PROJ, PROJ_ROWS):
        _project_rows(slice(r0, r0 + PROJ_ROWS), low_half, x_ref, sc_ref, sh_ref, ng_ref, w_ref, wvt_ref, qg_ref,
                      kg_ref, lng_ref, lnb_ref, bg_ref, qaug_ref, q_o, k_o, vt_o, u_o, vg_o, gt_o)


def _project_rows(rows, low_half, x_ref, sc_ref, sh_ref, ng_ref, w_ref, wvt_ref, qg_ref, kg_ref, lng_ref, lnb_ref,
                  bg_ref, qaug_ref, q_o, k_o, vt_o, u_o, vg_o, gt_o):
    x = x_ref[rows, :]
    ms = jnp.mean(x * x, axis=-1, keepdims=True)
    h = (x * lax.rsqrt(ms + EPS)) * ng_ref[...]
    h = h * (1.0 + sc_ref[0]) + sh_ref[0]
    hb = h.astype(BF16)

    def proj(c0, c1):
        return _dot(hb, w_ref[:, c0:c1])

    def store_normed_heads(raw, gain_ref, spare_ref, o_ref):
        for p in range(raw.shape[1] // HEAD_SLOT):
            pair = raw[:, p * HEAD_SLOT:(p + 1) * HEAD_SLOT]
            for hd, head in ((2 * p, pair), (2 * p + 1, pltpu.roll(pair, HEAD_DIM, 1))):
                cols = slice(hd * HEAD_SLOT, (hd + 1) * HEAD_SLOT)
                xh = jnp.where(low_half, head, 0.0)
                r = lax.rsqrt(jnp.sum(xh * xh, axis=-1, keepdims=True) * (1.0 / HEAD_DIM) + EPS)
                y = xh * r * gain_ref[:, cols]
                if spare_ref is not None:
                    y = y + spare_ref[:, cols]
                o_ref[rows, cols] = y.astype(BF16)

    store_normed_heads(proj(Q0, Q1), qg_ref, qaug_ref, q_o)
    store_normed_heads(proj(K0, K1), kg_ref, None, k_o)
    vt_o[:, rows] = _dot_nt(wvt_ref[...], hb).astype(BF16)

    cw = PROJ_CHUNK

    def gate_chunk(c):
        cols = slice(c * cw, (c + 1) * cw)
        gt_o[rows, cols] = _sigmoid(proj(B0 + c * cw, B0 + (c + 1) * cw) + bg_ref[:, cols]).astype(BF16)

    vg_parts = []
    for c in range(GMLP_WIDTH // cw):
        u_o[rows, c * cw:(c + 1) * cw] = _gelu(proj(U0 + c * cw, U0 + (c + 1) * cw)).astype(BF16)
        gate_chunk(2 * c)
        vg_parts.append(_gelu(proj(G0 + c * cw, G0 + (c + 1) * cw)))
        gate_chunk(2 * c + 1)
    vg = jnp.concatenate(vg_parts, axis=1)
    mu = jnp.mean(vg, axis=-1, keepdims=True)
    vc = vg - mu
    var = jnp.mean(vc * vc, axis=-1, keepdims=True)
    vg_o[rows, :] = (vc * lax.rsqrt(var + EPS) * lng_ref[...] + lnb_ref[...]).astype(BF16)


def _inproj_call(x2, scale1, shift1, norm_gain, w_in, wv_t, q_gain_row, k_gain_row, ln_gain, ln_bias, b_gate, q_aug):
    tm = TM_PROJ
    per_batch = SEQ // tm
    row = lambda n: pl.BlockSpec((1, n), lambda i: (0, 0))
    modrow = pl.BlockSpec((1, 1, D_MODEL), lambda i: (i // per_batch, 0, 0))
    tile = lambda n: pl.BlockSpec((tm, n), lambda i: (i, 0))
    out = lambda n: jax.ShapeDtypeStruct((TOKENS, n), BF16)
    return pl.pallas_call(
        _inproj_kernel,
        grid=(TOKENS // tm,),
        in_specs=[
            tile(D_MODEL), modrow, modrow, row(D_MODEL),
            pl.BlockSpec((D_MODEL, IN_WIDTH), lambda i: (0, 0)),
            pl.BlockSpec((KV_WIDTH, D_MODEL), lambda i: (0, 0)),
            row(N_HEADS * HEAD_SLOT), row(N_KV_HEADS * HEAD_SLOT), row(GMLP_WIDTH), row(GMLP_WIDTH),
            row(2 * D_MODEL), row(N_HEADS * HEAD_SLOT),
        ],
        out_specs=[tile(N_HEADS * HEAD_SLOT), tile(N_KV_HEADS * HEAD_SLOT),
                   pl.BlockSpec((KV_WIDTH, tm), lambda i: (0, i)),
                   tile(GMLP_WIDTH), tile(GMLP_WIDTH), tile(2 * D_MODEL)],
        out_shape=[out(N_HEADS * HEAD_SLOT), out(N_KV_HEADS * HEAD_SLOT),
                   jax.ShapeDtypeStruct((KV_WIDTH, TOKENS), BF16),
                   out(GMLP_WIDTH), out(GMLP_WIDTH), out(2 * D_MODEL)],
        compiler_params=_params(1),
        name="inproj",
    )(x2, scale1, shift1, norm_gain, w_in, wv_t, q_gain_row, k_gain_row, ln_gain, ln_bias, b_gate, q_aug)


def _attn_kernel(sink_ref, q_ref, kc_ref, kp_ref, vc_ref, vp_ref, *rest):
    i = pl.program_id(1)
    n_w = (len(rest) - 1) // 2
    w_refs, o_ref, w_outs = rest[:n_w], rest[n_w], rest[n_w + 1:]
    slopes = _alibi_slopes()
    kj = lax.broadcasted_iota(I32, (BLOCK, 2 * BLOCK), 0)
    qi = lax.broadcasted_iota(I32, (BLOCK, 2 * BLOCK), 1) & (BLOCK - 1)
    own = kj <= qi
    from_own = jnp.where(own, 1.0, 0.0).astype(BF16)
    from_prev = jnp.where(own, 0.0, 1.0).astype(BF16)
    has_prev = i > 0
    low_half = lax.broadcasted_iota(I32, (1, HEAD_SLOT), 1) < HEAD_DIM
    k_lane = lax.broadcasted_iota(I32, (2 * BLOCK, HEAD_SLOT), 1)
    k_aug = jnp.where((k_lane >= HEAD_DIM) & (k_lane < HEAD_DIM + ALIBI_PARTS),
                      lax.broadcasted_iota(I32, (2 * BLOCK, HEAD_SLOT), 0), 0).astype(F32).astype(BF16)
    q_pos = (lax.broadcasted_iota(I32, (1, BLOCK), 1) + BLOCK).astype(F32)

    keys_of = {}

    def scores(item):
        sb, kv, pr = item
        r0, r1 = sb * BLOCK, (sb + 1) * BLOCK
        if (sb, kv) not in keys_of:
            cols = slice(kv * HEAD_SLOT, (kv + 1) * HEAD_SLOT)
            k_prev = kp_ref[:, cols] if sb == 0 else kc_ref[r0 - BLOCK:r0, cols]
            keys_of[sb, kv] = jnp.where(low_half, jnp.concatenate([k_prev, kc_ref[r0:r1, cols]], axis=0), k_aug)
        ha = kv * Q_PER_KV + 2 * pr
        queries = jnp.concatenate([q_ref[r0:r1, ha * HEAD_SLOT:(ha + 1) * HEAD_SLOT],
                                   q_ref[r0:r1, (ha + 1) * HEAD_SLOT:(ha + 2) * HEAD_SLOT]], axis=0)
        s = _dot_nt(keys_of[sb, kv], queries)
        s_prev = s[0:BLOCK, :]
        if sb == 0:
            s_prev = jnp.where(has_prev, s_prev, NEG_INF)
        return jnp.where(own, s[BLOCK:2 * BLOCK, :], s_prev)

    def attend(item, s):
        sb, kv, pr = item
        r0, r1 = sb * BLOCK, (sb + 1) * BLOCK
        rows = slice(kv * HEAD_DIM, (kv + 1) * HEAD_DIM)
        v_prev = vp_ref[rows, :] if sb == 0 else vc_ref[rows, r0 - BLOCK:r0]
        vt = jnp.concatenate([v_prev, vc_ref[rows, r0:r1]], axis=1)
        ha = kv * Q_PER_KV + 2 * pr
        hb = ha + 1
        sink = LOG2E * jnp.concatenate(
            [sink_ref[ha] + slopes[ha] * q_pos, sink_ref[hb] + slopes[hb] * q_pos], axis=1)
        m = jnp.maximum(jnp.max(s, axis=0, keepdims=True), sink)
        p = jnp.exp2(s - m)
        den = jnp.sum(p, axis=0, keepdims=True) + jnp.exp2(sink - m)
        pb = p.astype(BF16)
        p_keys = jnp.concatenate([pb * from_prev, pb * from_own], axis=0)
        o = _dot(vt, p_keys) / den
        pair = jnp.concatenate([o[:, 0:BLOCK], o[:, BLOCK:2 * BLOCK]], axis=0).T
        o_ref[r0:r1, ha * HEAD_DIM:(hb + 1) * HEAD_DIM] = pair.astype(BF16)

    items = [(sb, kv, pr) for sb in range(TM_ATTN // BLOCK) for kv in range(N_KV_HEADS)
             for pr in range(Q_PER_KV // 2)]
    rounding = [(src, dst, r0) for src, dst in zip(w_refs, w_outs) for r0 in range(0, src.shape[0], ROUND_ROWS)]
    pending = []
    for n in range(len(items) + SCORES_AHEAD):
        if n < len(items):
            pending.append(scores(items[n]))
            for src, dst, r0 in rounding[n * len(rounding) // len(items):(n + 1) * len(rounding) // len(items)]:
                dst[r0:r0 + ROUND_ROWS, :] = src[r0:r0 + ROUND_ROWS, :].astype(BF16)
        if n >= SCORES_AHEAD:
            attend(items[n - SCORES_AHEAD], pending.pop(0))


def _attn_call(sinks, q, k, vt, *w_in):
    tq = TM_ATTN
    per_batch = SEQ // tq
    steps = BATCH * per_batch
    blocks_per_tile = tq // BLOCK
    q_width = N_HEADS * HEAD_SLOT
    k_width = N_KV_HEADS * HEAD_SLOT
    cur = lambda n: pl.BlockSpec((tq, n), lambda b, i: (b * per_batch + i, 0))
    prev_block = lambda b, i: b * (SEQ // BLOCK) + jnp.maximum(i * blocks_per_tile - 1, 0)
    w_blocks = [pl.BlockSpec((w.shape[0] // steps, w.shape[1]), lambda b, i: (b * per_batch + i, 0)) for w in w_in]
    return pl.pallas_call(
        _attn_kernel,
        grid=(BATCH, per_batch),
        in_specs=[
            pl.BlockSpec(memory_space=pltpu.SMEM),
            cur(q_width), cur(k_width),
            pl.BlockSpec((BLOCK, k_width), lambda b, i: (prev_block(b, i), 0)),
            pl.BlockSpec((KV_WIDTH, tq), lambda b, i: (0, b * per_batch + i)),
            pl.BlockSpec((KV_WIDTH, BLOCK), lambda b, i: (0, prev_block(b, i))),
        ] + w_blocks,
        out_specs=[cur(ATTN_WIDTH)] + w_blocks,
        out_shape=[jax.ShapeDtypeStruct((TOKENS, ATTN_WIDTH), BF16)]
        + [jax.ShapeDtypeStruct(w.shape, BF16) for w in w_in],
        compiler_params=_params(2),
        name="attn",
    )(sinks, q, k, k, vt, vt, *w_in)


def _first_argmax(vals):
    m = vals[0]
    for v in vals[1:]:
        m = jnp.maximum(m, v)
    idx = jnp.full(m.shape, len(vals) - 1, I32)
    for k in range(len(vals) - 2, -1, -1):
        idx = jnp.where(vals[k] == m, k, idx)
    return m, idx


def _mixer_kernel(x_ref, ya_ref, u_ref, vg_ref, gt_ref, ws_ref, bmap_ref, woa_ref, wog_ref, wout_ref, gate1_ref,
                  ng_ref, sc_ref, sh_ref, wrt_ref, brt_ref, x1_o, h2_o, bk_o, cnt_o, cnt):
    tm = TM_MIX
    t_idx = lax.broadcasted_iota(I32, (GMLP_CHUNK, GMLP_CHUNK), 0)
    s_idx = lax.broadcasted_iota(I32, (GMLP_CHUNK, GMLP_CHUNK), 1)
    ws = [jnp.where(t_idx >= s_idx, ws_ref[g], 0.0).astype(BF16) for g in range(GMLP_GROUPS)]
    gc = GMLP_WIDTH // GMLP_GROUPS

    def mix_rows(r0):
        rows = slice(r0, r0 + MIX_ROWS)
        chunks = []
        for c0 in range(r0, r0 + MIX_ROWS, GMLP_CHUNK):
            c1 = c0 + GMLP_CHUNK
            mixed = jnp.concatenate(
                [_dot(ws[g], vg_ref[c0:c1, g * gc:(g + 1) * gc]) for g in range(GMLP_GROUPS)], axis=1)
            mixed = mixed + bmap_ref[...]
            chunks.append((u_ref[c0:c1, :].astype(F32) * mixed).astype(BF16))
        y_gmlp = jnp.concatenate(chunks, axis=0)
        pa = _dot(ya_ref[rows, :], woa_ref[...])
        pg = _dot(y_gmlp, wog_ref[...])
        merged = gt_ref[rows, 0:D_MODEL].astype(F32) * pa + gt_ref[rows, D_MODEL:2 * D_MODEL].astype(F32) * pg
        x1 = x_ref[rows, :] + gate1_ref[0] * _dot(merged.astype(BF16), wout_ref[...])
        x1_o[rows, :] = x1

        ms = jnp.mean(x1 * x1, axis=-1, keepdims=True)
        h2 = (x1 * lax.rsqrt(ms + EPS)) * ng_ref[...]
        h2 = h2 * (1.0 + sc_ref[0]) + sh_ref[0]
        for s in range(SLAB):
            h2_o[pl.ds(r0 * SLAB + s, MIX_ROWS, stride=SLAB), :] = h2[:, s * 128:(s + 1) * 128]

        lg = _dot_nt(wrt_ref[...], h2.astype(BF16)) + brt_ref[...]
        _, gi = _first_argmax([lg[r:r + 1, :] for r in range(N_EXPERT_GROUPS)])
        el = [lg[EXPERT_ROW0 + r:EXPERT_ROW0 + r + 1, :] for r in range(N_EXPERTS)]
        eg = []
        for k in range(EXPERTS_PER_GROUP):
            v = el[(N_EXPERT_GROUPS - 1) * EXPERTS_PER_GROUP + k]
            for g in range(N_EXPERT_GROUPS - 2, -1, -1):
                v = jnp.where(gi == g, el[g * EXPERTS_PER_GROUP + k], v)
            eg.append(v)
        _, i1 = _first_argmax(eg)
        _, i2 = _first_argmax([jnp.where(i1 == k, -3e38, eg[k]) for k in range(EXPERTS_PER_GROUP)])
        lo = jnp.minimum(i1, i2)
        hi = jnp.maximum(i1, i2)
        pair = jnp.where(lo == 0, hi - 1, jnp.where(lo == 1, hi + 1, 5))
        bucket = gi * len(PAIRS) + pair
        bk_o[0, :, rows] = bucket
        in_bucket = lax.broadcasted_iota(I32, (BUCKET_ROWS, MIX_ROWS), 0) == bucket
        return jnp.sum(jnp.where(in_bucket, 1.0, 0.0), axis=1, keepdims=True)

    counts = [mix_rows(r0) for r0 in range(0, tm, MIX_ROWS)]

    @pl.when(pl.program_id(0) == 0)
    def _():
        cnt[...] = jnp.zeros_like(cnt)
    cnt[...] += sum(counts[1:], counts[0])
    cnt_o[...] = cnt[...]


def _mixer_call(x2, y_attn, u, vg, gates, w_spatial, bias_map, wo_attn, wo_gmlp, w_out, gate1, norm_gain, scale2, shift2,
                wr_t, br_t):
    tm = TM_MIX
    per_batch = SEQ // tm
    nt = TOKENS // tm
    tile = lambda n: pl.BlockSpec((tm, n), lambda i: (i, 0))
    full = lambda shape: pl.BlockSpec(shape, lambda i: (0,) * len(shape))
    once = lambda shape: pl.BlockSpec(shape, lambda i: (0,) * len(shape), pipeline_mode=pl.Buffered(1))
    modrow = pl.BlockSpec((1, 1, D_MODEL), lambda i: (i // per_batch, 0, 0))
    return pl.pallas_call(
        _mixer_kernel,
        grid=(nt,),
        in_specs=[
            tile(D_MODEL), tile(ATTN_WIDTH), tile(GMLP_WIDTH), tile(GMLP_WIDTH), tile(2 * D_MODEL),
            full((GMLP_GROUPS, GMLP_CHUNK, GMLP_CHUNK)), full((GMLP_CHUNK, GMLP_WIDTH)),
            once((ATTN_WIDTH, D_MODEL)), once((GMLP_WIDTH, D_MODEL)), once((D_MODEL, D_MODEL)),
            modrow, full((1, D_MODEL)), modrow, modrow,
            full((BUCKET_ROWS, D_MODEL)), full((BUCKET_ROWS, 1)),
        ],
        out_specs=[tile(D_MODEL), pl.BlockSpec((tm * SLAB, 128), lambda i: (i, 0)),
                   pl.BlockSpec((1, 1, tm), lambda i: (i, 0, 0)), full((BUCKET_ROWS, 128))],
        out_shape=[
            jax.ShapeDtypeStruct((TOKENS, D_MODEL), F32),
            jax.ShapeDtypeStruct((TOKENS * SLAB, 128), F32),
            jax.ShapeDtypeStruct((nt, 1, tm), I32),
            jax.ShapeDtypeStruct((BUCKET_ROWS, 128), F32),
        ],
        scratch_shapes=[pltpu.VMEM((BUCKET_ROWS, 128), F32)],
        compiler_params=_params(1),
        name="mixer",
    )(x2, y_attn, u, vg, gates, w_spatial, bias_map, wo_attn, wo_gmlp, w_out, gate1, norm_gain, scale2, shift2,
      wr_t, br_t)


META_TILE_BUCKET, META_N_TILES, META_PAD_START, META_PAD_END = 0, 1, 2, 3


def _rank_kernel(bk_ref, total_ref, pos_o, meta_o, seen, off):
    tm = TM_MIX
    i = pl.program_id(0)
    bk = bk_ref[0]
    onehot = lax.broadcasted_iota(I32, (BUCKET_ROWS, tm), 0) == bk

    @pl.when(i == 0)
    def _():
        total = total_ref[...]
        tiles = jnp.floor((total + (TM_MOE - 1)) * (1.0 / TM_MOE))
        r = lax.broadcasted_iota(I32, (BUCKET_ROWS, BUCKET_ROWS), 0)
        c = lax.broadcasted_iota(I32, (BUCKET_ROWS, BUCKET_ROWS), 1)
        before = jnp.where(c < r, 1.0, 0.0).astype(BF16)
        first_tile = _dot(before, tiles.astype(BF16))
        first_row = first_tile * TM_MOE
        off[...] = first_row
        seen[...] = jnp.zeros_like(seen)
        end_tile = (first_tile + tiles)[:, 0:1]
        lane = lax.broadcasted_iota(I32, (BUCKET_ROWS, META_LANES), 1)
        bsub = lax.broadcasted_iota(I32, (BUCKET_ROWS, META_LANES), 0)
        is_bucket = bsub < N_BUCKETS
        tile_bucket = jnp.sum(jnp.where((lane.astype(F32) >= end_tile) & is_bucket, 1.0, 0.0), axis=0, keepdims=True)
        as_row = lambda col: jnp.sum(jnp.where((bsub == lane) & is_bucket, col, 0.0), axis=0, keepdims=True)
        n_tiles = jnp.sum(jnp.where(bsub == N_BUCKETS - 1, end_tile, 0.0), axis=0, keepdims=True)
        pad_start = as_row(end_tile - jnp.minimum(tiles[:, 0:1], 1.0))
        pad_end = as_row(end_tile)
        row = lax.broadcasted_iota(I32, (8, META_LANES), 0)
        meta = jnp.zeros((8, META_LANES), F32)
        for k, v in ((META_TILE_BUCKET, tile_bucket), (META_N_TILES, n_tiles), (META_PAD_START, pad_start),
                     (META_PAD_END, pad_end)):
            meta = jnp.where(row == k, v, meta)
        meta_o[...] = meta.astype(I32)

    s_idx = lax.broadcasted_iota(I32, (tm, tm), 0)
    t_idx = lax.broadcasted_iota(I32, (tm, tm), 1)
    upto = jnp.where(s_idx <= t_idx, 1.0, 0.0).astype(BF16)
    incl = _dot(jnp.where(onehot, 1.0, 0.0).astype(BF16), upto)
    base = off[:, 0:1] + seen[:, 0:1]
    posf = jnp.sum(jnp.where(onehot, base + incl - 1.0, 0.0), axis=0, keepdims=True)
    pos_o[0] = posf.astype(I32)
    seen[...] += incl[:, tm - 1:tm]


def _rank_call(buckets, totals):
    nt, _, tm = buckets.shape
    return pl.pallas_call(
        _rank_kernel,
        grid=(nt,),
        in_specs=[pl.BlockSpec((1, 1, tm), lambda i: (i, 0, 0)),
                  pl.BlockSpec((BUCKET_ROWS, 128), lambda i: (0, 0))],
        out_specs=[
            pl.BlockSpec((1, 1, tm), lambda i: (i, 0, 0)),
            pl.BlockSpec((8, META_LANES), lambda i: (0, 0)),
        ],
        out_shape=[
            jax.ShapeDtypeStruct((nt, 1, tm), I32),
            jax.ShapeDtypeStruct((8, META_LANES), I32),
        ],
        scratch_shapes=[pltpu.VMEM((BUCKET_ROWS, 128), F32), pltpu.VMEM((BUCKET_ROWS, 128), F32)],
        compiler_params=_params(1),
        name="rank",
    )(buckets, totals)


def _store_slabs(ref, x):
    n = x.shape[0]
    for s in range(SLAB):
        ref[pl.ds(s, n, stride=SLAB), :] = x[:, s * 128:(s + 1) * 128]


def _load_slabs(ref, n):
    return jnp.concatenate([ref[pl.ds(s, n, stride=SLAB), :] for s in range(SLAB)], axis=1)


def _row_copy(src, src_row, dst, dst_row, sem):
    return pltpu.make_async_copy(src.at[pl.ds(pl.multiple_of(src_row * SLAB, SLAB), SLAB)],
                                 dst.at[pl.ds(pl.multiple_of(dst_row * SLAB, SLAB), SLAB)], sem)


def _start_tile_rows(copy_of_row):
    def trip(c, carry):
        for k in range(ROW_UNROLL):
            copy_of_row(c * ROW_UNROLL + k).start(priority=k % 2)
        return carry
    lax.fori_loop(0, TM_ROW // ROW_UNROLL, trip, 0)


def _tile_wait(src, dst, sem):
    pltpu.make_async_copy(src.at[pl.ds(0, TM_ROW * SLAB)], dst, sem).wait()


def _scatter_kernel(pos_ref, meta_ref, h_ref, o_ref, zero_tile, sem, pad_sem):
    @pl.when(pl.program_id(0) == 0)
    def _():
        zero_tile[...] = jnp.zeros_like(zero_tile)

        def tile_copy(j):
            rows = pl.ds(pl.multiple_of(j * (TM_MOE * SLAB), TM_MOE * SLAB), TM_MOE * SLAB)
            return pltpu.make_async_copy(zero_tile, o_ref.at[rows], pad_sem)

        def for_all_zero_copies(act):
            def per_bucket(b, carry):
                lo, hi = meta_ref[META_PAD_START, b], meta_ref[META_PAD_END, b]
                return lax.fori_loop(lo, hi, lambda j, c: (act(tile_copy(j)), c)[1], carry)
            lax.fori_loop(0, N_BUCKETS, per_bucket, 0)
            lax.fori_loop(meta_ref[META_N_TILES, 0], NT_MOE, lambda j, c: (act(tile_copy(j)), c)[1], 0)

        for_all_zero_copies(lambda cp: cp.start())
        for_all_zero_copies(lambda cp: cp.wait())

    _start_tile_rows(lambda r: _row_copy(h_ref, r, o_ref, pos_ref[0, 0, r], sem))
    _tile_wait(h_ref, o_ref.at[pl.ds(0, TM_ROW * SLAB)], sem)


def _scatter_call(pos, meta, h2):
    tm = TM_ROW
    return pl.pallas_call(
        _scatter_kernel,
        grid=(TOKENS // tm,),
        in_specs=[
            pl.BlockSpec((1, 1, tm), lambda i: (i, 0, 0), memory_space=pltpu.SMEM),
            pl.BlockSpec(memory_space=pltpu.SMEM),
            pl.BlockSpec((tm * SLAB, 128), lambda i: (i, 0)),
        ],
        out_specs=pl.BlockSpec(memory_space=pl.ANY),
        out_shape=jax.ShapeDtypeStruct((SORTED_ROWS * SLAB, 128), F32),
        scratch_shapes=[pltpu.VMEM((TM_MOE * SLAB, 128), F32), pltpu.SemaphoreType.DMA(()),
                        pltpu.SemaphoreType.DMA(())],
        compiler_params=_params(1),
        name="scatter",
    )(pos, meta, h2)


def _gather_kernel(pos_ref, pos_next_ref, ys_ref, x1_ref, gate2_ref, o_ref, buf, sem):
    i = pl.program_id(0)
    n = pl.num_programs(0)
    slot = lax.rem(i, 2)

    def fetch(p_ref, s):
        _start_tile_rows(lambda r: _row_copy(ys_ref, p_ref[0, 0, r], buf.at[s], r, sem.at[s]))

    @pl.when(i == 0)
    def _():
        fetch(pos_ref, 0)

    @pl.when(i + 1 < n)
    def _():
        fetch(pos_next_ref, 1 - slot)

    _tile_wait(ys_ref, buf.at[slot], sem.at[slot])
    o_ref[...] = x1_ref[...] + gate2_ref[0] * _load_slabs(buf.at[slot], TM_ROW)


def _gather_call(pos, y_sorted, x1, gate2):
    tm = TM_ROW
    per_batch = SEQ // tm
    n = TOKENS // tm
    return pl.pallas_call(
        _gather_kernel,
        grid=(n,),
        in_specs=[
            pl.BlockSpec((1, 1, tm), lambda i: (i, 0, 0), memory_space=pltpu.SMEM),
            pl.BlockSpec((1, 1, tm), lambda i: (jnp.minimum(i + 1, n - 1), 0, 0), memory_space=pltpu.SMEM),
            pl.BlockSpec(memory_space=pl.ANY),
            pl.BlockSpec((tm, D_MODEL), lambda i: (i, 0)),
            pl.BlockSpec((1, 1, D_MODEL), lambda i: (i // per_batch, 0, 0)),
        ],
        out_specs=pl.BlockSpec((tm, D_MODEL), lambda i: (i, 0)),
        out_shape=jax.ShapeDtypeStruct((TOKENS, D_MODEL), F32),
        scratch_shapes=[pltpu.VMEM((2, tm * SLAB, 128), F32), pltpu.SemaphoreType.DMA((2,))],
        compiler_params=_params(1),
        name="gather",
    )(pos, pos, y_sorted, x1, gate2)


PLAN_ROW, PLAN_E_LO, PLAN_E_HI, PLAN_GROUP = 0, 1, 2, 3


def _tile_plan(tile_bucket, n_tiles, tables):
    row = jnp.minimum(jnp.arange(NT_MOE, dtype=I32), n_tiles[0] - 1)
    bucket = tile_bucket[row]
    return jnp.concatenate([row, tables[bucket], tables[N_BUCKETS + bucket], tables[2 * N_BUCKETS + bucket]])


def _moe_kernel(plan_ref, nt_ref, *refs):
    n = MOE_TILES_PER_STEP
    h_refs = refs[0:n]
    wr_ref, brow_ref = refs[n:n + 2]
    w_refs = [refs[n + 2 + 6 * t:n + 2 + 6 * (t + 1)] for t in range(n)]
    o_ref = refs[n + 2 + 6 * n]
    first = pl.program_id(0) * n
    n_used = nt_ref[0]

    @pl.when(first < n_used)
    def _():
        tiles = []
        for t in range(n):
            e_lo = plan_ref[PLAN_E_LO * NT_MOE + first + t]
            e_hi = plan_ref[PLAN_E_HI * NT_MOE + first + t]
            g = plan_ref[PLAN_GROUP * NT_MOE + first + t]
            h = _load_slabs(h_refs[t], TM_MOE).astype(BF16)
            lg = _dot(h, wr_ref[...]) + brow_ref[...]
            wg_lo, wu_lo, _, wg_hi, wu_hi, _ = w_refs[t]
            ups = (_dot(h, wg_lo[...]), _dot(h, wu_lo[...]), _dot(h, wg_hi[...]), _dot(h, wu_hi[...]))
            tiles.append((lg, e_lo, e_hi, g, ups))

        for t, (lg, e_lo, e_hi, g, (a_lo, b_lo, a_hi, b_hi)) in enumerate(tiles):
            lane = lax.broadcasted_iota(I32, lg.shape, 1)
            is_group = lane < N_EXPERT_GROUPS
            gmax = jnp.max(jnp.where(is_group, lg, NEG_INF), axis=1, keepdims=True)
            ge = jnp.exp(lg - gmax)
            g_w = (jnp.sum(jnp.where(lane == g, ge, 0.0), axis=1, keepdims=True)
                   / jnp.sum(jnp.where(is_group, ge, 0.0), axis=1, keepdims=True))
            v_lo = jnp.sum(jnp.where(lane == EXPERT_ROW0 + e_lo, lg, 0.0), axis=1, keepdims=True)
            v_hi = jnp.sum(jnp.where(lane == EXPERT_ROW0 + e_hi, lg, 0.0), axis=1, keepdims=True)
            m = jnp.maximum(v_lo, v_hi)
            x_lo = jnp.exp(v_lo - m)
            x_hi = jnp.exp(v_hi - m)
            w_lo = x_lo / (x_lo + x_hi) * g_w
            w_hi = x_hi / (x_lo + x_hi) * g_w
            hid_lo = (a_lo * _sigmoid(a_lo) * b_lo * w_lo).astype(BF16)
            hid_hi = (a_hi * _sigmoid(a_hi) * b_hi * w_hi).astype(BF16)
            _, _, wd_lo, _, _, wd_hi = w_refs[t]
            y = _dot(hid_lo, wd_lo[...]) + _dot(hid_hi, wd_hi[...])
            if t > 0:
                y = jnp.where(first + t < n_used, y, 0.0)
            for s in range(SLAB):
                o_ref[pl.ds(t * TM_MOE * SLAB + s, TM_MOE, stride=SLAB), :] = y[:, s * 128:(s + 1) * 128]

    @pl.when(first >= n_used)
    def _():
        o_ref[...] = jnp.zeros_like(o_ref)


def _moe_call(plan, n_tiles, h_sorted, wr, br_row, w_gate, w_up, w_down):
    tm = TM_MOE
    n = MOE_TILES_PER_STEP

    def h_spec(t):
        return pl.BlockSpec((tm * SLAB, 128), lambda s, plan, nt: (plan[PLAN_ROW * NT_MOE + s * n + t], 0))

    def w_spec(shape, section, t):
        return pl.BlockSpec((None,) + shape, lambda s, plan, nt: (plan[section * NT_MOE + s * n + t], 0, 0))

    up = (D_MODEL, D_EXPERT)
    down = (D_EXPERT, D_MODEL)
    per_tile_weights = [[w_spec(up, PLAN_E_LO, t), w_spec(up, PLAN_E_LO, t), w_spec(down, PLAN_E_LO, t),
                         w_spec(up, PLAN_E_HI, t), w_spec(up, PLAN_E_HI, t), w_spec(down, PLAN_E_HI, t)]
                        for t in range(n)]
    grid_spec = pltpu.PrefetchScalarGridSpec(
        num_scalar_prefetch=2,
        grid=(NT_MOE // n,),
        in_specs=[h_spec(t) for t in range(n)] + [
            pl.BlockSpec((D_MODEL, 128), lambda s, plan, nt: (0, 0)),
            pl.BlockSpec((1, 128), lambda s, plan, nt: (0, 0)),
        ] + [spec for specs in per_tile_weights for spec in specs],
        out_specs=pl.BlockSpec((n * tm * SLAB, 128), lambda s, plan, nt: (s, 0)),
    )
    weights = [w_gate, w_up, w_down, w_gate, w_up, w_down] * n
    return pl.pallas_call(
        _moe_kernel,
        grid_spec=grid_spec,
        out_shape=jax.ShapeDtypeStruct((SORTED_ROWS * SLAB, 128), F32),
        compiler_params=_params(1),
        name="moe",
    )(plan, n_tiles, *([h_sorted] * n), wr, br_row, *weights)


def _alibi_query_lanes():
    lanes = np.zeros((1, N_HEADS * HEAD_SLOT), np.float32)
    for hd, slope in enumerate(_alibi_slopes()):
        rest = np.float32(slope * LOG2E)
        for part in range(ALIBI_PARTS):
            piece = np.float32(np.asarray(rest).astype(jnp.bfloat16))
            lanes[0, hd * HEAD_SLOT + HEAD_DIM + part] = piece
            rest = np.float32(rest - piece)
    return jnp.asarray(lanes)


def _bucket_tables():
    e_lo, e_hi, grp = [], [], []
    for g in range(N_EXPERT_GROUPS):
        for lo, hi in PAIRS:
            e_lo.append(g * EXPERTS_PER_GROUP + lo)
            e_hi.append(g * EXPERTS_PER_GROUP + hi)
            grp.append(g)
    return jnp.asarray(e_lo + e_hi + grp, I32)


def kernel(x, c, w_ada, b_ada, norm1_gain, w_in, b_branch_gate, q_norm_gain, k_norm_gain, attn_sinks, gmlp_norm_gain, gmlp_norm_bias, gmlp_w_spatial, gmlp_b_spatial, w_o_attn, w_o_gmlp, w_out, norm2_gain, w_group_router, b_group_router, w_expert_router, b_expert_router, w_expert_gate, w_expert_up, w_expert_down):
    depth = w_ada.shape[0]
    x2 = x.reshape(TOKENS, D_MODEL)
    tables = _bucket_tables()
    q_aug = _alibi_query_lanes()
    row = lambda v: v.reshape(1, -1)
    for l in range(depth):
        mod = _mod_call(c, w_ada[l], b_ada[l])
        shift1, scale1, gate1, shift2, scale2, gate2 = [
            m.reshape(BATCH, 1, D_MODEL) for m in jnp.split(mod, 6, axis=-1)]

        slot_pad = jnp.zeros((HEAD_SLOT - HEAD_DIM,), F32)
        q_gain_row = row(jnp.tile(jnp.concatenate([q_norm_gain[l] * (HEAD_DIM ** -0.5 * LOG2E), slot_pad]), N_HEADS))
        k_gain_row = row(jnp.tile(jnp.concatenate([k_norm_gain[l], slot_pad]), N_KV_HEADS))
        w_in_b = _round_call(w_in, l)
        q, k, vt, u, vg, gates = _inproj_call(
            x2, scale1, shift1, row(norm1_gain[l]), w_in_b, w_in_b[:, V0:V1].T,
            q_gain_row, k_gain_row, row(gmlp_norm_gain[l]), row(gmlp_norm_bias[l]), row(b_branch_gate[l]), q_aug)

        stack = lambda w: w[l].reshape(N_EXPERTS * w.shape[3], w.shape[4])
        y_attn, wg_b, wu_b, wd_b, wo_attn_b, wo_gmlp_b, w_out_b = _attn_call(
            attn_sinks[l], q, k, vt, stack(w_expert_gate), stack(w_expert_up), stack(w_expert_down),
            w_o_attn[l], w_o_gmlp[l], w_out[l])

        bias_map = jnp.repeat(gmlp_b_spatial[l].T, GMLP_WIDTH // GMLP_GROUPS, axis=1)
        e0, e1 = EXPERT_ROW0, EXPERT_ROW0 + N_EXPERTS
        wr = jnp.zeros((D_MODEL, 128), F32)
        wr = wr.at[:, 0:N_EXPERT_GROUPS].set(w_group_router[l]).at[:, e0:e1].set(w_expert_router[l]).astype(BF16)
        br_row = jnp.zeros((1, 128), F32)
        br_row = br_row.at[0, 0:N_EXPERT_GROUPS].set(b_group_router[l]).at[0, e0:e1].set(b_expert_router[l])
        x1, h2p, buckets, totals = _mixer_call(
            x2, y_attn, u, vg, gates, gmlp_w_spatial[l], bias_map, wo_attn_b, wo_gmlp_b, w_out_b,
            gate1, row(norm2_gain[l]), scale2, shift2,
            wr[:, 0:BUCKET_ROWS].T, br_row[:, 0:BUCKET_ROWS].T)

        pos, meta = _rank_call(buckets, totals)
        pos = pos.reshape(TOKENS // TM_ROW, 1, TM_ROW)
        h_sorted = _scatter_call(pos, meta, h2p)
        per_expert = lambda w: w.reshape(N_EXPERTS, w.shape[0] // N_EXPERTS, w.shape[1])

        n_tiles = meta[META_N_TILES, 0:1]
        y_sorted = _moe_call(
            _tile_plan(meta[META_TILE_BUCKET], n_tiles, tables), n_tiles, h_sorted, wr, br_row,
            per_expert(wg_b), per_expert(wu_b), per_expert(wd_b))

        x2 = _gather_call(pos, y_sorted, x1, gate2)
    return x2.reshape(x.shape)
```

```python
import functools

import jax
import jax.numpy as jnp
import numpy as np
from jax import lax
from jax.experimental import pallas as pl
from jax.experimental.pallas import tpu as pltpu

D_MODEL = 1024
BATCH = 8
SEQ = 4096
TOKENS = BATCH * SEQ
N_HEADS = 16
N_KV_HEADS = 4
HEAD_DIM = 64
Q_PER_KV = N_HEADS // N_KV_HEADS
BLOCK = 128
HEAD_SLOT = 128
ALIBI_PARTS = 3
ATTN_WIDTH = N_HEADS * HEAD_DIM
KV_WIDTH = N_KV_HEADS * HEAD_DIM
GMLP_WIDTH = 1024
GMLP_GROUPS = 8
GMLP_CHUNK = 128
N_EXPERT_GROUPS = 4
EXPERTS_PER_GROUP = 4
N_EXPERTS = N_EXPERT_GROUPS * EXPERTS_PER_GROUP
D_EXPERT = 512
IN_WIDTH = ATTN_WIDTH + 2 * KV_WIDTH + 2 * GMLP_WIDTH + 2 * D_MODEL
EPS = 1e-6
NEG_INF = -1e30
LOG2E = float(np.float32(np.log2(np.e)))

Q0, Q1 = 0, ATTN_WIDTH
K0, K1 = Q1, Q1 + KV_WIDTH
V0, V1 = K1, K1 + KV_WIDTH
U0, U1 = V1, V1 + GMLP_WIDTH
G0, G1 = U1, U1 + GMLP_WIDTH
B0, B1 = G1, G1 + 2 * D_MODEL

PAIRS = ((0, 1), (0, 2), (0, 3), (1, 2), (1, 3), (2, 3))
N_BUCKETS = N_EXPERT_GROUPS * len(PAIRS)
BUCKET_ROWS = 32
EXPERT_ROW0 = 8

TM_PROJ = 1024
PROJ_ROWS = 512
PROJ_CHUNK = 512
TM_ATTN = 1024
TM_MIX = 1024
MIX_ROWS = 512
TM_MOE = 256
MOE_TILES_PER_STEP = 2
TM_ROW = 512
ROUND_ROWS = 16
SCORES_AHEAD = 6
ROW_UNROLL = 16
NT_MOE = TOKENS // TM_MOE + N_BUCKETS
SORTED_ROWS = NT_MOE * TM_MOE
SLAB = D_MODEL // 128
META_LANES = 256
assert NT_MOE <= META_LANES

VMEM_LIMIT = 56 * 1024 * 1024

F32 = jnp.float32
BF16 = jnp.bfloat16
I32 = jnp.int32


def _alibi_slopes():
    return [float(np.float32(2.0 ** (-8.0 * (i + 1) / N_HEADS))) for i in range(N_HEADS)]


def _params(n_axes):
    return pltpu.CompilerParams(dimension_semantics=("arbitrary",) * n_axes, vmem_limit_bytes=VMEM_LIMIT)


def _sigmoid(x):
    return 1.0 / (1.0 + jnp.exp(-x))


def _gelu(x):
    return 0.5 * x * (1.0 + lax.erf(x * np.float32(1.0 / np.sqrt(2.0))))


def _dot(a, b):
    return jnp.dot(a, b, preferred_element_type=F32)


def _dot_nt(a, b):
    return lax.dot_general(a, b, (((1,), (1,)), ((), ())), preferred_element_type=F32)


def _mod_kernel(c_ref, w_ref, b_ref, o_ref):
    c = c_ref[...]
    ca = c * _sigmoid(c)
    o_ref[...] = _dot(ca.astype(BF16), w_ref[...].astype(BF16)) + b_ref[...]


def _mod_call(c, w_ada, b_ada):
    n = w_ada.shape[1]
    bn = 1536
    return pl.pallas_call(
        _mod_kernel,
        grid=(n // bn,),
        in_specs=[
            pl.BlockSpec((BATCH, D_MODEL), lambda j: (0, 0)),
            pl.BlockSpec((D_MODEL, bn), lambda j: (0, j)),
            pl.BlockSpec((1, bn), lambda j: (0, j)),
        ],
        out_specs=pl.BlockSpec((BATCH, bn), lambda j: (0, j)),
        out_shape=jax.ShapeDtypeStruct((BATCH, n), F32),
        compiler_params=_params(1),
        name="mod",
    )(c, w_ada, b_ada.reshape(1, n))


def _round_kernel(w_ref, o_ref):
    o_ref[...] = w_ref[...].astype(BF16)


def _round_call(w, layer):
    _, k, n = w.shape
    bn = 512
    return pl.pallas_call(
        _round_kernel,
        grid=(n // bn,),
        in_specs=[pl.BlockSpec((None, k, bn), lambda j: (layer, 0, j))],
        out_specs=pl.BlockSpec((k, bn), lambda j: (0, j)),
        out_shape=jax.ShapeDtypeStruct((k, n), BF16),
        compiler_params=_params(1),
        name="round_weight",
    )(w)


def _inproj_kernel(x_ref, sc_ref, sh_ref, ng_ref, w_ref, wvt_ref, qg_ref, kg_ref, lng_ref, lnb_ref, bg_ref, qaug_ref,
                   q_o, k_o, vt_o, u_o, vg_o, gt_o):
    low_half = lax.broadcasted_iota(I32, (1, HEAD_SLOT), 1) < HEAD_DIM
    for r0 in range(0, TM_PROJ, PROJ_ROWS):
        _project_rows(slice(r0, r0 + PROJ_ROWS), low_half, x_ref, sc_ref, sh_ref, ng_ref, w_ref, wvt_ref, qg_ref,
                      kg_ref, lng_ref, lnb_ref, bg_ref, qaug_ref, q_o, k_o, vt_o, u_o, vg_o, gt_o)


def _project_rows(rows, low_half, x_ref, sc_ref, sh_ref, ng_ref, w_ref, wvt_ref, qg_ref, kg_ref, lng_ref, lnb_ref,
                  bg_ref, qaug_ref, q_o, k_o, vt_o, u_o, vg_o, gt_o):
    x = x_ref[rows, :]
    ms = jnp.mean(x * x, axis=-1, keepdims=True)
    h = (x * lax.rsqrt(ms + EPS)) * ng_ref[...]
    h = h * (1.0 + sc_ref[0]) + sh_ref[0]
    hb = h.astype(BF16)

    def proj(c0, c1):
        return _dot(hb, w_ref[:, c0:c1])

    def store_normed_heads(raw, gain_ref, spare_ref, o_ref):
        for p in range(raw.shape[1] // HEAD_SLOT):
            pair = raw[:, p * HEAD_SLOT:(p + 1) * HEAD_SLOT]
            for hd, head in ((2 * p, pair), (2 * p + 1, pltpu.roll(pair, HEAD_DIM, 1))):
                cols = slice(hd * HEAD_SLOT, (hd + 1) * HEAD_SLOT)
                xh = jnp.where(low_half, head, 0.0)
                r = lax.rsqrt(jnp.sum(xh * xh, axis=-1, keepdims=True) * (1.0 / HEAD_DIM) + EPS)
                y = xh * r * gain_ref[:, cols]
                if spare_ref is not None:
                    y = y + spare_ref[:, cols]
                o_ref[rows, cols] = y.astype(BF16)

    store_normed_heads(proj(Q0, Q1), qg_ref, qaug_ref, q_o)
    store_normed_heads(proj(K0, K1), kg_ref, None, k_o)
    vt_o[:, rows] = _dot_nt(wvt_ref[...], hb).astype(BF16)

    cw = PROJ_CHUNK

    def gate_chunk(c):
        cols = slice(c * cw, (c + 1) * cw)
        gt_o[rows, cols] = _sigmoid(proj(B0 + c * cw, B0 + (c + 1) * cw) + bg_ref[:, cols]).astype(BF16)

    vg_parts = []
    for c in range(GMLP_WIDTH // cw):
        u_o[rows, c * cw:(c + 1) * cw] = _gelu(proj(U0 + c * cw, U0 + (c + 1) * cw)).astype(BF16)
        gate_chunk(2 * c)
        vg_parts.append(_gelu(proj(G0 + c * cw, G0 + (c + 1) * cw)))
        gate_chunk(2 * c + 1)
    vg = jnp.concatenate(vg_parts, axis=1)
    mu = jnp.mean(vg, axis=-1, keepdims=True)
    vc = vg - mu
    var = jnp.mean(vc * vc, axis=-1, keepdims=True)
    vg_o[rows, :] = (vc * lax.rsqrt(var + EPS) * lng_ref[...] + lnb_ref[...]).astype(BF16)


def _inproj_call(x2, scale1, shift1, norm_gain, w_in, wv_t, q_gain_row, k_gain_row, ln_gain, ln_bias, b_gate, q_aug):
    tm = TM_PROJ
    per_batch = SEQ // tm
    row = lambda n: pl.BlockSpec((1, n), lambda i: (0, 0))
    modrow = pl.BlockSpec((1, 1, D_MODEL), lambda i: (i // per_batch, 0, 0))
    tile = lambda n: pl.BlockSpec((tm, n), lambda i: (i, 0))
    out = lambda n: jax.ShapeDtypeStruct((TOKENS, n), BF16)
    return pl.pallas_call(
        _inproj_kernel,
        grid=(TOKENS // tm,),
        in_specs=[
            tile(D_MODEL), modrow, modrow, row(D_MODEL),
            pl.BlockSpec((D_MODEL, IN_WIDTH), lambda i: (0, 0)),
            pl.BlockSpec((KV_WIDTH, D_MODEL), lambda i: (0, 0)),
            row(N_HEADS * HEAD_SLOT), row(N_KV_HEADS * HEAD_SLOT), row(GMLP_WIDTH), row(GMLP_WIDTH),
            row(2 * D_MODEL), row(N_HEADS * HEAD_SLOT),
        ],
        out_specs=[tile(N_HEADS * HEAD_SLOT), tile(N_KV_HEADS * HEAD_SLOT),
                   pl.BlockSpec((KV_WIDTH, tm), lambda i: (0, i)),
                   tile(GMLP_WIDTH), tile(GMLP_WIDTH), tile(2 * D_MODEL)],
        out_shape=[out(N_HEADS * HEAD_SLOT), out(N_KV_HEADS * HEAD_SLOT),
                   jax.ShapeDtypeStruct((KV_WIDTH, TOKENS), BF16),
                   out(GMLP_WIDTH), out(GMLP_WIDTH), out(2 * D_MODEL)],
        compiler_params=_params(1),
        name="inproj",
    )(x2, scale1, shift1, norm_gain, w_in, wv_t, q_gain_row, k_gain_row, ln_gain, ln_bias, b_gate, q_aug)


def _attn_kernel(group_sizes, sink_ref, q_ref, kc_ref, kp_ref, vc_ref, vp_ref, *rest):
    i = pl.program_id(1)
    n_w = sum(group_sizes)
    w_refs, o_ref, w_outs = rest[:n_w], rest[n_w], rest[n_w + 1:]
    slopes = _alibi_slopes()
    kj = lax.broadcasted_iota(I32, (BLOCK, 2 * BLOCK), 0)
    qi = lax.broadcasted_iota(I32, (BLOCK, 2 * BLOCK), 1) & (BLOCK - 1)
    own = kj <= qi
    from_own = jnp.where(own, 1.0, 0.0).astype(BF16)
    from_prev = jnp.where(own, 0.0, 1.0).astype(BF16)
    has_prev = i > 0
    low_half = lax.broadcasted_iota(I32, (1, HEAD_SLOT), 1) < HEAD_DIM
    k_lane = lax.broadcasted_iota(I32, (2 * BLOCK, HEAD_SLOT), 1)
    k_aug = jnp.where((k_lane >= HEAD_DIM) & (k_lane < HEAD_DIM + ALIBI_PARTS),
                      lax.broadcasted_iota(I32, (2 * BLOCK, HEAD_SLOT), 0), 0).astype(F32).astype(BF16)
    q_pos = (lax.broadcasted_iota(I32, (1, BLOCK), 1) + BLOCK).astype(F32)

    keys_of = {}

    def scores(item):
        sb, kv, pr = item
        r0, r1 = sb * BLOCK, (sb + 1) * BLOCK
        if (sb, kv) not in keys_of:
            cols = slice(kv * HEAD_SLOT, (kv + 1) * HEAD_SLOT)
            k_prev = kp_ref[:, cols] if sb == 0 else kc_ref[r0 - BLOCK:r0, cols]
            keys_of[sb, kv] = jnp.where(low_half, jnp.concatenate([k_prev, kc_ref[r0:r1, cols]], axis=0), k_aug)
        ha = kv * Q_PER_KV + 2 * pr
        queries = jnp.concatenate([q_ref[r0:r1, ha * HEAD_SLOT:(ha + 1) * HEAD_SLOT],
                                   q_ref[r0:r1, (ha + 1) * HEAD_SLOT:(ha + 2) * HEAD_SLOT]], axis=0)
        s = _dot_nt(keys_of[sb, kv], queries)
        s_prev = s[0:BLOCK, :]
        if sb == 0:
            s_prev = jnp.where(has_prev, s_prev, NEG_INF)
        return jnp.where(own, s[BLOCK:2 * BLOCK, :], s_prev)

    def attend(item, s):
        sb, kv, pr = item
        r0, r1 = sb * BLOCK, (sb + 1) * BLOCK
        rows = slice(kv * HEAD_DIM, (kv + 1) * HEAD_DIM)
        v_prev = vp_ref[rows, :] if sb == 0 else vc_ref[rows, r0 - BLOCK:r0]
        vt = jnp.concatenate([v_prev, vc_ref[rows, r0:r1]], axis=1)
        ha = kv * Q_PER_KV + 2 * pr
        hb = ha + 1
        sink = LOG2E * jnp.concatenate(
            [sink_ref[ha] + slopes[ha] * q_pos, sink_ref[hb] + slopes[hb] * q_pos], axis=1)
        m = jnp.maximum(jnp.max(s, axis=0, keepdims=True), sink)
        p = jnp.exp2(s - m)
        den = jnp.sum(p, axis=0, keepdims=True) + jnp.exp2(sink - m)
        pb = p.astype(BF16)
        p_keys = jnp.concatenate([pb * from_prev, pb * from_own], axis=0)
        o = _dot(vt, p_keys) / den
        pair = jnp.concatenate([o[:, 0:BLOCK], o[:, BLOCK:2 * BLOCK]], axis=0).T
        o_ref[r0:r1, ha * HEAD_DIM:(hb + 1) * HEAD_DIM] = pair.astype(BF16)

    items = [(sb, kv, pr) for sb in range(TM_ATTN // BLOCK) for kv in range(N_KV_HEADS)
             for pr in range(Q_PER_KV // 2)]
    rounding, first = [], 0
    for size, dst in zip(group_sizes, w_outs):
        c0 = 0
        for src in w_refs[first:first + size]:
            rounding += [(src, dst, r0, c0) for r0 in range(0, src.shape[0], ROUND_ROWS)]
            c0 += src.shape[1]
        first += size
    pending = []
    for n in range(len(items) + SCORES_AHEAD):
        if n < len(items):
            pending.append(scores(items[n]))
            for src, dst, r0, c0 in rounding[n * len(rounding) // len(items):(n + 1) * len(rounding) // len(items)]:
                dst[r0:r0 + ROUND_ROWS, c0:c0 + src.shape[1]] = src[r0:r0 + ROUND_ROWS, :].astype(BF16)
        if n >= SCORES_AHEAD:
            attend(items[n - SCORES_AHEAD], pending.pop(0))


def _attn_call(sinks, q, k, vt, *w_groups):
    w_in = [w for group in w_groups for w in group]
    group_shapes = [(group[0].shape[0], sum(w.shape[1] for w in group)) for group in w_groups]
    tq = TM_ATTN
    per_batch = SEQ // tq
    steps = BATCH * per_batch
    blocks_per_tile = tq // BLOCK
    q_width = N_HEADS * HEAD_SLOT
    k_width = N_KV_HEADS * HEAD_SLOT
    cur = lambda n: pl.BlockSpec((tq, n), lambda b, i: (b * per_batch + i, 0))
    prev_block = lambda b, i: b * (SEQ // BLOCK) + jnp.maximum(i * blocks_per_tile - 1, 0)
    rows_block = lambda shape: pl.BlockSpec((shape[0] // steps, shape[1]), lambda b, i: (b * per_batch + i, 0))
    return pl.pallas_call(
        functools.partial(_attn_kernel, tuple(len(group) for group in w_groups)),
        grid=(BATCH, per_batch),
        in_specs=[
            pl.BlockSpec(memory_space=pltpu.SMEM),
            cur(q_width), cur(k_width),
            pl.BlockSpec((BLOCK, k_width), lambda b, i: (prev_block(b, i), 0)),
            pl.BlockSpec((KV_WIDTH, tq), lambda b, i: (0, b * per_batch + i)),
            pl.BlockSpec((KV_WIDTH, BLOCK), lambda b, i: (0, prev_block(b, i))),
        ] + [rows_block(w.shape) for w in w_in],
        out_specs=[cur(ATTN_WIDTH)] + [rows_block(shape) for shape in group_shapes],
        out_shape=[jax.ShapeDtypeStruct((TOKENS, ATTN_WIDTH), BF16)]
        + [jax.ShapeDtypeStruct(shape, BF16) for shape in group_shapes],
        compiler_params=_params(2),
        name="attn",
    )(sinks, q, k, k, vt, vt, *w_in)


def _first_argmax(vals):
    m = vals[0]
    for v in vals[1:]:
        m = jnp.maximum(m, v)
    idx = jnp.full(m.shape, len(vals) - 1, I32)
    for k in range(len(vals) - 2, -1, -1):
        idx = jnp.where(vals[k] == m, k, idx)
    return m, idx


def _mixer_kernel(x_ref, ya_ref, u_ref, vg_ref, gt_ref, ws_ref, bmap_ref, woa_ref, wog_ref, wout_ref, gate1_ref,
                  ng_ref, sc_ref, sh_ref, wrt_ref, brt_ref, x1_o, h2_o, bk_o, cnt_o, cnt):
    tm = TM_MIX
    t_idx = lax.broadcasted_iota(I32, (GMLP_CHUNK, GMLP_CHUNK), 0)
    s_idx = lax.broadcasted_iota(I32, (GMLP_CHUNK, GMLP_CHUNK), 1)
    ws = [jnp.where(t_idx >= s_idx, ws_ref[g], 0.0).astype(BF16) for g in range(GMLP_GROUPS)]
    gc = GMLP_WIDTH // GMLP_GROUPS

    def mix_rows(r0):
        rows = slice(r0, r0 + MIX_ROWS)
        chunks = []
        for c0 in range(r0, r0 + MIX_ROWS, GMLP_CHUNK):
            c1 = c0 + GMLP_CHUNK
            mixed = jnp.concatenate(
                [_dot(ws[g], vg_ref[c0:c1, g * gc:(g + 1) * gc]) for g in range(GMLP_GROUPS)], axis=1)
            mixed = mixed + bmap_ref[...]
            chunks.append((u_ref[c0:c1, :].astype(F32) * mixed).astype(BF16))
        y_gmlp = jnp.concatenate(chunks, axis=0)
        pa = _dot(ya_ref[rows, :], woa_ref[...])
        pg = _dot(y_gmlp, wog_ref[...])
        merged = gt_ref[rows, 0:D_MODEL].astype(F32) * pa + gt_ref[rows, D_MODEL:2 * D_MODEL].astype(F32) * pg
        x1 = x_ref[rows, :] + gate1_ref[0] * _dot(merged.astype(BF16), wout_ref[...])
        x1_o[rows, :] = x1

        ms = jnp.mean(x1 * x1, axis=-1, keepdims=True)
        h2 = (x1 * lax.rsqrt(ms + EPS)) * ng_ref[...]
        h2 = h2 * (1.0 + sc_ref[0]) + sh_ref[0]
        for s in range(SLAB):
            h2_o[pl.ds(r0 * SLAB + s, MIX_ROWS, stride=SLAB), :] = h2[:, s * 128:(s + 1) * 128]

        lg = _dot_nt(wrt_ref[...], h2.astype(BF16)) + brt_ref[...]
        _, gi = _first_argmax([lg[r:r + 1, :] for r in range(N_EXPERT_GROUPS)])
        el = [lg[EXPERT_ROW0 + r:EXPERT_ROW0 + r + 1, :] for r in range(N_EXPERTS)]
        eg = []
        for k in range(EXPERTS_PER_GROUP):
            v = el[(N_EXPERT_GROUPS - 1) * EXPERTS_PER_GROUP + k]
            for g in range(N_EXPERT_GROUPS - 2, -1, -1):
                v = jnp.where(gi == g, el[g * EXPERTS_PER_GROUP + k], v)
            eg.append(v)
        _, i1 = _first_argmax(eg)
        _, i2 = _first_argmax([jnp.where(i1 == k, -3e38, eg[k]) for k in range(EXPERTS_PER_GROUP)])
        lo = jnp.minimum(i1, i2)
        hi = jnp.maximum(i1, i2)
        pair = jnp.where(lo == 0, hi - 1, jnp.where(lo == 1, hi + 1, 5))
        bucket = gi * len(PAIRS) + pair
        bk_o[0, :, rows] = bucket
        in_bucket = lax.broadcasted_iota(I32, (BUCKET_ROWS, MIX_ROWS), 0) == bucket
        return jnp.sum(jnp.where(in_bucket, 1.0, 0.0), axis=1, keepdims=True)

    counts = [mix_rows(r0) for r0 in range(0, tm, MIX_ROWS)]

    @pl.when(pl.program_id(0) == 0)
    def _():
        cnt[...] = jnp.zeros_like(cnt)
    cnt[...] += sum(counts[1:], counts[0])
    cnt_o[...] = cnt[...]


def _mixer_call(x2, y_attn, u, vg, gates, w_spatial, bias_map, wo_attn, wo_gmlp, w_out, gate1, norm_gain, scale2, shift2,
                wr_t, br_t):
    tm = TM_MIX
    per_batch = SEQ // tm
    nt = TOKENS // tm
    tile = lambda n: pl.BlockSpec((tm, n), lambda i: (i, 0))
    full = lambda shape: pl.BlockSpec(shape, lambda i: (0,) * len(shape))
    once = lambda shape: pl.BlockSpec(shape, lambda i: (0,) * len(shape), pipeline_mode=pl.Buffered(1))
    modrow = pl.BlockSpec((1, 1, D_MODEL), lambda i: (i // per_batch, 0, 0))
    return pl.pallas_call(
        _mixer_kernel,
        grid=(nt,),
        in_specs=[
            tile(D_MODEL), tile(ATTN_WIDTH), tile(GMLP_WIDTH), tile(GMLP_WIDTH), tile(2 * D_MODEL),
            full((GMLP_GROUPS, GMLP_CHUNK, GMLP_CHUNK)), full((GMLP_CHUNK, GMLP_WIDTH)),
            once((ATTN_WIDTH, D_MODEL)), once((GMLP_WIDTH, D_MODEL)), once((D_MODEL, D_MODEL)),
            modrow, full((1, D_MODEL)), modrow, modrow,
            full((BUCKET_ROWS, D_MODEL)), full((BUCKET_ROWS, 1)),
        ],
        out_specs=[tile(D_MODEL), pl.BlockSpec((tm * SLAB, 128), lambda i: (i, 0)),
                   pl.BlockSpec((1, 1, tm), lambda i: (i, 0, 0)), full((BUCKET_ROWS, 128))],
        out_shape=[
            jax.ShapeDtypeStruct((TOKENS, D_MODEL), F32),
            jax.ShapeDtypeStruct((TOKENS * SLAB, 128), F32),
            jax.ShapeDtypeStruct((nt, 1, tm), I32),
            jax.ShapeDtypeStruct((BUCKET_ROWS, 128), F32),
        ],
        scratch_shapes=[pltpu.VMEM((BUCKET_ROWS, 128), F32)],
        compiler_params=_params(1),
        name="mixer",
    )(x2, y_attn, u, vg, gates, w_spatial, bias_map, wo_attn, wo_gmlp, w_out, gate1, norm_gain, scale2, shift2,
      wr_t, br_t)


META_TILE_BUCKET, META_N_TILES, META_PAD_START, META_PAD_END = 0, 1, 2, 3


def _rank_kernel(bk_ref, total_ref, pos_o, meta_o, seen, off):
    tm = TM_MIX
    i = pl.program_id(0)
    bk = bk_ref[0]
    onehot = lax.broadcasted_iota(I32, (BUCKET_ROWS, tm), 0) == bk

    @pl.when(i == 0)
    def _():
        total = total_ref[...]
        tiles = jnp.floor((total + (TM_MOE - 1)) * (1.0 / TM_MOE))
        r = lax.broadcasted_iota(I32, (BUCKET_ROWS, BUCKET_ROWS), 0)
        c = lax.broadcasted_iota(I32, (BUCKET_ROWS, BUCKET_ROWS), 1)
        before = jnp.where(c < r, 1.0, 0.0).astype(BF16)
        first_tile = _dot(before, tiles.astype(BF16))
        first_row = first_tile * TM_MOE
        off[...] = first_row
        seen[...] = jnp.zeros_like(seen)
        end_tile = (first_tile + tiles)[:, 0:1]
        lane = lax.broadcasted_iota(I32, (BUCKET_ROWS, META_LANES), 1)
        bsub = lax.broadcasted_iota(I32, (BUCKET_ROWS, META_LANES), 0)
        is_bucket = bsub < N_BUCKETS
        tile_bucket = jnp.sum(jnp.where((lane.astype(F32) >= end_tile) & is_bucket, 1.0, 0.0), axis=0, keepdims=True)
        as_row = lambda col: jnp.sum(jnp.where((bsub == lane) & is_bucket, col, 0.0), axis=0, keepdims=True)
        n_tiles = jnp.sum(jnp.where(bsub == N_BUCKETS - 1, end_tile, 0.0), axis=0, keepdims=True)
        pad_start = as_row(end_tile - jnp.minimum(tiles[:, 0:1], 1.0))
        pad_end = as_row(end_tile)
        row = lax.broadcasted_iota(I32, (8, META_LANES), 0)
        meta = jnp.zeros((8, META_LANES), F32)
        for k, v in ((META_TILE_BUCKET, tile_bucket), (META_N_TILES, n_tiles), (META_PAD_START, pad_start),
                     (META_PAD_END, pad_end)):
            meta = jnp.where(row == k, v, meta)
        meta_o[...] = meta.astype(I32)

    s_idx = lax.broadcasted_iota(I32, (tm, tm), 0)
    t_idx = lax.broadcasted_iota(I32, (tm, tm), 1)
    upto = jnp.where(s_idx <= t_idx, 1.0, 0.0).astype(BF16)
    incl = _dot(jnp.where(onehot, 1.0, 0.0).astype(BF16), upto)
    base = off[:, 0:1] + seen[:, 0:1]
    posf = jnp.sum(jnp.where(onehot, base + incl - 1.0, 0.0), axis=0, keepdims=True)
    pos_o[0] = posf.astype(I32)
    seen[...] += incl[:, tm - 1:tm]


def _rank_call(buckets, totals):
    nt, _, tm = buckets.shape
    return pl.pallas_call(
        _rank_kernel,
        grid=(nt,),
        in_specs=[pl.BlockSpec((1, 1, tm), lambda i: (i, 0, 0)),
                  pl.BlockSpec((BUCKET_ROWS, 128), lambda i: (0, 0))],
        out_specs=[
            pl.BlockSpec((1, 1, tm), lambda i: (i, 0, 0)),
            pl.BlockSpec((8, META_LANES), lambda i: (0, 0)),
        ],
        out_shape=[
            jax.ShapeDtypeStruct((nt, 1, tm), I32),
            jax.ShapeDtypeStruct((8, META_LANES), I32),
        ],
        scratch_shapes=[pltpu.VMEM((BUCKET_ROWS, 128), F32), pltpu.VMEM((BUCKET_ROWS, 128), F32)],
        compiler_params=_params(1),
        name="rank",
    )(buckets, totals)


def _store_slabs(ref, x):
    n = x.shape[0]
    for s in range(SLAB):
        ref[pl.ds(s, n, stride=SLAB), :] = x[:, s * 128:(s + 1) * 128]


def _load_slabs(ref, n):
    return jnp.concatenate([ref[pl.ds(s, n, stride=SLAB), :] for s in range(SLAB)], axis=1)


def _row_copy(src, src_row, dst, dst_row, sem):
    return pltpu.make_async_copy(src.at[pl.ds(pl.multiple_of(src_row * SLAB, SLAB), SLAB)],
                                 dst.at[pl.ds(pl.multiple_of(dst_row * SLAB, SLAB), SLAB)], sem)


def _start_tile_rows(copy_of_row):
    def trip(c, carry):
        for k in range(ROW_UNROLL):
            copy_of_row(c * ROW_UNROLL + k).start(priority=k % 2)
        return carry
    lax.fori_loop(0, TM_ROW // ROW_UNROLL, trip, 0)


def _tile_wait(src, dst, sem):
    pltpu.make_async_copy(src.at[pl.ds(0, TM_ROW * SLAB)], dst, sem).wait()


def _scatter_kernel(pos_ref, meta_ref, h_ref, o_ref, zero_tile, sem, pad_sem):
    @pl.when(pl.program_id(0) == 0)
    def _():
        zero_tile[...] = jnp.zeros_like(zero_tile)

        def tile_copy(j):
            rows = pl.ds(pl.multiple_of(j * (TM_MOE * SLAB), TM_MOE * SLAB), TM_MOE * SLAB)
            return pltpu.make_async_copy(zero_tile, o_ref.at[rows], pad_sem)

        def for_all_zero_copies(act):
            def per_bucket(b, carry):
                lo, hi = meta_ref[META_PAD_START, b], meta_ref[META_PAD_END, b]
                return lax.fori_loop(lo, hi, lambda j, c: (act(tile_copy(j)), c)[1], carry)
            lax.fori_loop(0, N_BUCKETS, per_bucket, 0)
            lax.fori_loop(meta_ref[META_N_TILES, 0], NT_MOE, lambda j, c: (act(tile_copy(j)), c)[1], 0)

        for_all_zero_copies(lambda cp: cp.start())
        for_all_zero_copies(lambda cp: cp.wait())

    _start_tile_rows(lambda r: _row_copy(h_ref, r, o_ref, pos_ref[0, 0, r], sem))
    _tile_wait(h_ref, o_ref.at[pl.ds(0, TM_ROW * SLAB)], sem)


def _scatter_call(pos, meta, h2):
    tm = TM_ROW
    return pl.pallas_call(
        _scatter_kernel,
        grid=(TOKENS // tm,),
        in_specs=[
            pl.BlockSpec((1, 1, tm), lambda i: (i, 0, 0), memory_space=pltpu.SMEM),
            pl.BlockSpec(memory_space=pltpu.SMEM),
            pl.BlockSpec((tm * SLAB, 128), lambda i: (i, 0)),
        ],
        out_specs=pl.BlockSpec(memory_space=pl.ANY),
        out_shape=jax.ShapeDtypeStruct((SORTED_ROWS * SLAB, 128), F32),
        scratch_shapes=[pltpu.VMEM((TM_MOE * SLAB, 128), F32), pltpu.SemaphoreType.DMA(()),
                        pltpu.SemaphoreType.DMA(())],
        compiler_params=_params(1),
        name="scatter",
    )(pos, meta, h2)


def _gather_kernel(pos_ref, pos_next_ref, ys_ref, x1_ref, gate2_ref, o_ref, buf, sem):
    i = pl.program_id(0)
    n = pl.num_programs(0)
    slot = lax.rem(i, 2)

    def fetch(p_ref, s):
        _start_tile_rows(lambda r: _row_copy(ys_ref, p_ref[0, 0, r], buf.at[s], r, sem.at[s]))

    @pl.when(i == 0)
    def _():
        fetch(pos_ref, 0)

    @pl.when(i + 1 < n)
    def _():
        fetch(pos_next_ref, 1 - slot)

    _tile_wait(ys_ref, buf.at[slot], sem.at[slot])
    o_ref[...] = x1_ref[...] + gate2_ref[0] * _load_slabs(buf.at[slot], TM_ROW)


def _gather_call(pos, y_sorted, x1, gate2):
    tm = TM_ROW
    per_batch = SEQ // tm
    n = TOKENS // tm
    return pl.pallas_call(
        _gather_kernel,
        grid=(n,),
        in_specs=[
            pl.BlockSpec((1, 1, tm), lambda i: (i, 0, 0), memory_space=pltpu.SMEM),
            pl.BlockSpec((1, 1, tm), lambda i: (jnp.minimum(i + 1, n - 1), 0, 0), memory_space=pltpu.SMEM),
            pl.BlockSpec(memory_space=pl.ANY),
            pl.BlockSpec((tm, D_MODEL), lambda i: (i, 0)),
            pl.BlockSpec((1, 1, D_MODEL), lambda i: (i // per_batch, 0, 0)),
        ],
        out_specs=pl.BlockSpec((tm, D_MODEL), lambda i: (i, 0)),
        out_shape=jax.ShapeDtypeStruct((TOKENS, D_MODEL), F32),
        scratch_shapes=[pltpu.VMEM((2, tm * SLAB, 128), F32), pltpu.SemaphoreType.DMA((2,))],
        compiler_params=_params(1),
        name="gather",
    )(pos, pos, y_sorted, x1, gate2)


PLAN_ROW, PLAN_E_LO, PLAN_E_HI, PLAN_GROUP = 0, 1, 2, 3


def _tile_plan(tile_bucket, n_tiles, tables):
    row = jnp.minimum(jnp.arange(NT_MOE, dtype=I32), n_tiles[0] - 1)
    bucket = tile_bucket[row]
    return jnp.concatenate([row, tables[bucket], tables[N_BUCKETS + bucket], tables[2 * N_BUCKETS + bucket]])


def _moe_kernel(plan_ref, nt_ref, *refs):
    n = MOE_TILES_PER_STEP
    h_refs = refs[0:n]
    wr_ref, brow_ref = refs[n:n + 2]
    w_refs = [refs[n + 2 + 4 * t:n + 2 + 4 * (t + 1)] for t in range(n)]
    o_ref = refs[n + 2 + 4 * n]
    first = pl.program_id(0) * n
    n_used = nt_ref[0]

    @pl.when(first < n_used)
    def _():
        tiles = []
        for t in range(n):
            e_lo = plan_ref[PLAN_E_LO * NT_MOE + first + t]
            e_hi = plan_ref[PLAN_E_HI * NT_MOE + first + t]
            g = plan_ref[PLAN_GROUP * NT_MOE + first + t]
            h = _load_slabs(h_refs[t], TM_MOE).astype(BF16)
            lg = _dot(h, wr_ref[...]) + brow_ref[...]
            wgu_lo, _, wgu_hi, _ = w_refs[t]
            tiles.append((lg, e_lo, e_hi, g, _dot(h, wgu_lo[...]), _dot(h, wgu_hi[...])))

        for t, (lg, e_lo, e_hi, g, ab_lo, ab_hi) in enumerate(tiles):
            a_lo, b_lo = ab_lo[:, 0:D_EXPERT], ab_lo[:, D_EXPERT:2 * D_EXPERT]
            a_hi, b_hi = ab_hi[:, 0:D_EXPERT], ab_hi[:, D_EXPERT:2 * D_EXPERT]
            lane = lax.broadcasted_iota(I32, lg.shape, 1)
            is_group = lane < N_EXPERT_GROUPS
            gmax = jnp.max(jnp.where(is_group, lg, NEG_INF), axis=1, keepdims=True)
            ge = jnp.exp(lg - gmax)
            g_w = (jnp.sum(jnp.where(lane == g, ge, 0.0), axis=1, keepdims=True)
                   / jnp.sum(jnp.where(is_group, ge, 0.0), axis=1, keepdims=True))
            v_lo = jnp.sum(jnp.where(lane == EXPERT_ROW0 + e_lo, lg, 0.0), axis=1, keepdims=True)
            v_hi = jnp.sum(jnp.where(lane == EXPERT_ROW0 + e_hi, lg, 0.0), axis=1, keepdims=True)
            m = jnp.maximum(v_lo, v_hi)
            x_lo = jnp.exp(v_lo - m)
            x_hi = jnp.exp(v_hi - m)
            w_lo = x_lo / (x_lo + x_hi) * g_w
            w_hi = x_hi / (x_lo + x_hi) * g_w
            hid_lo = (a_lo * _sigmoid(a_lo) * b_lo * w_lo).astype(BF16)
            hid_hi = (a_hi * _sigmoid(a_hi) * b_hi * w_hi).astype(BF16)
            _, wd_lo, _, wd_hi = w_refs[t]
            y = _dot(hid_lo, wd_lo[...]) + _dot(hid_hi, wd_hi[...])
            if t > 0:
                y = jnp.where(first + t < n_used, y, 0.0)
            for s in range(SLAB):
                o_ref[pl.ds(t * TM_MOE * SLAB + s, TM_MOE, stride=SLAB), :] = y[:, s * 128:(s + 1) * 128]

    @pl.when(first >= n_used)
    def _():
        o_ref[...] = jnp.zeros_like(o_ref)


def _moe_call(plan, n_tiles, h_sorted, wr, br_row, w_gate_up, w_down):
    tm = TM_MOE
    n = MOE_TILES_PER_STEP

    def h_spec(t):
        return pl.BlockSpec((tm * SLAB, 128), lambda s, plan, nt: (plan[PLAN_ROW * NT_MOE + s * n + t], 0))

    def w_spec(shape, section, t):
        return pl.BlockSpec((None,) + shape, lambda s, plan, nt: (plan[section * NT_MOE + s * n + t], 0, 0))

    up = (D_MODEL, 2 * D_EXPERT)
    down = (D_EXPERT, D_MODEL)
    per_tile_weights = [[w_spec(up, PLAN_E_LO, t), w_spec(down, PLAN_E_LO, t),
                         w_spec(up, PLAN_E_HI, t), w_spec(down, PLAN_E_HI, t)] for t in range(n)]
    grid_spec = pltpu.PrefetchScalarGridSpec(
        num_scalar_prefetch=2,
        grid=(NT_MOE // n,),
        in_specs=[h_spec(t) for t in range(n)] + [
            pl.BlockSpec((D_MODEL, 128), lambda s, plan, nt: (0, 0)),
            pl.BlockSpec((1, 128), lambda s, plan, nt: (0, 0)),
        ] + [spec for specs in per_tile_weights for spec in specs],
        out_specs=pl.BlockSpec((n * tm * SLAB, 128), lambda s, plan, nt: (s, 0)),
    )
    weights = [w_gate_up, w_down, w_gate_up, w_down] * n
    return pl.pallas_call(
        _moe_kernel,
        grid_spec=grid_spec,
        out_shape=jax.ShapeDtypeStruct((SORTED_ROWS * SLAB, 128), F32),
        compiler_params=_params(1),
        name="moe",
    )(plan, n_tiles, *([h_sorted] * n), wr, br_row, *weights)


def _alibi_query_lanes():
    lanes = np.zeros((1, N_HEADS * HEAD_SLOT), np.float32)
    for hd, slope in enumerate(_alibi_slopes()):
        rest = np.float32(slope * LOG2E)
        for part in range(ALIBI_PARTS):
            piece = np.float32(np.asarray(rest).astype(jnp.bfloat16))
            lanes[0, hd * HEAD_SLOT + HEAD_DIM + part] = piece
            rest = np.float32(rest - piece)
    return jnp.asarray(lanes)


def _bucket_tables():
    e_lo, e_hi, grp = [], [], []
    for g in range(N_EXPERT_GROUPS):
        for lo, hi in PAIRS:
            e_lo.append(g * EXPERTS_PER_GROUP + lo)
            e_hi.append(g * EXPERTS_PER_GROUP + hi)
            grp.append(g)
    return jnp.asarray(e_lo + e_hi + grp, I32)


def kernel(x, c, w_ada, b_ada, norm1_gain, w_in, b_branch_gate, q_norm_gain, k_norm_gain, attn_sinks, gmlp_norm_gain, gmlp_norm_bias, gmlp_w_spatial, gmlp_b_spatial, w_o_attn, w_o_gmlp, w_out, norm2_gain, w_group_router, b_group_router, w_expert_router, b_expert_router, w_expert_gate, w_expert_up, w_expert_down):
    depth = w_ada.shape[0]
    x2 = x.reshape(TOKENS, D_MODEL)
    tables = _bucket_tables()
    q_aug = _alibi_query_lanes()
    row = lambda v: v.reshape(1, -1)
    for l in range(depth):
        mod = _mod_call(c, w_ada[l], b_ada[l])
        shift1, scale1, gate1, shift2, scale2, gate2 = [
            m.reshape(BATCH, 1, D_MODEL) for m in jnp.split(mod, 6, axis=-1)]

        slot_pad = jnp.zeros((HEAD_SLOT - HEAD_DIM,), F32)
        q_gain_row = row(jnp.tile(jnp.concatenate([q_norm_gain[l] * (HEAD_DIM ** -0.5 * LOG2E), slot_pad]), N_HEADS))
        k_gain_row = row(jnp.tile(jnp.concatenate([k_norm_gain[l], slot_pad]), N_KV_HEADS))
        w_in_b = _round_call(w_in, l)
        q, k, vt, u, vg, gates = _inproj_call(
            x2, scale1, shift1, row(norm1_gain[l]), w_in_b, w_in_b[:, V0:V1].T,
            q_gain_row, k_gain_row, row(gmlp_norm_gain[l]), row(gmlp_norm_bias[l]), row(b_branch_gate[l]), q_aug)

        stack = lambda w: w[l].reshape(N_EXPERTS * w.shape[3], w.shape[4])
        y_attn, wgu_b, wd_b, wo_attn_b, wo_gmlp_b, w_out_b = _attn_call(
            attn_sinks[l], q, k, vt, [stack(w_expert_gate), stack(w_expert_up)], [stack(w_expert_down)],
            [w_o_attn[l]], [w_o_gmlp[l]], [w_out[l]])

        bias_map = jnp.repeat(gmlp_b_spatial[l].T, GMLP_WIDTH // GMLP_GROUPS, axis=1)
        e0, e1 = EXPERT_ROW0, EXPERT_ROW0 + N_EXPERTS
        wr = jnp.zeros((D_MODEL, 128), F32)
        wr = wr.at[:, 0:N_EXPERT_GROUPS].set(w_group_router[l]).at[:, e0:e1].set(w_expert_router[l]).astype(BF16)
        br_row = jnp.zeros((1, 128), F32)
        br_row = br_row.at[0, 0:N_EXPERT_GROUPS].set(b_group_router[l]).at[0, e0:e1].set(b_expert_router[l])
        x1, h2p, buckets, totals = _mixer_call(
            x2, y_attn, u, vg, gates, gmlp_w_spatial[l], bias_map, wo_attn_b, wo_gmlp_b, w_out_b,
            gate1, row(norm2_gain[l]), scale2, shift2,
            wr[:, 0:BUCKET_ROWS].T, br_row[:, 0:BUCKET_ROWS].T)

        pos, meta = _rank_call(buckets, totals)
        pos = pos.reshape(TOKENS // TM_ROW, 1, TM_ROW)
        h_sorted = _scatter_call(pos, meta, h2p)
        per_expert = lambda w: w.reshape(N_EXPERTS, w.shape[0] // N_EXPERTS, w.shape[1])

        n_tiles = meta[META_N_TILES, 0:1]
        y_sorted = _moe_call(
            _tile_plan(meta[META_TILE_BUCKET], n_tiles, tables), n_tiles, h_sorted, wr, br_row,
            per_expert(wgu_b), per_expert(wd_b))

        x2 = _gather_call(pos, y_sorted, x1, gate2)
    return x2.reshape(x.shape)
```
